```python
import math
import jax
import jax.numpy as jnp
from jax import lax
import numpy as np

D_MODEL = 1024
BATCH = 4
SEQ = 4096
DEPTH = 1
DEC_BATCH = 32
DEC_SEQ = 4
PAST_LEN = 8192
PAGE_SIZE = 128

D_MIX = D_MODEL
N_HEADS_ATT = 8
HEAD_DIM = 64
D_ATT = N_HEADS_ATT * HEAD_DIM
N_GROUPS_SGU = 8
SGU_GROUP_DIM = 64
D_SGU = N_GROUPS_SGU * SGU_GROUP_DIM
CHUNK = 128
BRANCHES = ((128, 1), (512, 4), (2048, 16))
WINDOW = max(w for w, _ in BRANCHES)
Q_BLOCK = 128
N_BUCKETS = 32
MAX_DISTANCE = WINDOW
D_FF = -(-8 * D_MODEL // (3 * 256)) * 256
D_IN = 3 * D_ATT + 2 * D_SGU
EPS = 1e-6
NEG_INF = -1e30

kernel_name = "hymba_dilated_sgu_decoder_step"


def _rmsnorm(x, g):
    xf = x.astype(jnp.float32)
    y = xf * lax.rsqrt(jnp.mean(xf * xf, axis=-1, keepdims=True) + EPS)
    return (y * g.astype(jnp.float32)).astype(x.dtype)


def _layernorm(x, g, b):
    xf = x.astype(jnp.float32)
    mu = jnp.mean(xf, axis=-1, keepdims=True)
    var = jnp.mean(jnp.square(xf - mu), axis=-1, keepdims=True)
    y = (xf - mu) * lax.rsqrt(var + EPS)
    return (y * g.astype(jnp.float32) + b.astype(jnp.float32)).astype(x.dtype)


def _rel_bucket(dist):
    max_exact = N_BUCKETS // 2
    df = jnp.maximum(dist, 1).astype(jnp.float32)
    large = max_exact + (jnp.log(df / max_exact) / math.log(MAX_DISTANCE / max_exact)
                         * (N_BUCKETS - max_exact)).astype(jnp.int32)
    large = jnp.minimum(large, N_BUCKETS - 1)
    return jnp.where(dist < max_exact, dist, large)


def _branch_biases(rel_bias):
    out = []
    for w, d in BRANCHES:
        nj = w // d + 1
        dist = jnp.arange(nj, dtype=jnp.int32) * d
        out.append(rel_bias[_rel_bucket(dist)].T.astype(jnp.float32))
    return out


def _heads(t):
    return t.reshape(t.shape[:-1] + (N_HEADS_ATT, HEAD_DIM))


def _project(x, g, w_in):
    z = _rmsnorm(x, g) @ w_in
    return jnp.split(z, [D_ATT, 2 * D_ATT, 3 * D_ATT, 3 * D_ATT + D_SGU], axis=-1)


def _dilated_prompt(q, k, v, bias_hj, d, nj):
    B, S, H, C = q.shape
    L = S // d
    nb = -(-L // Q_BLOCK)
    Lp = nb * Q_BLOCK

    def split(t):
        t = t.reshape(B, L, d, H, C)
        t = jnp.pad(t, ((0, 0), (0, Lp - L), (0, 0), (0, 0), (0, 0)))
        return t.reshape(B, nb, Q_BLOCK, d, H, C)

    def with_prev(t):
        prev = jnp.pad(t, ((0, 0), (1, 0), (0, 0), (0, 0), (0, 0), (0, 0)))[:, :-1]
        return jnp.concatenate([prev, t], axis=2)

    qb = split(q)
    kc = with_prev(split(k))
    vc = with_prev(split(v))
    qi = jnp.arange(Q_BLOCK)[:, None]
    ki = jnp.arange(2 * Q_BLOCK)[None, :]
    j = Q_BLOCK + qi - ki
    valid = (j >= 0) & (j < nj)
    valid = valid[None] & ((jnp.arange(nb)[:, None, None] > 0) | (ki[None] >= Q_BLOCK))
    bias = bias_hj[:, jnp.clip(j, 0, nj - 1)]
    s = jnp.einsum('bnqrhc,bnkrhc->bnrhqk', qb, kc).astype(jnp.float32)
    s = s * (HEAD_DIM ** -0.5) + bias[None, None, None]
    s = jnp.where(valid[None, :, None, None], s, NEG_INF)
    lse = jax.nn.logsumexp(s, axis=-1)
    p = jnp.exp(s - lse[..., None])
    o = jnp.einsum('bnrhqk,bnkrhc->bnqrhc', p.astype(vc.dtype), vc)
    o = o.reshape(B, Lp, d, H, C)[:, :L].reshape(B, S, H, C)
    lse = lse.transpose(0, 1, 4, 2, 3).reshape(B, Lp, d, H)[:, :L].reshape(B, S, H)
    return o, lse


def _dilated_sample(q, k_all, v_all, bias_hj, d, nj, wb):
    T = q.shape[1]
    idx = wb + jnp.arange(T)[:, None] - jnp.arange(nj)[None, :] * d
    valid = idx >= 0
    idxc = jnp.maximum(idx, 0)
    kg = k_all[:, idxc]
    vg = v_all[:, idxc]
    s = jnp.einsum('bthc,btjhc->bthj', q, kg).astype(jnp.float32)
    s = s * (HEAD_DIM ** -0.5) + bias_hj[None, None]
    s = jnp.where(valid[None, :, None, :], s, NEG_INF)
    lse = jax.nn.logsumexp(s, axis=-1)
    p = jnp.exp(s - lse[..., None])
    o = jnp.einsum('bthj,btjhc->bthc', p.astype(vg.dtype), vg)
    return o, lse


def _merge_branches(outs, lses):
    w = jax.nn.softmax(jnp.stack(lses, axis=0), axis=0)
    return jnp.sum(w[..., None] * jnp.stack(outs, axis=0).astype(jnp.float32), axis=0)


def _spatial_gate(vc, w_s, b_s):
    T = vc.shape[-3]
    causal = jnp.tril(jnp.ones((T, T), dtype=w_s.dtype))
    wm = w_s[:, :T, :T] * causal
    return jnp.einsum('gts,...sgc->...tgc', wm, vc) + b_s[:, :T].T[:, :, None]


def _swiglu(x, g, w_gate, w_up, w_down):
    h = _rmsnorm(x, g)
    return (jax.nn.silu(h @ w_gate) * (h @ w_up)) @ w_down


def _layer_prompt(x, biases, norm1_g, w_in, sgu_ln_g, sgu_ln_b, sgu_w, sgu_b, w_out,
                  norm2_g, w_gate, w_up, w_down):
    B, S, _ = x.shape
    q, k, v, u, vg = _project(x, norm1_g, w_in)
    q, k, v = _heads(q), _heads(k), _heads(v)
    outs, lses = [], []
    for (w, d), bias_hj in zip(BRANCHES, biases):
        o, l = _dilated_prompt(q, k, v, bias_hj, d, w // d + 1)
        outs.append(o)
        lses.append(l)
    att = _merge_branches(outs, lses).reshape(B, S, D_ATT).astype(x.dtype)
    vn = _layernorm(vg, sgu_ln_g, sgu_ln_b)
    vn = vn.reshape(B, S // CHUNK, CHUNK, N_GROUPS_SGU, SGU_GROUP_DIM)
    gate = _spatial_gate(vn, sgu_w, sgu_b).reshape(B, S, D_SGU)
    x = x + jnp.concatenate([att, u * gate], axis=-1) @ w_out
    x = x + _swiglu(x, norm2_g, w_gate, w_up, w_down)
    nw = min(WINDOW, S)
    return x, k[:, S - nw:], v[:, S - nw:]


def _layer_sample(x, cache_k, cache_v, biases, norm1_g, w_in, sgu_ln_g, sgu_ln_b, sgu_w,
                  sgu_b, w_out, norm2_g, w_gate, w_up, w_down):
    Bd, T, _ = x.shape
    wb = cache_k.shape[1]
    q, k, v, u, vg = _project(x, norm1_g, w_in)
    q, k, v = _heads(q), _heads(k), _heads(v)
    k_all = jnp.concatenate([cache_k.astype(k.dtype), k], axis=1)
    v_all = jnp.concatenate([cache_v.astype(v.dtype), v], axis=1)
    outs, lses = [], []
    for (w, d), bias_hj in zip(BRANCHES, biases):
        o, l = _dilated_sample(q, k_all, v_all, bias_hj, d, w // d + 1, wb)
        outs.append(o)
        lses.append(l)
    att = _merge_branches(outs, lses).reshape(Bd, T, D_ATT).astype(x.dtype)
    vn = _layernorm(vg, sgu_ln_g, sgu_ln_b)
    gate = _spatial_gate(vn.reshape(Bd, T, N_GROUPS_SGU, SGU_GROUP_DIM), sgu_w, sgu_b)
    gate = gate.reshape(Bd, T, D_SGU)
    x = x + jnp.concatenate([att, u * gate], axis=-1) @ w_out
    x = x + _swiglu(x, norm2_g, w_gate, w_up, w_down)
    keep = wb + T - min(WINDOW, wb + T)
    return x, k_all[:, keep:], v_all[:, keep:], vn


def setup_inputs(seed: int = 0) -> dict:
    key = jax.random.key(seed)
    ks = jax.random.split(key, 20)
    wb = min(WINDOW, PAST_LEN)
    f32 = jnp.float32
    nrm = lambda k, shape: jax.random.normal(k, shape, dtype=f32)
    return {
        "x_prompt": nrm(ks[0], (BATCH, SEQ, D_MODEL)),
        "x_sample": nrm(ks[1], (DEC_BATCH, DEC_SEQ, D_MODEL)),
        "cache_k_win": nrm(ks[2], (DEPTH, DEC_BATCH, wb, N_HEADS_ATT, HEAD_DIM)),
        "cache_v_win": nrm(ks[3], (DEPTH, DEC_BATCH, wb, N_HEADS_ATT, HEAD_DIM)),
        "norm1_g": 1.0 + 0.02 * nrm(ks[4], (DEPTH, D_MODEL)),
        "w_in": nrm(ks[5], (DEPTH, D_MODEL, D_IN)) * D_MODEL ** -0.5,
        "sgu_ln_g": 1.0 + 0.02 * nrm(ks[6], (DEPTH, D_SGU)),
        "sgu_ln_b": 0.02 * nrm(ks[7], (DEPTH, D_SGU)),
        "sgu_w": nrm(ks[8], (DEPTH, N_GROUPS_SGU, CHUNK, CHUNK)) * CHUNK ** -0.5,
        "sgu_b": 1.0 + 0.02 * nrm(ks[9], (DEPTH, N_GROUPS_SGU, CHUNK)),
        "w_out": nrm(ks[10], (DEPTH, D_MIX, D_MODEL)) * D_MIX ** -0.5,
        "norm2_g": 1.0 + 0.02 * nrm(ks[11], (DEPTH, D_MODEL)),
        "w_gate": nrm(ks[12], (DEPTH, D_MODEL, D_FF)) * D_MODEL ** -0.5,
        "w_up": nrm(ks[13], (DEPTH, D_MODEL, D_FF)) * D_MODEL ** -0.5,
        "w_down": nrm(ks[14], (DEPTH, D_FF, D_MODEL)) * D_FF ** -0.5,
        "rel_bias": 0.5 * nrm(ks[15], (N_BUCKETS, N_HEADS_ATT)),
        "final_g": 1.0 + 0.02 * nrm(ks[16], (D_MODEL,)),
    }


def reference(x_prompt, x_sample, cache_k_win, cache_v_win, norm1_g, w_in, sgu_ln_g, sgu_ln_b,
              sgu_w, sgu_b, w_out, norm2_g, w_gate, w_up, w_down, rel_bias, final_g):
    biases = _branch_biases(rel_bias)
    xp, xs = x_prompt, x_sample
    kp, vp, ksm, vsm, svs = [], [], [], [], []
    for l in range(DEPTH):
        lw = (norm1_g[l], w_in[l], sgu_ln_g[l], sgu_ln_b[l], sgu_w[l], sgu_b[l], w_out[l],
              norm2_g[l], w_gate[l], w_up[l], w_down[l])
        xp, k_new, v_new = _layer_prompt(xp, biases, *lw)
        kp.append(k_new)
        vp.append(v_new)
        xs, k_buf, v_buf, sv = _layer_sample(xs, cache_k_win[l], cache_v_win[l], biases, *lw)
        ksm.append(k_buf)
        vsm.append(v_buf)
        svs.append(sv)
    y_prompt = _rmsnorm(xp, final_g)
    y_sample = _rmsnorm(xs, final_g)
    return (y_prompt, y_sample, jnp.stack(kp), jnp.stack(vp), jnp.stack(ksm), jnp.stack(vsm), jnp.stack(svs))
```

```python
import functools
import math

import jax
import jax.numpy as jnp
from jax import lax
from jax.experimental import pallas as pl
from jax.experimental.pallas import tpu as pltpu

D_MODEL = 1024
N_HEADS = 8
HEAD_DIM = 64
D_ATT = N_HEADS * HEAD_DIM
N_GROUPS = 8
D_SGU = 512
CHUNK = 128
BRANCHES = ((128, 1), (512, 4), (2048, 16))
WINDOW = 2048
Q_BLOCK = 128
N_BUCKETS = 32
MAX_DISTANCE = WINDOW
D_FF = 2816
EPS = 1e-6
NEG_INF = -1e30

LANES = 128
FF_CHUNK = 256
N_FF_CHUNKS = D_FF // FF_CHUNK
VMEM_LIMIT = 48 * 1024 * 1024

F32 = jnp.float32
BF16 = jnp.bfloat16


def _const_spec(shape):
    nd = len(shape)
    return pl.BlockSpec(shape, lambda *_: (0,) * nd, pipeline_mode=pl.Buffered(1))


def _proj_kernel(x_ref, g1_ref, w_ref, lng_ref, lnb_ref, mix_ref, mixb_ref, *outs,
                 tm, emit_vn, emit_t, tiles_per_seq, win_tiles):
    q_ref, k_ref, v_ref, sg_ref = outs[:4]
    rest = outs[4:]
    x = x_ref[...]
    ms = jnp.mean(x * x, axis=-1, keepdims=True)
    xn = (x * lax.rsqrt(ms + EPS) * g1_ref[...]).astype(BF16)

    def proj(c0):
        return jnp.dot(xn, w_ref[:, c0:c0 + D_ATT], preferred_element_type=F32)

    q_ref[...] = proj(0)
    k = proj(D_ATT)
    v = proj(2 * D_ATT)
    k_ref[...] = k
    v_ref[...] = v
    u = proj(3 * D_ATT)
    vg = proj(3 * D_ATT + D_SGU)

    mu = jnp.mean(vg, axis=-1, keepdims=True)
    dv = vg - mu
    var = jnp.mean(dv * dv, axis=-1, keepdims=True)
    vn = dv * lax.rsqrt(var + EPS) * lng_ref[...] + lnb_ref[...]
    if emit_vn:
        rest[0][...] = vn

    lane = lax.broadcasted_iota(jnp.int32, (1, LANES), 1)
    low = lane < HEAD_DIM
    for c in range(tm // CHUNK):
        r0 = c * CHUNK
        for s in range(D_SGU // LANES):
            slab = vn[r0:r0 + CHUNK, s * LANES:(s + 1) * LANES]
            lo = jnp.where(low, slab, 0.0).astype(BF16)
            hi = jnp.where(low, 0.0, slab).astype(BF16)
            gate = (jnp.dot(mix_ref[2 * s], lo, preferred_element_type=F32)
                    + jnp.dot(mix_ref[2 * s + 1], hi, preferred_element_type=F32)
                    + mixb_ref[:, s * LANES:(s + 1) * LANES])
            sg_ref[r0:r0 + CHUNK, s * LANES:(s + 1) * LANES] = (
                u[r0:r0 + CHUNK, s * LANES:(s + 1) * LANES] * gate).astype(BF16)

    if emit_t:
        kt_ref, vt_ref = rest[-2:]
        j = pl.program_id(0) % tiles_per_seq

        @pl.when(j >= tiles_per_seq - win_tiles)
        def _():
            kt_ref[0] = k.T
            vt_ref[0] = v.T


def _proj(x, g1, w_in, ln_g, ln_b, mix, mixb, *, tm, emit_vn, emit_t, seq_len=None):
    t = x.shape[0]
    n_tiles = t // tm
    row = lambda shape: pl.BlockSpec(shape, lambda i: (i, 0))
    out_shape = [jax.ShapeDtypeStruct((t, D_ATT), F32)] * 3 + [jax.ShapeDtypeStruct((t, D_SGU), BF16)]
    out_specs = [row((tm, D_ATT))] * 3 + [row((tm, D_SGU))]
    tiles_per_seq = win_tiles = 0
    if emit_vn:
        out_shape.append(jax.ShapeDtypeStruct((t, D_SGU), F32))
        out_specs.append(row((tm, D_SGU)))
    if emit_t:
        tiles_per_seq = seq_len // tm
        win = min(WINDOW, seq_len)
        win_tiles = win // tm
        first = tiles_per_seq - win_tiles

        def t_map(i):
            return (i // tiles_per_seq, 0, jnp.maximum(i % tiles_per_seq - first, 0))

        for _ in range(2):
            out_shape.append(jax.ShapeDtypeStruct((t // seq_len, D_ATT, win), F32))
            out_specs.append(pl.BlockSpec((1, D_ATT, tm), t_map))
    kern = functools.partial(_proj_kernel, tm=tm, emit_vn=emit_vn, emit_t=emit_t,
                             tiles_per_seq=tiles_per_seq, win_tiles=win_tiles)
    return pl.pallas_call(
        kern,
        grid=(n_tiles,),
        in_specs=[row((tm, D_MODEL)), _const_spec(g1.shape), _const_spec(w_in.shape),
                  _const_spec(ln_g.shape), _const_spec(ln_b.shape), _const_spec(mix.shape),
                  _const_spec(mixb.shape)],
        out_specs=out_specs,
        out_shape=out_shape,
        compiler_params=pltpu.CompilerParams(dimension_semantics=("arbitrary",),
                                             vmem_limit_bytes=VMEM_LIMIT),
        name="proj_t" if emit_t else "proj_s",
    )(x, g1, w_in, ln_g, ln_b, mix, mixb)


def _attn_kernel(q_ref, k_ref, v_ref, bias_ref, o_ref, acc_ref, m_ref, l_ref, *, seq_len):
    lane = lax.broadcasted_iota(jnp.int32, (1, LANES), 1)
    low = lane < HEAD_DIM
    contract_last = (((1,), (1,)), ((), ()))

    def rows(start, n, d):
        return pl.ds(start, n) if d == 1 else pl.ds(start, n, stride=d)

    def block(br, d, start_q, first):
        nk = Q_BLOCK if first else 2 * Q_BLOCK
        start_k = start_q if first else start_q - Q_BLOCK * d
        qi = rows(start_q, Q_BLOCK, d)
        ki = rows(start_k, nk, d)
        qb = q_ref[0, qi, :] * (HEAD_DIM ** -0.5)
        kb = k_ref[0, ki, :].astype(BF16)
        vb = v_ref[0, ki, :]
        o = m_vec = l_vec = None
        for h in range(2):
            hm = low if h == 0 else jnp.logical_not(low)
            qh = jnp.where(hm, qb, 0.0).astype(BF16)
            s = lax.dot_general(qh, kb, contract_last, preferred_element_type=F32)
            if first:
                s = s + bias_ref[br, h, :, Q_BLOCK:]
            else:
                s = s + bias_ref[br, h]
            m = jnp.max(s, axis=-1, keepdims=True)
            p = jnp.exp(s - m)
            l = jnp.sum(p, axis=-1, keepdims=True)
            vh = jnp.where(hm, vb, 0.0).astype(BF16)
            oh = jnp.dot(p.astype(BF16), vh, preferred_element_type=F32)
            if h == 0:
                o, m_vec, l_vec = oh, m, l
            else:
                o = o + oh
                m_vec = jnp.where(low, m_vec, m)
                l_vec = jnp.where(low, l_vec, l)
        if br == 0:
            acc_ref[qi, :] = o
            m_ref[qi, :] = m_vec
            l_ref[qi, :] = l_vec
        else:
            m_old = m_ref[qi, :]
            m_new = jnp.maximum(m_old, m_vec)
            a = jnp.exp(m_old - m_new)
            b = jnp.exp(m_vec - m_new)
            acc_ref[qi, :] = acc_ref[qi, :] * a + o * b
            l_ref[qi, :] = l_ref[qi, :] * a + l_vec * b
            m_ref[qi, :] = m_new

    for br, (w, d) in enumerate(BRANCHES):
        nb = seq_len // d // Q_BLOCK

        def residue(r, carry, br=br, d=d, nb=nb):
            block(br, d, r, True)

            def body(n, c):
                block(br, d, r + n * (Q_BLOCK * d), False)
                return c

            return lax.fori_loop(1, nb, body, carry)

        lax.fori_loop(0, d, residue, 0)

    rows_per = 512

    def fin(i, c):
        sl = pl.ds(pl.multiple_of(i * rows_per, rows_per), rows_per)
        o_ref[0, sl, :] = (acc_ref[sl, :] / l_ref[sl, :]).astype(o_ref.dtype)
        return c

    lax.fori_loop(0, seq_len // rows_per, fin, 0)


def _attn(q, k, v, bias_tab):
    b, s, _ = q.shape
    blk = pl.BlockSpec((1, s, LANES), lambda i, j: (i, 0, j))
    return pl.pallas_call(
        functools.partial(_attn_kernel, seq_len=s),
        grid=(b, D_ATT // LANES),
        in_specs=[blk, blk, blk,
                  pl.BlockSpec((len(BRANCHES), 2, Q_BLOCK, 2 * Q_BLOCK), lambda i, j: (0, j, 0, 0))],
        out_specs=blk,
        out_shape=jax.ShapeDtypeStruct((b, s, D_ATT), BF16),
        scratch_shapes=[pltpu.VMEM((s, LANES), F32)] * 3,
        compiler_params=pltpu.CompilerParams(dimension_semantics=("arbitrary", "arbitrary"),
                                             vmem_limit_bytes=VMEM_LIMIT),
        name="attn",
    )(q, k, v, bias_tab)


def _sattn_kernel(q_ref, kt_ref, vt_ref, ktn_ref, vtn_ref, logc_ref, logcn_ref,
                  att_ref, kto_ref, vto_ref, *, wb, t_new):
    rows = t_new * N_HEADS
    contract_last = (((1,), (1,)), ((), ()))
    q = q_ref[0] * (HEAD_DIM ** -0.5)
    qrep = jnp.broadcast_to(q[:, None, :], (t_new, N_HEADS, D_ATT)).reshape(rows, D_ATT)
    row_h = lax.broadcasted_iota(jnp.int32, (rows, D_ATT), 0) % N_HEADS
    col_h = lax.broadcasted_iota(jnp.int32, (rows, D_ATT), 1) // HEAD_DIM
    own = row_h == col_h
    qbd = jnp.where(own, qrep, 0.0).astype(BF16)

    s = jnp.dot(qbd, kt_ref[0].astype(BF16), preferred_element_type=F32) + logc_ref[...]
    sn = jnp.dot(qbd, ktn_ref[0].astype(BF16), preferred_element_type=F32) + logcn_ref[...]
    m = jnp.maximum(jnp.max(s, axis=-1, keepdims=True), jnp.max(sn, axis=-1, keepdims=True))
    p = jnp.exp(s - m)
    pn = jnp.exp(sn - m)
    l = jnp.sum(p, axis=-1, keepdims=True) + jnp.sum(pn, axis=-1, keepdims=True)
    o = (lax.dot_general(p.astype(BF16), vt_ref[0].astype(BF16), contract_last,
                         preferred_element_type=F32)
         + lax.dot_general(pn.astype(BF16), vtn_ref[0].astype(BF16), contract_last,
                           preferred_element_type=F32))
    o = jnp.where(own, o, 0.0) / l
    att_ref[0] = jnp.sum(o.reshape(t_new, N_HEADS, D_ATT), axis=1)

    lane = lax.broadcasted_iota(jnp.int32, (1, LANES), 1)
    keep = lane < LANES - t_new
    n_tiles = wb // LANES
    for src, new, dst in ((kt_ref, ktn_ref, kto_ref), (vt_ref, vtn_ref, vto_ref)):
        cur = pltpu.roll(src[0, :, 0:LANES], LANES - t_new, 1)
        for j in range(n_tiles):
            if j + 1 < n_tiles:
                nxt = pltpu.roll(src[0, :, (j + 1) * LANES:(j + 2) * LANES], LANES - t_new, 1)
            else:
                nxt = new[0]
            dst[0, :, j * LANES:(j + 1) * LANES] = jnp.where(keep, cur, nxt)
            cur = nxt


def _sattn(q, kt, vt, ktn, vtn, logc, logcn):
    bd, t_new, _ = q.shape
    wb = kt.shape[-1]
    big = pl.BlockSpec((1, D_ATT, wb), lambda i: (i, 0, 0))
    tile = pl.BlockSpec((1, D_ATT, LANES), lambda i: (i, 0, 0))
    qspec = pl.BlockSpec((1, t_new, D_ATT), lambda i: (i, 0, 0))
    return pl.pallas_call(
        functools.partial(_sattn_kernel, wb=wb, t_new=t_new),
        grid=(bd,),
        in_specs=[qspec, big, big, tile, tile, _const_spec(logc.shape), _const_spec(logcn.shape)],
        out_specs=[qspec, big, big],
        out_shape=[jax.ShapeDtypeStruct((bd, t_new, D_ATT), F32),
                   jax.ShapeDtypeStruct(kt.shape, F32), jax.ShapeDtypeStruct(vt.shape, F32)],
        compiler_params=pltpu.CompilerParams(dimension_semantics=("arbitrary",),
                                             vmem_limit_bytes=VMEM_LIMIT),
        name="sattn",
    )(q, kt, vt, ktn, vtn, logc, logcn)


def _ffn_kernel(x_ref, att_ref, sg_ref, woa_ref, wos_ref, g2_ref, wg_ref, wu_ref, wd_ref, gf_ref,
                y_ref, h_ref, acc_ref):
    x1 = (x_ref[...]
          + jnp.dot(att_ref[...], woa_ref[...], preferred_element_type=F32)
          + jnp.dot(sg_ref[...], wos_ref[...], preferred_element_type=F32))
    ms = jnp.mean(x1 * x1, axis=-1, keepdims=True)
    h_ref[...] = (x1 * lax.rsqrt(ms + EPS) * g2_ref[...]).astype(BF16)
    acc_ref[...] = x1

    def body(c, carry):
        h = h_ref[...]
        g = jnp.dot(h, wg_ref[c], preferred_element_type=F32)
        u = jnp.dot(h, wu_ref[c], preferred_element_type=F32)
        a = (g * jax.nn.sigmoid(g) * u).astype(BF16)
        acc_ref[...] += jnp.dot(a, wd_ref[c], preferred_element_type=F32)
        return carry

    lax.fori_loop(0, N_FF_CHUNKS, body, 0)
    x2 = acc_ref[...]
    ms2 = jnp.mean(x2 * x2, axis=-1, keepdims=True)
    y_ref[...] = x2 * lax.rsqrt(ms2 + EPS) * gf_ref[...]


def _ffn(x, att, sg, woa, wos, g2, wg, wu, wd, gf, *, tm):
    t = x.shape[0]
    row = lambda shape: pl.BlockSpec(shape, lambda i: (i, 0))
    return pl.pallas_call(
        _ffn_kernel,
        grid=(t // tm,),
        in_specs=[row((tm, D_MODEL)), row((tm, D_ATT)), row((tm, D_SGU)),
                  _const_spec(woa.shape), _const_spec(wos.shape), _const_spec(g2.shape),
                  _const_spec(wg.shape), _const_spec(wu.shape), _const_spec(wd.shape),
                  _const_spec(gf.shape)],
        out_specs=row((tm, D_MODEL)),
        out_shape=jax.ShapeDtypeStruct((t, D_MODEL), F32),
        scratch_shapes=[pltpu.VMEM((tm, D_MODEL), BF16), pltpu.VMEM((tm, D_MODEL), F32)],
        compiler_params=pltpu.CompilerParams(dimension_semantics=("arbitrary",),
                                             vmem_limit_bytes=VMEM_LIMIT),
        name="ffn",
    )(x, att, sg, woa, wos, g2, wg, wu, wd, gf)


def _rel_bucket(dist):
    max_exact = N_BUCKETS // 2
    df = jnp.maximum(dist, 1).astype(F32)
    large = max_exact + (jnp.log(df / max_exact) / math.log(MAX_DISTANCE / max_exact)
                         * (N_BUCKETS - max_exact)).astype(jnp.int32)
    large = jnp.minimum(large, N_BUCKETS - 1)
    return jnp.where(dist < max_exact, dist, large)


def _branch_bias(rel_bias, w, d):
    nj = w // d + 1
    dist = jnp.arange(nj, dtype=jnp.int32) * d
    return rel_bias[_rel_bucket(dist)].T.astype(F32)


def _prompt_bias_table(rel_bias):
    qi = jnp.arange(Q_BLOCK)[:, None]
    ki = jnp.arange(2 * Q_BLOCK)[None, :]
    j = Q_BLOCK + qi - ki
    tabs = []
    for w, d in BRANCHES:
        nj = w // d + 1
        b = _branch_bias(rel_bias, w, d)[:, jnp.clip(j, 0, nj - 1)]
        tabs.append(jnp.where(((j >= 0) & (j < nj))[None], b, NEG_INF))
    return jnp.stack(tabs)


def _sample_bias_table(rel_bias, wb, t_new):
    r = jnp.concatenate([jnp.arange(wb), jnp.full((LANES - t_new,), -1), wb + jnp.arange(t_new)])
    t = jnp.arange(t_new)[:, None]
    dist = wb + t - r[None, :]
    real = (r >= 0)[None, :]
    terms, valids = [], []
    for w, d in BRANCHES:
        nj = w // d + 1
        j = dist // d
        ok = real & (dist >= 0) & (dist % d == 0) & (j < nj)
        b = _branch_bias(rel_bias, w, d)[:, jnp.clip(j, 0, nj - 1)]
        terms.append(b)
        valids.append(jnp.broadcast_to(ok[None], b.shape))
    terms = jnp.stack(terms)
    valids = jnp.stack(valids)
    mx = jnp.max(jnp.where(valids, terms, NEG_INF), axis=0)
    tot = jnp.sum(jnp.where(valids, jnp.exp(terms - mx[None]), 0.0), axis=0)
    logc = jnp.where(tot > 0, mx + jnp.log(jnp.maximum(tot, 1e-30)), NEG_INF)
    logc = logc.transpose(1, 0, 2).reshape(t_new * N_HEADS, -1)
    return logc[:, :wb], logc[:, wb:]


def kernel(x_prompt, x_sample, cache_k_win, cache_v_win, norm1_g, w_in, sgu_ln_g, sgu_ln_b,
           sgu_w, sgu_b, w_out, norm2_g, w_gate, w_up, w_down, rel_bias, final_g):
    depth = w_in.shape[0]
    assert depth == 1
    b, s, _ = x_prompt.shape
    bd, t_new, _ = x_sample.shape
    wb = cache_k_win.shape[2]
    assert bd * t_new == CHUNK and s % (Q_BLOCK * 16) == 0 and wb % LANES == 0

    l = 0
    w_in_b = w_in[l].astype(BF16)
    woa = w_out[l, :D_ATT].astype(BF16)
    wos = w_out[l, D_ATT:].astype(BF16)
    wg = w_gate[l].reshape(D_MODEL, N_FF_CHUNKS, FF_CHUNK).transpose(1, 0, 2).astype(BF16)
    wu = w_up[l].reshape(D_MODEL, N_FF_CHUNKS, FF_CHUNK).transpose(1, 0, 2).astype(BF16)
    wd = w_down[l].reshape(N_FF_CHUNKS, FF_CHUNK, D_MODEL).astype(BF16)
    g1 = norm1_g[l][None]
    g2 = norm2_g[l][None]
    gf = final_g[None]
    ln_g = sgu_ln_g[l][None]
    ln_b = sgu_ln_b[l][None]

    causal = jnp.tril(jnp.ones((CHUNK, CHUNK), F32))
    wm = sgu_w[l] * causal
    mix_p = wm.astype(BF16)
    mixb_p = jnp.repeat(sgu_b[l].T, HEAD_DIM, axis=1)
    eye = jnp.eye(bd, dtype=F32)
    mix_s = jnp.einsum('ab,gts->gatbs', eye, wm[:, :t_new, :t_new]).reshape(
        N_GROUPS, CHUNK, CHUNK).astype(BF16)
    mixb_s = jnp.tile(mixb_p[:t_new], (bd, 1))

    xp = x_prompt.reshape(b * s, D_MODEL)
    q, k, v, sg, kt_p, vt_p = _proj(xp, g1, w_in_b, ln_g, ln_b, mix_p, mixb_p,
                                    tm=512, emit_vn=False, emit_t=True, seq_len=s)
    att = _attn(q.reshape(b, s, D_ATT), k.reshape(b, s, D_ATT), v.reshape(b, s, D_ATT),
                _prompt_bias_table(rel_bias))
    y_prompt = _ffn(xp, att.reshape(b * s, D_ATT), sg, woa, wos, g2, wg, wu, wd, gf,
                    tm=512).reshape(b, s, D_MODEL)
    nw = min(WINDOW, s)
    new_k_p = kt_p.reshape(1, b, N_HEADS, HEAD_DIM, nw).transpose(0, 1, 4, 2, 3)
    new_v_p = vt_p.reshape(1, b, N_HEADS, HEAD_DIM, nw).transpose(0, 1, 4, 2, 3)

    xs = x_sample.reshape(bd * t_new, D_MODEL)
    qs, ks, vs, sgs, vn_s = _proj(xs, g1, w_in_b, ln_g, ln_b, mix_s, mixb_s,
                                  tm=CHUNK, emit_vn=True, emit_t=False)
    kt_c = cache_k_win[l].transpose(0, 2, 3, 1).reshape(bd, D_ATT, wb)
    vt_c = cache_v_win[l].transpose(0, 2, 3, 1).reshape(bd, D_ATT, wb)

    def new_tile(z):
        zt = z.reshape(bd, t_new, D_ATT).transpose(0, 2, 1)
        return jnp.pad(zt, ((0, 0), (0, 0), (LANES - t_new, 0)))

    logc, logcn = _sample_bias_table(rel_bias, wb, t_new)
    att_s, kt_o, vt_o = _sattn(qs.reshape(bd, t_new, D_ATT), kt_c, vt_c, new_tile(ks), new_tile(vs),
                               logc, logcn)
    y_sample = _ffn(xs, att_s.reshape(bd * t_new, D_ATT).astype(BF16), sgs, woa, wos, g2, wg, wu,
                    wd, gf, tm=CHUNK).reshape(bd, t_new, D_MODEL)
    new_k_s = kt_o.reshape(1, bd, N_HEADS, HEAD_DIM, wb).transpose(0, 1, 4, 2, 3)
    new_v_s = vt_o.reshape(1, bd, N_HEADS, HEAD_DIM, wb).transpose(0, 1, 4, 2, 3)
    sgu_v = vn_s.reshape(1, bd, t_new, D_SGU)

    return (y_prompt, y_sample, new_k_p, new_v_p, new_k_s, new_v_s, sgu_v)
```

```python
import functools
import math

import jax
import jax.numpy as jnp
from jax import lax
from jax.experimental import pallas as pl
from jax.experimental.pallas import tpu as pltpu

D_MODEL = 1024
N_HEADS = 8
HEAD_DIM = 64
D_ATT = N_HEADS * HEAD_DIM
N_GROUPS = 8
D_SGU = 512
CHUNK = 128
BRANCHES = ((128, 1), (512, 4), (2048, 16))
WINDOW = 2048
Q_BLOCK = 128
N_BUCKETS = 32
MAX_DISTANCE = WINDOW
D_FF = 2816
EPS = 1e-6
NEG_INF = -1e30
LOG2E = math.log2(math.e)

LANES = 128
FF_CHUNK = 256
N_FF_CHUNKS = D_FF // FF_CHUNK
VMEM_LIMIT = 48 * 1024 * 1024
ATTN_GROUP = 4
PREP_ROWS = 256

F32 = jnp.float32
BF16 = jnp.bfloat16


def _const_spec(shape):
    nd = len(shape)
    return pl.BlockSpec(shape, lambda *_: (0,) * nd, pipeline_mode=pl.Buffered(1))


def _proj_kernel(x_ref, g1_ref, w_ref, lng_ref, lnb_ref, mix_ref, mixb_ref, *outs,
                 tm, emit_vn, emit_t, tiles_per_seq, win_tiles):
    q_ref, k_ref, v_ref, sg_ref = outs[:4]
    rest = outs[4:]
    x = x_ref[...]
    ms = jnp.mean(x * x, axis=-1, keepdims=True)
    xn = (x * lax.rsqrt(ms + EPS) * g1_ref[...]).astype(BF16)

    def proj(c0):
        return jnp.dot(xn, w_ref[:, c0:c0 + D_ATT], preferred_element_type=F32)

    q_ref[...] = proj(0)
    k = proj(D_ATT)
    v = proj(2 * D_ATT)
    k_ref[...] = k
    v_ref[...] = v
    u = proj(3 * D_ATT)
    vg = proj(3 * D_ATT + D_SGU)

    mu = jnp.mean(vg, axis=-1, keepdims=True)
    dv = vg - mu
    var = jnp.mean(dv * dv, axis=-1, keepdims=True)
    vn = dv * lax.rsqrt(var + EPS) * lng_ref[...] + lnb_ref[...]
    if emit_vn:
        rest[0][...] = vn

    lane = lax.broadcasted_iota(jnp.int32, (1, LANES), 1)
    low = lane < HEAD_DIM
    for c in range(tm // CHUNK):
        r0 = c * CHUNK
        for s in range(D_SGU // LANES):
            slab = vn[r0:r0 + CHUNK, s * LANES:(s + 1) * LANES]
            lo = jnp.where(low, slab, 0.0).astype(BF16)
            hi = jnp.where(low, 0.0, slab).astype(BF16)
            gate = (jnp.dot(mix_ref[2 * s], lo, preferred_element_type=F32)
                    + jnp.dot(mix_ref[2 * s + 1], hi, preferred_element_type=F32)
                    + mixb_ref[:, s * LANES:(s + 1) * LANES])
            sg_ref[r0:r0 + CHUNK, s * LANES:(s + 1) * LANES] = (
                u[r0:r0 + CHUNK, s * LANES:(s + 1) * LANES] * gate).astype(BF16)

    if emit_t:
        kt_ref, vt_ref = rest[-2:]
        j = pl.program_id(0) % tiles_per_seq

        @pl.when(j >= tiles_per_seq - win_tiles)
        def _():
            kt_ref[0] = k.T
            vt_ref[0] = v.T


def _proj(x, g1, w_in, ln_g, ln_b, mix, mixb, *, tm, emit_vn, emit_t, seq_len=None):
    t = x.shape[0]
    n_tiles = t // tm
    row = lambda shape: pl.BlockSpec(shape, lambda i: (i, 0))
    out_shape = [jax.ShapeDtypeStruct((t, D_ATT), F32)] * 3 + [jax.ShapeDtypeStruct((t, D_SGU), BF16)]
    out_specs = [row((tm, D_ATT))] * 3 + [row((tm, D_SGU))]
    tiles_per_seq = win_tiles = 0
    if emit_vn:
        out_shape.append(jax.ShapeDtypeStruct((t, D_SGU), F32))
        out_specs.append(row((tm, D_SGU)))
    if emit_t:
        tiles_per_seq = seq_len // tm
        win = min(WINDOW, seq_len)
        win_tiles = win // tm
        first = tiles_per_seq - win_tiles

        def t_map(i):
            return (i // tiles_per_seq, 0, jnp.maximum(i % tiles_per_seq - first, 0))

        for _ in range(2):
            out_shape.append(jax.ShapeDtypeStruct((t // seq_len, D_ATT, win), F32))
            out_specs.append(pl.BlockSpec((1, D_ATT, tm), t_map))
    kern = functools.partial(_proj_kernel, tm=tm, emit_vn=emit_vn, emit_t=emit_t,
                             tiles_per_seq=tiles_per_seq, win_tiles=win_tiles)
    return pl.pallas_call(
        kern,
        grid=(n_tiles,),
        in_specs=[row((tm, D_MODEL)), _const_spec(g1.shape), _const_spec(w_in.shape),
                  _const_spec(ln_g.shape), _const_spec(ln_b.shape), _const_spec(mix.shape),
                  _const_spec(mixb.shape)],
        out_specs=out_specs,
        out_shape=out_shape,
        compiler_params=pltpu.CompilerParams(dimension_semantics=("arbitrary",),
                                             vmem_limit_bytes=VMEM_LIMIT),
        name="proj_t" if emit_t else "proj_s",
    )(x, g1, w_in, ln_g, ln_b, mix, mixb)


def _div_pow2(x, n):
    assert n & (n - 1) == 0
    return lax.shift_right_logical(x, n.bit_length() - 1)


def _mod_pow2(x, n):
    assert n & (n - 1) == 0
    return lax.bitwise_and(x, n - 1)


def _attn_kernel(q_ref, k_ref, v_ref, r0_ref, o_ref, tab_ref, acc_ref, m_ref, l_ref,
                 q0_ref, q1_ref, kd_ref, v0_ref, v1_ref, s_ref, mx_ref, *, seq_len):
    hp = pl.program_id(1)
    lane = lax.broadcasted_iota(jnp.int32, (1, LANES), 1)
    low = lane < HEAD_DIM
    contract_last = (((1,), (1,)), ((), ()))
    neg = NEG_INF * LOG2E

    @pl.when((pl.program_id(0) == 0) & (hp == 0))
    def _():
        col = lax.broadcasted_iota(jnp.int32, (Q_BLOCK, 2 * Q_BLOCK), 1)
        for br in range(len(BRANCHES)):
            for h in range(N_HEADS):
                base = jnp.broadcast_to(r0_ref[br, h:h + 1, :], (Q_BLOCK, 2 * Q_BLOCK))
                t = pltpu.roll(base, 0, 1, stride=1, stride_axis=0)
                tab_ref[br, 0, h] = t
                tab_ref[br, 1, h] = jnp.where(col >= Q_BLOCK, t, neg)
        zeros = jnp.zeros((Q_BLOCK, LANES), BF16)
        kd_ref[0:Q_BLOCK, :] = zeros
        v0_ref[0:Q_BLOCK, :] = zeros
        v1_ref[0:Q_BLOCK, :] = zeros

    rows_per = 512

    def init(i, c):
        sl = pl.ds(pl.multiple_of(i * rows_per, rows_per), rows_per)
        acc_ref[sl, :] = jnp.zeros((rows_per, LANES), F32)
        l_ref[sl, :] = jnp.zeros((rows_per, LANES), F32)
        m_ref[sl, :] = jnp.full((rows_per, LANES), neg, F32)
        return c

    lax.fori_loop(0, seq_len // rows_per, init, 0)

    def strided(start, n, d):
        return pl.ds(start, n) if d == 1 else pl.ds(start, n, stride=d)

    for br, (w, d) in enumerate(BRANCHES):
        sub_len = seq_len // d
        nb = sub_len // Q_BLOCK

        def prep(c, carry, d=d, sub_len=sub_len):
            de0 = c * PREP_ROWS
            src = strided(_div_pow2(de0, sub_len) + _mod_pow2(de0, sub_len) * d, PREP_ROWS, d)
            dst = pl.ds(pl.multiple_of(de0, PREP_ROWS), PREP_ROWS)
            dstp = pl.ds(pl.multiple_of(de0 + Q_BLOCK, Q_BLOCK), PREP_ROWS)
            qs = q_ref[0, src, :] * (HEAD_DIM ** -0.5 * LOG2E)
            q0_ref[dst, :] = jnp.where(low, qs, 0.0).astype(BF16)
            q1_ref[dst, :] = jnp.where(low, 0.0, qs).astype(BF16)
            kd_ref[dstp, :] = k_ref[0, src, :].astype(BF16)
            vv = v_ref[0, src, :]
            v0_ref[dstp, :] = jnp.where(low, vv, 1.0).astype(BF16)
            v1_ref[dstp, :] = jnp.where(low, 1.0, vv).astype(BF16)
            return carry

        lax.fori_loop(0, seq_len // PREP_ROWS, prep, 0)

        def score(i, br=br, nb=nb):
            for u in range(ATTN_GROUP):
                g = i * ATTN_GROUP + u
                rq = pl.ds(pl.multiple_of(g * Q_BLOCK, Q_BLOCK), Q_BLOCK)
                rk = pl.ds(pl.multiple_of(g * Q_BLOCK, Q_BLOCK), 2 * Q_BLOCK)
                first = jnp.where(_mod_pow2(g, nb) == 0, 1, 0)
                kb = kd_ref[rk, :]
                for h, qh_ref in enumerate((q0_ref, q1_ref)):
                    s = lax.dot_general(qh_ref[rq, :], kb, contract_last,
                                        preferred_element_type=F32)
                    s = s + tab_ref[br, first, 2 * hp + h]
                    s_ref[u, h] = s
                    mx_ref[u, h] = jnp.broadcast_to(jnp.max(s, axis=-1, keepdims=True),
                                                    (Q_BLOCK, LANES))

        def finish(i, d=d, nb=nb):
            merged = []
            for u in range(ATTN_GROUP):
                g = i * ATTN_GROUP + u
                rk = pl.ds(pl.multiple_of(g * Q_BLOCK, Q_BLOCK), 2 * Q_BLOCK)
                outs = []
                for h, vh_ref in enumerate((v0_ref, v1_ref)):
                    m = mx_ref[u, h]
                    p = jnp.concatenate([jnp.exp2(s_ref[u, h, :, :LANES] - m),
                                         jnp.exp2(s_ref[u, h, :, LANES:] - m)], axis=1)
                    outs.append((jnp.dot(p.astype(BF16), vh_ref[rk, :],
                                         preferred_element_type=F32), m))
                (o0, m0), (o1, m1) = outs
                o_blk = jnp.where(low, o0, o1)
                l_blk = pltpu.roll(jnp.where(low, o1, o0), HEAD_DIM, 1)
                m_blk = jnp.where(low, m0, m1)
                idx = strided(_div_pow2(g, nb) + _mod_pow2(g, nb) * (Q_BLOCK * d), Q_BLOCK, d)
                m_old = m_ref[idx, :]
                m_new = jnp.maximum(m_old, m_blk)
                a = jnp.exp2(m_old - m_new)
                b = jnp.exp2(m_blk - m_new)
                merged.append((idx, acc_ref[idx, :] * a + o_blk * b,
                               l_ref[idx, :] * a + l_blk * b, m_new))
            for idx, acc_new, l_new, m_new in merged:
                acc_ref[idx, :] = acc_new
                l_ref[idx, :] = l_new
                m_ref[idx, :] = m_new

        def step(i, carry):
            finish(i - 1)
            score(i)
            return carry

        n_groups = seq_len // Q_BLOCK // ATTN_GROUP
        score(0)
        lax.fori_loop(1, n_groups, step, 0)
        finish(n_groups - 1)

    def fin(i, c):
        sl = pl.ds(pl.multiple_of(i * rows_per, rows_per), rows_per)
        o_ref[0, sl, :] = (acc_ref[sl, :] / l_ref[sl, :]).astype(o_ref.dtype)
        return c

    lax.fori_loop(0, seq_len // rows_per, fin, 0)


def _attn(q, k, v, r0):
    b, s, _ = q.shape
    blk = pl.BlockSpec((1, s, LANES), lambda i, j: (i, 0, j))
    state = pltpu.VMEM((s, LANES), F32)
    qd = pltpu.VMEM((s, LANES), BF16)
    kd = pltpu.VMEM((s + Q_BLOCK, LANES), BF16)
    return pl.pallas_call(
        functools.partial(_attn_kernel, seq_len=s),
        grid=(b, D_ATT // LANES),
        in_specs=[blk, blk, blk, _const_spec(r0.shape)],
        out_specs=blk,
        out_shape=jax.ShapeDtypeStruct((b, s, D_ATT), BF16),
        scratch_shapes=[pltpu.VMEM((len(BRANCHES), 2, N_HEADS, Q_BLOCK, 2 * Q_BLOCK), F32),
                        state, state, state, qd, qd, kd, kd, kd,
                        pltpu.VMEM((ATTN_GROUP, 2, Q_BLOCK, 2 * Q_BLOCK), F32),
                        pltpu.VMEM((ATTN_GROUP, 2, Q_BLOCK, LANES), F32)],
        compiler_params=pltpu.CompilerParams(dimension_semantics=("arbitrary", "arbitrary"),
                                             vmem_limit_bytes=VMEM_LIMIT),
        name="attn",
    )(q, k, v, r0)


def _sattn_kernel(q_ref, kt_ref, vt_ref, ktn_ref, vtn_ref, logc_ref,
                  att_ref, kto_ref, vto_ref, *, wb, t_new):
    rows = t_new * N_HEADS
    contract_last = (((1,), (1,)), ((), ()))
    q = q_ref[0] * (HEAD_DIM ** -0.5)
    qrep = jnp.broadcast_to(q[:, None, :], (t_new, N_HEADS, D_ATT)).reshape(rows, D_ATT)
    row_h = lax.broadcasted_iota(jnp.int32, (rows, D_ATT), 0) % N_HEADS
    col_h = lax.broadcasted_iota(jnp.int32, (rows, D_ATT), 1) // HEAD_DIM
    own = row_h == col_h
    qbd = jnp.where(own, qrep, 0.0).astype(BF16)

    s = jnp.dot(qbd, kt_ref[0].astype(BF16), preferred_element_type=F32) + logc_ref[:, :wb]
    sn = jnp.dot(qbd, ktn_ref[0].astype(BF16), preferred_element_type=F32) + logc_ref[:, wb:]
    m = jnp.maximum(jnp.max(s, axis=-1, keepdims=True), jnp.max(sn, axis=-1, keepdims=True))
    p = jnp.exp(s - m)
    pn = jnp.exp(sn - m)
    l = jnp.sum(p, axis=-1, keepdims=True) + jnp.sum(pn, axis=-1, keepdims=True)
    o = (lax.dot_general(p.astype(BF16), vt_ref[0].astype(BF16), contract_last,
                         preferred_element_type=F32)
         + lax.dot_general(pn.astype(BF16), vtn_ref[0].astype(BF16), contract_last,
                           preferred_element_type=F32))
    o = jnp.where(own, o, 0.0) / l
    att_ref[0] = jnp.sum(o.reshape(t_new, N_HEADS, D_ATT), axis=1)

    lane = lax.broadcasted_iota(jnp.int32, (1, LANES), 1)
    keep = lane < LANES - t_new
    n_tiles = wb // LANES
    for src, new, dst in ((kt_ref, ktn_ref, kto_ref), (vt_ref, vtn_ref, vto_ref)):
        cur = pltpu.roll(src[0, :, 0:LANES], LANES - t_new, 1)
        for j in range(n_tiles):
            if j + 1 < n_tiles:
                nxt = pltpu.roll(src[0, :, (j + 1) * LANES:(j + 2) * LANES], LANES - t_new, 1)
            else:
                nxt = pltpu.roll(new[0], LANES - t_new, 1)
            dst[0, :, j * LANES:(j + 1) * LANES] = jnp.where(keep, cur, nxt)
            cur = nxt


def _sattn(q, kt, vt, ktn, vtn, logc):
    bd, t_new, _ = q.shape
    wb = kt.shape[-1]
    big = pl.BlockSpec((1, D_ATT, wb), lambda i: (i, 0, 0))
    tile = pl.BlockSpec((1, D_ATT, LANES), lambda i: (i, 0, 0))
    qspec = pl.BlockSpec((1, t_new, D_ATT), lambda i: (i, 0, 0))
    return pl.pallas_call(
        functools.partial(_sattn_kernel, wb=wb, t_new=t_new),
        grid=(bd,),
        in_specs=[qspec, big, big, tile, tile, _const_spec(logc.shape)],
        out_specs=[qspec, big, big],
        out_shape=[jax.ShapeDtypeStruct((bd, t_new, D_ATT), F32),
                   jax.ShapeDtypeStruct(kt.shape, F32), jax.ShapeDtypeStruct(vt.shape, F32)],
        compiler_params=pltpu.CompilerParams(dimension_semantics=("arbitrary",),
                                             vmem_limit_bytes=VMEM_LIMIT),
        name="sattn",
    )(q, kt, vt, ktn, vtn, logc)


def _ffn_kernel(x_ref, att_ref, sg_ref, woa_ref, wos_ref, g2_ref, wg_ref, wu_ref, wd_ref, gf_ref,
                y_ref, h_ref, acc_ref):
    x1 = (x_ref[...]
          + jnp.dot(att_ref[...], woa_ref[...], preferred_element_type=F32)
          + jnp.dot(sg_ref[...], wos_ref[...], preferred_element_type=F32))
    ms = jnp.mean(x1 * x1, axis=-1, keepdims=True)
    h_ref[...] = (x1 * lax.rsqrt(ms + EPS) * g2_ref[...]).astype(BF16)
    acc_ref[...] = x1

    def body(c, carry):
        h = h_ref[...]
        g = jnp.dot(h, wg_ref[c], preferred_element_type=F32)
        u = jnp.dot(h, wu_ref[c], preferred_element_type=F32)
        a = (g * jax.nn.sigmoid(g) * u).astype(BF16)
        acc_ref[...] += jnp.dot(a, wd_ref[c], preferred_element_type=F32)
        return carry

    lax.fori_loop(0, N_FF_CHUNKS, body, 0)
    x2 = acc_ref[...]
    ms2 = jnp.mean(x2 * x2, axis=-1, keepdims=True)
    y_ref[...] = x2 * lax.rsqrt(ms2 + EPS) * gf_ref[...]


def _ffn(x, att, sg, woa, wos, g2, wg, wu, wd, gf, *, tm):
    t = x.shape[0]
    row = lambda shape: pl.BlockSpec(shape, lambda i: (i, 0))
    return pl.pallas_call(
        _ffn_kernel,
        grid=(t // tm,),
        in_specs=[row((tm, D_MODEL)), row((tm, D_ATT)), row((tm, D_SGU)),
                  _const_spec(woa.shape), _const_spec(wos.shape), _const_spec(g2.shape),
                  _const_spec(wg.shape), _const_spec(wu.shape), _const_spec(wd.shape),
                  _const_spec(gf.shape)],
        out_specs=row((tm, D_MODEL)),
        out_shape=jax.ShapeDtypeStruct((t, D_MODEL), F32),
        scratch_shapes=[pltpu.VMEM((tm, D_MODEL), BF16), pltpu.VMEM((tm, D_MODEL), F32)],
        compiler_params=pltpu.CompilerParams(dimension_semantics=("arbitrary",),
                                             vmem_limit_bytes=VMEM_LIMIT),
        name="ffn",
    )(x, att, sg, woa, wos, g2, wg, wu, wd, gf)


def _rel_bucket(dist):
    max_exact = N_BUCKETS // 2
    df = jnp.maximum(dist, 1).astype(F32)
    large = max_exact + (jnp.log(df / max_exact) / math.log(MAX_DISTANCE / max_exact)
                         * (N_BUCKETS - max_exact)).astype(jnp.int32)
    large = jnp.minimum(large, N_BUCKETS - 1)
    return jnp.where(dist < max_exact, dist, large)


def _branch_bias_reversed(rel_bias, w, d):
    nj = w // d + 1
    dist = (nj - 1 - jnp.arange(nj, dtype=jnp.int32)) * d
    return rel_bias[_rel_bucket(dist)].T.astype(F32)


def _prompt_bias_rows(rev):
    rows = [jnp.pad(r, ((0, 0), (0, 2 * Q_BLOCK - r.shape[1])), constant_values=NEG_INF)
            for r in rev]
    return jnp.stack(rows) * LOG2E


def _sample_bias_table(rev, wb, t_new):
    width = wb + LANES
    per_branch = []
    for (w, d), r in zip(BRANCHES, rev):
        nj = r.shape[1]
        if d > 1:
            fill = jnp.full((N_HEADS, nj, d - 1), NEG_INF, F32)
            r = jnp.concatenate([r[:, :, None], fill], axis=2).reshape(N_HEADS, nj * d)
        rows = []
        for t in range(t_new):
            base = wb + t - (nj - 1) * d
            rows.append(jnp.pad(r, ((0, 0), (base, width - base - nj * d)),
                                constant_values=NEG_INF))
        per_branch.append(jnp.stack(rows))
    x = jnp.stack(per_branch)
    mx = jnp.max(x, axis=0)
    logc = mx + jnp.log(jnp.sum(jnp.exp(x - mx), axis=0))
    return logc.reshape(t_new * N_HEADS, width)


def kernel(x_prompt, x_sample, cache_k_win, cache_v_win, norm1_g, w_in, sgu_ln_g, sgu_ln_b,
           sgu_w, sgu_b, w_out, norm2_g, w_gate, w_up, w_down, rel_bias, final_g):
    depth = w_in.shape[0]
    assert depth == 1
    b, s, _ = x_prompt.shape
    bd, t_new, _ = x_sample.shape
    wb = cache_k_win.shape[2]
    assert bd * t_new == CHUNK and s % (Q_BLOCK * 16) == 0 and wb == WINDOW
    assert (s // Q_BLOCK) % ATTN_GROUP == 0 and s % PREP_ROWS == 0

    l = 0
    w_in_b = w_in[l].astype(BF16)
    woa = w_out[l, :D_ATT].astype(BF16)
    wos = w_out[l, D_ATT:].astype(BF16)
    wg = w_gate[l].reshape(D_MODEL, N_FF_CHUNKS, FF_CHUNK).transpose(1, 0, 2).astype(BF16)
    wu = w_up[l].reshape(D_MODEL, N_FF_CHUNKS, FF_CHUNK).transpose(1, 0, 2).astype(BF16)
    wd = w_down[l].reshape(N_FF_CHUNKS, FF_CHUNK, D_MODEL).astype(BF16)
    g1 = norm1_g[l][None]
    g2 = norm2_g[l][None]
    gf = final_g[None]
    ln_g = sgu_ln_g[l][None]
    ln_b = sgu_ln_b[l][None]

    causal = jnp.tril(jnp.ones((CHUNK, CHUNK), F32))
    wm = sgu_w[l] * causal
    mix_p = wm.astype(BF16)
    mixb_p = jnp.repeat(sgu_b[l].T, HEAD_DIM, axis=1)
    eye = jnp.eye(bd, dtype=F32)
    mix_s = jnp.einsum('ab,gts->gatbs', eye, wm[:, :t_new, :t_new]).reshape(
        N_GROUPS, CHUNK, CHUNK).astype(BF16)
    mixb_s = jnp.tile(mixb_p[:t_new], (bd, 1))

    rev = [_branch_bias_reversed(rel_bias, w, d) for w, d in BRANCHES]

    xp = x_prompt.reshape(b * s, D_MODEL)
    q, k, v, sg, kt_p, vt_p = _proj(xp, g1, w_in_b, ln_g, ln_b, mix_p, mixb_p,
                                    tm=512, emit_vn=False, emit_t=True, seq_len=s)
    att = _attn(q.reshape(b, s, D_ATT), k.reshape(b, s, D_ATT), v.reshape(b, s, D_ATT),
                _prompt_bias_rows(rev))
    y_prompt = _ffn(xp, att.reshape(b * s, D_ATT), sg, woa, wos, g2, wg, wu, wd, gf,
                    tm=512).reshape(b, s, D_MODEL)
    nw = min(WINDOW, s)
    new_k_p = kt_p.reshape(1, b, N_HEADS, HEAD_DIM, nw).transpose(0, 1, 4, 2, 3)
    new_v_p = vt_p.reshape(1, b, N_HEADS, HEAD_DIM, nw).transpose(0, 1, 4, 2, 3)

    xs = x_sample.reshape(bd * t_new, D_MODEL)
    qs, ks, vs, sgs, vn_s = _proj(xs, g1, w_in_b, ln_g, ln_b, mix_s, mixb_s,
                                  tm=CHUNK, emit_vn=True, emit_t=False)
    kt_c = cache_k_win[l].transpose(0, 2, 3, 1).reshape(bd, D_ATT, wb)
    vt_c = cache_v_win[l].transpose(0, 2, 3, 1).reshape(bd, D_ATT, wb)

    def new_tile(z):
        zt = z.reshape(bd, t_new, D_ATT).transpose(0, 2, 1)
        return jnp.pad(zt, ((0, 0), (0, 0), (0, LANES - t_new)))

    logc = _sample_bias_table(rev, wb, t_new)
    att_s, kt_o, vt_o = _sattn(qs.reshape(bd, t_new, D_ATT), kt_c, vt_c, new_tile(ks), new_tile(vs),
                               logc)
    y_sample = _ffn(xs, att_s.reshape(bd * t_new, D_ATT).astype(BF16), sgs, woa, wos, g2, wg, wu,
                    wd, gf, tm=CHUNK).reshape(bd, t_new, D_MODEL)
    new_k_s = kt_o.reshape(1, bd, N_HEADS, HEAD_DIM, wb).transpose(0, 1, 4, 2, 3)
    new_v_s = vt_o.reshape(1, bd, N_HEADS, HEAD_DIM, wb).transpose(0, 1, 4, 2, 3)
    sgu_v = vn_s.reshape(1, bd, t_new, D_SGU)

    return (y_prompt, y_sample, new_k_p, new_v_p, new_k_s, new_v_s, sgu_v)
```

```python
import functools
import math

import jax
import jax.numpy as jnp
from jax import lax
from jax.experimental import pallas as pl
from jax.experimental.pallas import tpu as pltpu

D_MODEL = 1024
N_HEADS = 8
HEAD_DIM = 64
D_ATT = N_HEADS * HEAD_DIM
N_GROUPS = 8
D_SGU = 512
CHUNK = 128
BRANCHES = ((128, 1), (512, 4), (2048, 16))
WINDOW = 2048
Q_BLOCK = 128
N_BUCKETS = 32
MAX_DISTANCE = WINDOW
D_FF = 2816
EPS = 1e-6
NEG_INF = -1e30
LOG2E = math.log2(math.e)

LANES = 128
FF_CHUNK = 256
N_FF_CHUNKS = D_FF // FF_CHUNK
VMEM_LIMIT = 48 * 1024 * 1024
ATTN_GROUP = 8
PREP_ROWS = 256

F32 = jnp.float32
BF16 = jnp.bfloat16


def _const_spec(shape):
    nd = len(shape)
    return pl.BlockSpec(shape, lambda *_: (0,) * nd, pipeline_mode=pl.Buffered(1))


def _proj_kernel(x_ref, g1_ref, w_ref, lng_ref, lnb_ref, mix_ref, mixb_ref, *outs,
                 tm, emit_vn, emit_t, tiles_per_seq, win_tiles):
    q_ref, k_ref, v_ref, sg_ref = outs[:4]
    rest = outs[4:]
    x = x_ref[...]
    ms = jnp.mean(x * x, axis=-1, keepdims=True)
    xn = (x * lax.rsqrt(ms + EPS) * g1_ref[...]).astype(BF16)

    def proj(c0):
        return jnp.dot(xn, w_ref[:, c0:c0 + D_ATT], preferred_element_type=F32)

    q_ref[...] = proj(0)
    k = proj(D_ATT)
    v = proj(2 * D_ATT)
    k_ref[...] = k
    v_ref[...] = v
    u = proj(3 * D_ATT)
    vg = proj(3 * D_ATT + D_SGU)

    mu = jnp.mean(vg, axis=-1, keepdims=True)
    dv = vg - mu
    var = jnp.mean(dv * dv, axis=-1, keepdims=True)
    vn = dv * lax.rsqrt(var + EPS) * lng_ref[...] + lnb_ref[...]
    if emit_vn:
        rest[0][...] = vn

    lane = lax.broadcasted_iota(jnp.int32, (1, LANES), 1)
    low = lane < HEAD_DIM
    for c in range(tm // CHUNK):
        r0 = c * CHUNK
        for s in range(D_SGU // LANES):
            slab = vn[r0:r0 + CHUNK, s * LANES:(s + 1) * LANES]
            lo = jnp.where(low, slab, 0.0).astype(BF16)
            hi = jnp.where(low, 0.0, slab).astype(BF16)
            gate = (jnp.dot(mix_ref[2 * s], lo, preferred_element_type=F32)
                    + jnp.dot(mix_ref[2 * s + 1], hi, preferred_element_type=F32)
                    + mixb_ref[:, s * LANES:(s + 1) * LANES])
            sg_ref[r0:r0 + CHUNK, s * LANES:(s + 1) * LANES] = (
                u[r0:r0 + CHUNK, s * LANES:(s + 1) * LANES] * gate).astype(BF16)

    if emit_t:
        kt_ref, vt_ref = rest[-2:]
        j = pl.program_id(0) % tiles_per_seq

        @pl.when(j >= tiles_per_seq - win_tiles)
        def _():
            kt_ref[0] = k.T
            vt_ref[0] = v.T


def _proj(x, g1, w_in, ln_g, ln_b, mix, mixb, *, tm, emit_vn, emit_t, seq_len=None):
    t = x.shape[0]
    n_tiles = t // tm
    row = lambda shape: pl.BlockSpec(shape, lambda i: (i, 0))
    out_shape = [jax.ShapeDtypeStruct((t, D_ATT), F32)] * 3 + [jax.ShapeDtypeStruct((t, D_SGU), BF16)]
    out_specs = [row((tm, D_ATT))] * 3 + [row((tm, D_SGU))]
    tiles_per_seq = win_tiles = 0
    if emit_vn:
        out_shape.append(jax.ShapeDtypeStruct((t, D_SGU), F32))
        out_specs.append(row((tm, D_SGU)))
    if emit_t:
        tiles_per_seq = seq_len // tm
        win = min(WINDOW, seq_len)
        win_tiles = win // tm
        first = tiles_per_seq - win_tiles

        def t_map(i):
            return (i // tiles_per_seq, 0, jnp.maximum(i % tiles_per_seq - first, 0))

        for _ in range(2):
            out_shape.append(jax.ShapeDtypeStruct((t // seq_len, D_ATT, win), F32))
            out_specs.append(pl.BlockSpec((1, D_ATT, tm), t_map))
    kern = functools.partial(_proj_kernel, tm=tm, emit_vn=emit_vn, emit_t=emit_t,
                             tiles_per_seq=tiles_per_seq, win_tiles=win_tiles)
    return pl.pallas_call(
        kern,
        grid=(n_tiles,),
        in_specs=[row((tm, D_MODEL)), _const_spec(g1.shape), _const_spec(w_in.shape),
                  _const_spec(ln_g.shape), _const_spec(ln_b.shape), _const_spec(mix.shape),
                  _const_spec(mixb.shape)],
        out_specs=out_specs,
        out_shape=out_shape,
        compiler_params=pltpu.CompilerParams(dimension_semantics=("arbitrary",),
                                             vmem_limit_bytes=VMEM_LIMIT),
        name="proj_t" if emit_t else "proj_s",
    )(x, g1, w_in, ln_g, ln_b, mix, mixb)


def _div_pow2(x, n):
    assert n & (n - 1) == 0
    return lax.shift_right_logical(x, n.bit_length() - 1)


def _mod_pow2(x, n):
    assert n & (n - 1) == 0
    return lax.bitwise_and(x, n - 1)


def _attn_kernel(q_ref, k_ref, v_ref, r0_ref, o_ref, tab_ref, acc_ref, m_ref, l_ref,
                 q0_ref, q1_ref, kd_ref, v0_ref, v1_ref, s_ref, mx_ref, *, seq_len):
    hp = pl.program_id(1)
    lane = lax.broadcasted_iota(jnp.int32, (1, LANES), 1)
    low = lane < HEAD_DIM
    contract_last = (((1,), (1,)), ((), ()))
    neg = NEG_INF * LOG2E
    quarter = seq_len // 4

    col = lax.broadcasted_iota(jnp.int32, (Q_BLOCK, 2 * Q_BLOCK), 1)
    for br in range(len(BRANCHES)):
        for h in range(2):
            base = jnp.broadcast_to(r0_ref[br, pl.ds(2 * hp + h, 1), :], (Q_BLOCK, 2 * Q_BLOCK))
            t = pltpu.roll(base, 0, 1, stride=1, stride_axis=0)
            tab_ref[br, 0, h] = t
            tab_ref[br, 1, h] = jnp.where(col >= Q_BLOCK, t, neg)
    zeros = jnp.zeros((Q_BLOCK, LANES), BF16)
    kd_ref[0:Q_BLOCK, :] = zeros
    v0_ref[0:Q_BLOCK, :] = zeros
    v1_ref[0:Q_BLOCK, :] = zeros

    def strided(start, n, d):
        return pl.ds(start, n) if d == 1 else pl.ds(start, n, stride=d)

    for br, (w, d) in enumerate(BRANCHES):
        sub_len = seq_len // d
        nb = sub_len // Q_BLOCK

        def prep(c, carry, d=d, sub_len=sub_len):
            de0 = c * PREP_ROWS
            dst = pl.ds(pl.multiple_of(de0, PREP_ROWS), PREP_ROWS)
            dstp = pl.ds(pl.multiple_of(de0 + Q_BLOCK, Q_BLOCK), PREP_ROWS)
            if d <= 4:
                src = strided(_div_pow2(de0, sub_len) + _mod_pow2(de0, sub_len) * d, PREP_ROWS, d)
                qq, kk, vv = q_ref[0, src, :], k_ref[0, src, :], v_ref[0, src, :]
            else:
                res, m0 = _div_pow2(de0, sub_len), _mod_pow2(de0, sub_len)
                src = pl.ds(_mod_pow2(res, 4) * quarter + m0 * (d // 4) + _div_pow2(res, 4),
                            PREP_ROWS, stride=d // 4)
                qq, kk, vv = acc_ref[2, src, :], m_ref[2, src, :], l_ref[2, src, :]
            if d == 4:
                acc_ref[2, dst, :] = qq
                m_ref[2, dst, :] = kk
                l_ref[2, dst, :] = vv
            qs = qq * (HEAD_DIM ** -0.5 * LOG2E)
            q0_ref[dst, :] = jnp.where(low, qs, 0.0).astype(BF16)
            q1_ref[dst, :] = jnp.where(low, 0.0, qs).astype(BF16)
            kd_ref[dstp, :] = kk.astype(BF16)
            v0_ref[dstp, :] = jnp.where(low, vv, 1.0).astype(BF16)
            v1_ref[dstp, :] = jnp.where(low, 1.0, vv).astype(BF16)
            return carry

        lax.fori_loop(0, seq_len // PREP_ROWS, prep, 0)

        def score(i, br=br, nb=nb):
            for u in range(ATTN_GROUP):
                g = i * ATTN_GROUP + u
                rq = pl.ds(pl.multiple_of(g * Q_BLOCK, Q_BLOCK), Q_BLOCK)
                rk = pl.ds(pl.multiple_of(g * Q_BLOCK, Q_BLOCK), 2 * Q_BLOCK)
                first = jnp.where(_mod_pow2(g, nb) == 0, 1, 0)
                kb = kd_ref[rk, :]
                for h, qh_ref in enumerate((q0_ref, q1_ref)):
                    s = lax.dot_general(qh_ref[rq, :], kb, contract_last,
                                        preferred_element_type=F32)
                    s = s + tab_ref[br, first, h]
                    s_ref[u, h] = s
                    mx_ref[u, h] = jnp.broadcast_to(jnp.max(s, axis=-1, keepdims=True),
                                                    (Q_BLOCK, LANES))

        def finish(i, br=br, d=d, nb=nb):
            for u in range(ATTN_GROUP):
                g = i * ATTN_GROUP + u
                rk = pl.ds(pl.multiple_of(g * Q_BLOCK, Q_BLOCK), 2 * Q_BLOCK)
                outs = []
                for h, vh_ref in enumerate((v0_ref, v1_ref)):
                    m = mx_ref[u, h]
                    p = jnp.concatenate([jnp.exp2(s_ref[u, h, :, :LANES] - m),
                                         jnp.exp2(s_ref[u, h, :, LANES:] - m)], axis=1)
                    outs.append((jnp.dot(p.astype(BF16), vh_ref[rk, :],
                                         preferred_element_type=F32), m))
                (o0, m0), (o1, m1) = outs
                if d <= 4:
                    idx = pl.ds(pl.multiple_of(g * Q_BLOCK, Q_BLOCK), Q_BLOCK)
                else:
                    res, blk = _div_pow2(g, nb), _mod_pow2(g, nb)
                    idx = pl.ds(_mod_pow2(res, 4) * quarter + blk * (Q_BLOCK * d // 4)
                                + _div_pow2(res, 4), Q_BLOCK, stride=d // 4)
                acc_ref[br, idx, :] = jnp.where(low, o0, o1)
                l_ref[br, idx, :] = pltpu.roll(jnp.where(low, o1, o0), HEAD_DIM, 1)
                m_ref[br, idx, :] = jnp.where(low, m0, m1)

        def step(i, carry):
            finish(i - 1)
            score(i)
            return carry

        n_groups = seq_len // Q_BLOCK // ATTN_GROUP
        score(0)
        lax.fori_loop(1, n_groups, step, 0)
        finish(n_groups - 1)

    def merge(c, carry):
        de0 = c * PREP_ROWS
        dil = pl.ds(pl.multiple_of(de0, PREP_ROWS), PREP_ROWS)
        nat = pl.ds(_div_pow2(de0, quarter) + _mod_pow2(de0, quarter) * 4, PREP_ROWS, stride=4)
        ms = (m_ref[0, nat, :], m_ref[1, dil, :], m_ref[2, dil, :])
        m_all = jnp.maximum(jnp.maximum(ms[0], ms[1]), ms[2])
        ws = [jnp.exp2(m - m_all) for m in ms]
        num = (acc_ref[0, nat, :] * ws[0] + acc_ref[1, dil, :] * ws[1] + acc_ref[2, dil, :] * ws[2])
        den = l_ref[0, nat, :] * ws[0] + l_ref[1, dil, :] * ws[1] + l_ref[2, dil, :] * ws[2]
        acc_ref[0, nat, :] = num / den
        return carry

    lax.fori_loop(0, seq_len // PREP_ROWS, merge, 0)

    rows_per = 512

    def fin(i, c):
        sl = pl.ds(pl.multiple_of(i * rows_per, rows_per), rows_per)
        o_ref[0, sl, :] = acc_ref[0, sl, :].astype(o_ref.dtype)
        return c

    lax.fori_loop(0, seq_len // rows_per, fin, 0)


def _attn(q, k, v, r0):
    b, s, _ = q.shape
    n_br = len(BRANCHES)
    blk = pl.BlockSpec((1, s, LANES), lambda i, j: (i, 0, j))
    state = pltpu.VMEM((n_br, s, LANES), F32)
    qd = pltpu.VMEM((s, LANES), BF16)
    kd = pltpu.VMEM((s + Q_BLOCK, LANES), BF16)
    return pl.pallas_call(
        functools.partial(_attn_kernel, seq_len=s),
        grid=(b, D_ATT // LANES),
        in_specs=[blk, blk, blk, _const_spec(r0.shape)],
        out_specs=blk,
        out_shape=jax.ShapeDtypeStruct((b, s, D_ATT), BF16),
        scratch_shapes=[pltpu.VMEM((n_br, 2, 2, Q_BLOCK, 2 * Q_BLOCK), F32),
                        state, state, state, qd, qd, kd, kd, kd,
                        pltpu.VMEM((ATTN_GROUP, 2, Q_BLOCK, 2 * Q_BLOCK), F32),
                        pltpu.VMEM((ATTN_GROUP, 2, Q_BLOCK, LANES), F32)],
        compiler_params=pltpu.CompilerParams(dimension_semantics=("arbitrary", "arbitrary"),
                                             vmem_limit_bytes=VMEM_LIMIT),
        name="attn",
    )(q, k, v, r0)


def _sattn_kernel(q_ref, kt_ref, vt_ref, ktn_ref, vtn_ref, logc_ref,
                  att_ref, kto_ref, vto_ref, *, wb, t_new):
    rows = t_new * N_HEADS
    contract_last = (((1,), (1,)), ((), ()))
    q = q_ref[0] * (HEAD_DIM ** -0.5)
    qrep = jnp.broadcast_to(q[:, None, :], (t_new, N_HEADS, D_ATT)).reshape(rows, D_ATT)
    row_h = lax.broadcasted_iota(jnp.int32, (rows, D_ATT), 0) % N_HEADS
    col_h = lax.broadcasted_iota(jnp.int32, (rows, D_ATT), 1) // HEAD_DIM
    own = row_h == col_h
    qbd = jnp.where(own, qrep, 0.0).astype(BF16)

    s = jnp.dot(qbd, kt_ref[0].astype(BF16), preferred_element_type=F32) + logc_ref[:, :wb]
    sn = jnp.dot(qbd, ktn_ref[0].astype(BF16), preferred_element_type=F32) + logc_ref[:, wb:]
    m = jnp.maximum(jnp.max(s, axis=-1, keepdims=True), jnp.max(sn, axis=-1, keepdims=True))
    p = jnp.exp(s - m)
    pn = jnp.exp(sn - m)
    l = jnp.sum(p, axis=-1, keepdims=True) + jnp.sum(pn, axis=-1, keepdims=True)
    o = (lax.dot_general(p.astype(BF16), vt_ref[0].astype(BF16), contract_last,
                         preferred_element_type=F32)
         + lax.dot_general(pn.astype(BF16), vtn_ref[0].astype(BF16), contract_last,
                           preferred_element_type=F32))
    o = jnp.where(own, o, 0.0) / l
    att_ref[0] = jnp.sum(o.reshape(t_new, N_HEADS, D_ATT), axis=1)

    lane = lax.broadcasted_iota(jnp.int32, (1, LANES), 1)
    keep = lane < LANES - t_new
    n_tiles = wb // LANES
    for src, new, dst in ((kt_ref, ktn_ref, kto_ref), (vt_ref, vtn_ref, vto_ref)):
        cur = pltpu.roll(src[0, :, 0:LANES], LANES - t_new, 1)
        for j in range(n_tiles):
            if j + 1 < n_tiles:
                nxt = pltpu.roll(src[0, :, (j + 1) * LANES:(j + 2) * LANES], LANES - t_new, 1)
            else:
                nxt = pltpu.roll(new[0], LANES - t_new, 1)
            dst[0, :, j * LANES:(j + 1) * LANES] = jnp.where(keep, cur, nxt)
            cur = nxt


def _sattn(q, kt, vt, ktn, vtn, logc):
    bd, t_new, _ = q.shape
    wb = kt.shape[-1]
    big = pl.BlockSpec((1, D_ATT, wb), lambda i: (i, 0, 0))
    tile = pl.BlockSpec((1, D_ATT, LANES), lambda i: (i, 0, 0))
    qspec = pl.BlockSpec((1, t_new, D_ATT), lambda i: (i, 0, 0))
    return pl.pallas_call(
        functools.partial(_sattn_kernel, wb=wb, t_new=t_new),
        grid=(bd,),
        in_specs=[qspec, big, big, tile, tile, _const_spec(logc.shape)],
        out_specs=[qspec, big, big],
        out_shape=[jax.ShapeDtypeStruct((bd, t_new, D_ATT), F32),
                   jax.ShapeDtypeStruct(kt.shape, F32), jax.ShapeDtypeStruct(vt.shape, F32)],
        compiler_params=pltpu.CompilerParams(dimension_semantics=("arbitrary",),
                                             vmem_limit_bytes=VMEM_LIMIT),
        name="sattn",
    )(q, kt, vt, ktn, vtn, logc)


def _ffn_kernel(x_ref, att_ref, sg_ref, woa_ref, wos_ref, g2_ref, wg_ref, wu_ref, wd_ref, gf_ref,
                y_ref, h_ref, acc_ref):
    x1 = (x_ref[...]
          + jnp.dot(att_ref[...], woa_ref[...], preferred_element_type=F32)
          + jnp.dot(sg_ref[...], wos_ref[...], preferred_element_type=F32))
    ms = jnp.mean(x1 * x1, axis=-1, keepdims=True)
    h_ref[...] = (x1 * lax.rsqrt(ms + EPS) * g2_ref[...]).astype(BF16)
    acc_ref[...] = x1

    def body(c, carry):
        h = h_ref[...]
        g = jnp.dot(h, wg_ref[c], preferred_element_type=F32)
        u = jnp.dot(h, wu_ref[c], preferred_element_type=F32)
        a = (g * jax.nn.sigmoid(g) * u).astype(BF16)
        acc_ref[...] += jnp.dot(a, wd_ref[c], preferred_element_type=F32)
        return carry

    lax.fori_loop(0, N_FF_CHUNKS, body, 0)
    x2 = acc_ref[...]
    ms2 = jnp.mean(x2 * x2, axis=-1, keepdims=True)
    y_ref[...] = x2 * lax.rsqrt(ms2 + EPS) * gf_ref[...]


def _ffn(x, att, sg, woa, wos, g2, wg, wu, wd, gf, *, tm):
    t = x.shape[0]
    row = lambda shape: pl.BlockSpec(shape, lambda i: (i, 0))
    return pl.pallas_call(
        _ffn_kernel,
        grid=(t // tm,),
        in_specs=[row((tm, D_MODEL)), row((tm, D_ATT)), row((tm, D_SGU)),
                  _const_spec(woa.shape), _const_spec(wos.shape), _const_spec(g2.shape),
                  _const_spec(wg.shape), _const_spec(wu.shape), _const_spec(wd.shape),
                  _const_spec(gf.shape)],
        out_specs=row((tm, D_MODEL)),
        out_shape=jax.ShapeDtypeStruct((t, D_MODEL), F32),
        scratch_shapes=[pltpu.VMEM((tm, D_MODEL), BF16), pltpu.VMEM((tm, D_MODEL), F32)],
        compiler_params=pltpu.CompilerParams(dimension_semantics=("arbitrary",),
                                             vmem_limit_bytes=VMEM_LIMIT),
        name="ffn",
    )(x, att, sg, woa, wos, g2, wg, wu, wd, gf)


def _rel_bucket(dist):
    max_exact = N_BUCKETS // 2
    df = jnp.maximum(dist, 1).astype(F32)
    large = max_exact + (jnp.log(df / max_exact) / math.log(MAX_DISTANCE / max_exact)
                         * (N_BUCKETS - max_exact)).astype(jnp.int32)
    large = jnp.minimum(large, N_BUCKETS - 1)
    return jnp.where(dist < max_exact, dist, large)


def _branch_bias_reversed(rel_bias, w, d):
    nj = w // d + 1
    dist = (nj - 1 - jnp.arange(nj, dtype=jnp.int32)) * d
    return rel_bias[_rel_bucket(dist)].T.astype(F32)


def _prompt_bias_rows(rev):
    rows = [jnp.pad(r, ((0, 0), (0, 2 * Q_BLOCK - r.shape[1])), constant_values=NEG_INF)
            for r in rev]
    return jnp.stack(rows) * LOG2E


def _sample_bias_table(rev, wb, t_new):
    width = wb + LANES
    per_branch = []
    for (w, d), r in zip(BRANCHES, rev):
        nj = r.shape[1]
        if d > 1:
            fill = jnp.full((N_HEADS, nj, d - 1), NEG_INF, F32)
            r = jnp.concatenate([r[:, :, None], fill], axis=2).reshape(N_HEADS, nj * d)
        rows = []
        for t in range(t_new):
            base = wb + t - (nj - 1) * d
            rows.append(jnp.pad(r, ((0, 0), (base, width - base - nj * d)),
                                constant_values=NEG_INF))
        per_branch.append(jnp.stack(rows))
    x = jnp.stack(per_branch)
    mx = jnp.max(x, axis=0)
    logc = mx + jnp.log(jnp.sum(jnp.exp(x - mx), axis=0))
    return logc.reshape(t_new * N_HEADS, width)


def kernel(x_prompt, x_sample, cache_k_win, cache_v_win, norm1_g, w_in, sgu_ln_g, sgu_ln_b,
           sgu_w, sgu_b, w_out, norm2_g, w_gate, w_up, w_down, rel_bias, final_g):
    depth = w_in.shape[0]
    assert depth == 1
    b, s, _ = x_prompt.shape
    bd, t_new, _ = x_sample.shape
    wb = cache_k_win.shape[2]
    assert bd * t_new == CHUNK and s % (Q_BLOCK * 16) == 0 and wb == WINDOW
    assert (s // Q_BLOCK) % ATTN_GROUP == 0 and s % PREP_ROWS == 0

    l = 0
    w_in_b = w_in[l].astype(BF16)
    woa = w_out[l, :D_ATT].astype(BF16)
    wos = w_out[l, D_ATT:].astype(BF16)
    wg = w_gate[l].reshape(D_MODEL, N_FF_CHUNKS, FF_CHUNK).transpose(1, 0, 2).astype(BF16)
    wu = w_up[l].reshape(D_MODEL, N_FF_CHUNKS, FF_CHUNK).transpose(1, 0, 2).astype(BF16)
    wd = w_down[l].reshape(N_FF_CHUNKS, FF_CHUNK, D_MODEL).astype(BF16)
    g1 = norm1_g[l][None]
    g2 = norm2_g[l][None]
    gf = final_g[None]
    ln_g = sgu_ln_g[l][None]
    ln_b = sgu_ln_b[l][None]

    causal = jnp.tril(jnp.ones((CHUNK, CHUNK), F32))
    wm = sgu_w[l] * causal
    mix_p = wm.astype(BF16)
    mixb_p = jnp.repeat(sgu_b[l].T, HEAD_DIM, axis=1)
    eye = jnp.eye(bd, dtype=F32)
    mix_s = jnp.einsum('ab,gts->gatbs', eye, wm[:, :t_new, :t_new]).reshape(
        N_GROUPS, CHUNK, CHUNK).astype(BF16)
    mixb_s = jnp.tile(mixb_p[:t_new], (bd, 1))

    rev = [_branch_bias_reversed(rel_bias, w, d) for w, d in BRANCHES]

    xp = x_prompt.reshape(b * s, D_MODEL)
    q, k, v, sg, kt_p, vt_p = _proj(xp, g1, w_in_b, ln_g, ln_b, mix_p, mixb_p,
                                    tm=512, emit_vn=False, emit_t=True, seq_len=s)
    att = _attn(q.reshape(b, s, D_ATT), k.reshape(b, s, D_ATT), v.reshape(b, s, D_ATT),
                _prompt_bias_rows(rev))
    y_prompt = _ffn(xp, att.reshape(b * s, D_ATT), sg, woa, wos, g2, wg, wu, wd, gf,
                    tm=512).reshape(b, s, D_MODEL)
    nw = min(WINDOW, s)
    new_k_p = kt_p.reshape(1, b, N_HEADS, HEAD_DIM, nw).transpose(0, 1, 4, 2, 3)
    new_v_p = vt_p.reshape(1, b, N_HEADS, HEAD_DIM, nw).transpose(0, 1, 4, 2, 3)

    xs = x_sample.reshape(bd * t_new, D_MODEL)
    qs, ks, vs, sgs, vn_s = _proj(xs, g1, w_in_b, ln_g, ln_b, mix_s, mixb_s,
                                  tm=CHUNK, emit_vn=True, emit_t=False)
    kt_c = cache_k_win[l].transpose(0, 2, 3, 1).reshape(bd, D_ATT, wb)
    vt_c = cache_v_win[l].transpose(0, 2, 3, 1).reshape(bd, D_ATT, wb)

    def new_tile(z):
        zt = z.reshape(bd, t_new, D_ATT).transpose(0, 2, 1)
        return jnp.pad(zt, ((0, 0), (0, 0), (0, LANES - t_new)))

    logc = _sample_bias_table(rev, wb, t_new)
    att_s, kt_o, vt_o = _sattn(qs.reshape(bd, t_new, D_ATT), kt_c, vt_c, new_tile(ks), new_tile(vs),
                               logc)
    y_sample = _ffn(xs, att_s.reshape(bd * t_new, D_ATT).astype(BF16), sgs, woa, wos, g2, wg, wu,
                    wd, gf, tm=CHUNK).reshape(bd, t_new, D_MODEL)
    new_k_s = kt_o.reshape(1, bd, N_HEADS, HEAD_DIM, wb).transpose(0, 1, 4, 2, 3)
    new_v_s = vt_o.reshape(1, bd, N_HEADS, HEAD_DIM, wb).transpose(0, 1, 4, 2, 3)
    sgu_v = vn_s.reshape(1, bd, t_new, D_SGU)

    return (y_prompt, y_sample, new_k_p, new_v_p, new_k_s, new_v_s, sgu_v)
```

```python
import functools
import math

import jax
import jax.numpy as jnp
from jax import lax
from jax.experimental import pallas as pl
from jax.experimental.pallas import tpu as pltpu

D_MODEL = 1024
N_HEADS = 8
HEAD_DIM = 64
D_ATT = N_HEADS * HEAD_DIM
N_GROUPS = 8
D_SGU = 512
CHUNK = 128
BRANCHES = ((128, 1), (512, 4), (2048, 16))
WINDOW = 2048
Q_BLOCK = 128
N_BUCKETS = 32
MAX_DISTANCE = WINDOW
D_FF = 2816
EPS = 1e-6
NEG_INF = -1e30
LOG2E = math.log2(math.e)

LANES = 128
FF_CHUNK = 256
N_FF_CHUNKS = D_FF // FF_CHUNK
VMEM_LIMIT = 48 * 1024 * 1024
ATTN_GROUP = 8
PREP_ROWS = 256

F32 = jnp.float32
BF16 = jnp.bfloat16


def _const_spec(shape):
    nd = len(shape)
    return pl.BlockSpec(shape, lambda *_: (0,) * nd, pipeline_mode=pl.Buffered(1))


def _proj_kernel(x_ref, g1_ref, w_ref, lng_ref, lnb_ref, mix_ref, mixb_ref, *outs,
                 tm, emit_vn, emit_t):
    q_ref, k_ref, v_ref, sg_ref = outs[:4]
    rest = outs[4:]
    lane = lax.broadcasted_iota(jnp.int32, (1, LANES), 1)
    low = lane < HEAD_DIM

    x = x_ref[...]
    xg = (x * g1_ref[...]).astype(BF16)
    r = lax.rsqrt(jnp.mean(x * x, axis=-1, keepdims=True) + EPS)

    def proj(c0):
        return jnp.dot(xg, w_ref[:, c0:c0 + D_ATT], preferred_element_type=F32) * r

    u = proj(3 * D_ATT)
    vg = proj(3 * D_ATT + D_SGU)
    mu = jnp.mean(vg, axis=-1, keepdims=True)
    dv = vg - mu
    var = jnp.mean(dv * dv, axis=-1, keepdims=True)
    vn = dv * lax.rsqrt(var + EPS) * lng_ref[...] + lnb_ref[...]
    if emit_vn:
        rest[0][...] = vn

    for c0 in range(0, tm, CHUNK):
        for s in range(D_SGU // LANES):
            cols = slice(s * LANES, (s + 1) * LANES)
            slab = vn[c0:c0 + CHUNK, cols]
            lo = jnp.where(low, slab, 0.0).astype(BF16)
            hi = jnp.where(low, 0.0, slab).astype(BF16)
            gate = (jnp.dot(mix_ref[2 * s], lo, preferred_element_type=F32)
                    + jnp.dot(mix_ref[2 * s + 1], hi, preferred_element_type=F32)
                    + mixb_ref[:, cols])
            sg_ref[c0:c0 + CHUNK, cols] = (u[c0:c0 + CHUNK, cols] * gate).astype(BF16)

    q_ref[...] = proj(0)
    k = proj(D_ATT)
    v = proj(2 * D_ATT)
    k_ref[...] = k
    v_ref[...] = v
    if emit_t:
        kt_ref, vt_ref = rest[-2:]
        kt_ref[0] = k.T
        vt_ref[0] = v.T


def _proj(x, g1, w_in, ln_g, ln_b, mix, mixb, *, tm, emit_vn, emit_t, seq_len=None):
    t = x.shape[0]
    n_tiles = t // tm
    row = lambda shape: pl.BlockSpec(shape, lambda i: (i, 0))
    out_shape = [jax.ShapeDtypeStruct((t, D_ATT), F32)] * 3 + [jax.ShapeDtypeStruct((t, D_SGU), BF16)]
    out_specs = [row((tm, D_ATT))] * 3 + [row((tm, D_SGU))]
    if emit_vn:
        out_shape.append(jax.ShapeDtypeStruct((t, D_SGU), F32))
        out_specs.append(row((tm, D_SGU)))
    if emit_t:
        tiles_per_seq = seq_len // tm
        win = min(WINDOW, seq_len)
        first = tiles_per_seq - win // tm

        def t_map(i):
            return (i // tiles_per_seq, 0, jnp.maximum(i % tiles_per_seq - first, 0))

        for _ in range(2):
            out_shape.append(jax.ShapeDtypeStruct((t // seq_len, D_ATT, win), F32))
            out_specs.append(pl.BlockSpec((1, D_ATT, tm), t_map))
    kern = functools.partial(_proj_kernel, tm=tm, emit_vn=emit_vn, emit_t=emit_t)
    return pl.pallas_call(
        kern,
        grid=(n_tiles,),
        in_specs=[row((tm, D_MODEL)), _const_spec(g1.shape), _const_spec(w_in.shape),
                  _const_spec(ln_g.shape), _const_spec(ln_b.shape), _const_spec(mix.shape),
                  _const_spec(mixb.shape)],
        out_specs=out_specs,
        out_shape=out_shape,
        compiler_params=pltpu.CompilerParams(dimension_semantics=("arbitrary",),
                                             vmem_limit_bytes=VMEM_LIMIT),
        name="proj_t" if emit_t else "proj_s",
    )(x, g1, w_in, ln_g, ln_b, mix, mixb)


def _div_pow2(x, n):
    assert n & (n - 1) == 0
    return lax.shift_right_logical(x, n.bit_length() - 1)


def _mod_pow2(x, n):
    assert n & (n - 1) == 0
    return lax.bitwise_and(x, n - 1)


def _attn_kernel(q_ref, k_ref, v_ref, r0_ref, o_ref, tab_ref, acc_ref, m_ref, l_ref,
                 q0_ref, q1_ref, kd_ref, v0_ref, v1_ref, s_ref, mx_ref, *, seq_len):
    hp = pl.program_id(1)
    lane = lax.broadcasted_iota(jnp.int32, (1, LANES), 1)
    low = lane < HEAD_DIM
    contract_last = (((1,), (1,)), ((), ()))
    neg = NEG_INF * LOG2E
    quarter = seq_len // 4

    col = lax.broadcasted_iota(jnp.int32, (Q_BLOCK, 2 * Q_BLOCK), 1)
    for br in range(len(BRANCHES)):
        for h in range(2):
            base = jnp.broadcast_to(r0_ref[br, pl.ds(2 * hp + h, 1), :], (Q_BLOCK, 2 * Q_BLOCK))
            t = pltpu.roll(base, 0, 1, stride=1, stride_axis=0)
            tab_ref[br, 0, h] = t
            tab_ref[br, 1, h] = jnp.where(col >= Q_BLOCK, t, neg)
    zeros = jnp.zeros((Q_BLOCK, LANES), BF16)
    kd_ref[0:Q_BLOCK, :] = zeros
    v0_ref[0:Q_BLOCK, :] = zeros
    v1_ref[0:Q_BLOCK, :] = zeros

    def strided(start, n, d):
        return pl.ds(start, n) if d == 1 else pl.ds(start, n, stride=d)

    for br, (w, d) in enumerate(BRANCHES):
        sub_len = seq_len // d
        nb = sub_len // Q_BLOCK

        def prep(c, carry, d=d, sub_len=sub_len):
            de0 = c * PREP_ROWS
            dst = pl.ds(pl.multiple_of(de0, PREP_ROWS), PREP_ROWS)
            dstp = pl.ds(pl.multiple_of(de0 + Q_BLOCK, Q_BLOCK), PREP_ROWS)
            if d <= 4:
                src = strided(_div_pow2(de0, sub_len) + _mod_pow2(de0, sub_len) * d, PREP_ROWS, d)
                qq, kk, vv = q_ref[0, src, :], k_ref[0, src, :], v_ref[0, src, :]
            else:
                res, m0 = _div_pow2(de0, sub_len), _mod_pow2(de0, sub_len)
                src = pl.ds(_mod_pow2(res, 4) * quarter + m0 * (d // 4) + _div_pow2(res, 4),
                            PREP_ROWS, stride=d // 4)
                qq, kk, vv = acc_ref[2, src, :], m_ref[2, src, :], l_ref[2, src, :]
            if d == 4:
                acc_ref[2, dst, :] = qq
                m_ref[2, dst, :] = kk
                l_ref[2, dst, :] = vv
            qs = qq * (HEAD_DIM ** -0.5 * LOG2E)
            q0_ref[dst, :] = jnp.where(low, qs, 0.0).astype(BF16)
            q1_ref[dst, :] = jnp.where(low, 0.0, qs).astype(BF16)
            kd_ref[dstp, :] = kk.astype(BF16)
            v0_ref[dstp, :] = jnp.where(low, vv, 1.0).astype(BF16)
            v1_ref[dstp, :] = jnp.where(low, 1.0, vv).astype(BF16)
            return carry

        lax.fori_loop(0, seq_len // PREP_ROWS, prep, 0)

        def score(i, br=br, nb=nb):
            for u in range(ATTN_GROUP):
                g = i * ATTN_GROUP + u
                rq = pl.ds(pl.multiple_of(g * Q_BLOCK, Q_BLOCK), Q_BLOCK)
                rk = pl.ds(pl.multiple_of(g * Q_BLOCK, Q_BLOCK), 2 * Q_BLOCK)
                first = jnp.where(_mod_pow2(g, nb) == 0, 1, 0)
                kb = kd_ref[rk, :]
                for h, qh_ref in enumerate((q0_ref, q1_ref)):
                    s = lax.dot_general(qh_ref[rq, :], kb, contract_last,
                                        preferred_element_type=F32)
                    s = s + tab_ref[br, first, h]
                    s_ref[u, h] = s
                    mx_ref[u, h] = jnp.broadcast_to(jnp.max(s, axis=-1, keepdims=True),
                                                    (Q_BLOCK, LANES))

        def finish(i, br=br, d=d, nb=nb):
            for u in range(ATTN_GROUP):
                g = i * ATTN_GROUP + u
                rk = pl.ds(pl.multiple_of(g * Q_BLOCK, Q_BLOCK), 2 * Q_BLOCK)
                outs = []
                for h, vh_ref in enumerate((v0_ref, v1_ref)):
                    m = mx_ref[u, h]
                    p = jnp.concatenate([jnp.exp2(s_ref[u, h, :, :LANES] - m),
                                         jnp.exp2(s_ref[u, h, :, LANES:] - m)], axis=1)
                    outs.append((jnp.dot(p.astype(BF16), vh_ref[rk, :],
                                         preferred_element_type=F32), m))
                (o0, m0), (o1, m1) = outs
                if d <= 4:
                    idx = pl.ds(pl.multiple_of(g * Q_BLOCK, Q_BLOCK), Q_BLOCK)
                else:
                    res, blk = _div_pow2(g, nb), _mod_pow2(g, nb)
                    idx = pl.ds(_mod_pow2(res, 4) * quarter + blk * (Q_BLOCK * d // 4)
                                + _div_pow2(res, 4), Q_BLOCK, stride=d // 4)
                acc_ref[br, idx, :] = jnp.where(low, o0, o1)
                l_ref[br, idx, :] = pltpu.roll(jnp.where(low, o1, o0), HEAD_DIM, 1)
                m_ref[br, idx, :] = jnp.where(low, m0, m1)

        def step(i, carry):
            finish(i - 1)
            score(i)
            return carry

        n_groups = seq_len // Q_BLOCK // ATTN_GROUP
        score(0)
        lax.fori_loop(1, n_groups, step, 0)
        finish(n_groups - 1)

    def merge(c, carry):
        de0 = c * PREP_ROWS
        dil = pl.ds(pl.multiple_of(de0, PREP_ROWS), PREP_ROWS)
        nat = pl.ds(_div_pow2(de0, quarter) + _mod_pow2(de0, quarter) * 4, PREP_ROWS, stride=4)
        ms = (m_ref[0, nat, :], m_ref[1, dil, :], m_ref[2, dil, :])
        m_all = jnp.maximum(jnp.maximum(ms[0], ms[1]), ms[2])
        ws = [jnp.exp2(m - m_all) for m in ms]
        num = (acc_ref[0, nat, :] * ws[0] + acc_ref[1, dil, :] * ws[1] + acc_ref[2, dil, :] * ws[2])
        den = l_ref[0, nat, :] * ws[0] + l_ref[1, dil, :] * ws[1] + l_ref[2, dil, :] * ws[2]
        acc_ref[0, nat, :] = num / den
        return carry

    lax.fori_loop(0, seq_len // PREP_ROWS, merge, 0)

    rows_per = 512

    def fin(i, c):
        sl = pl.ds(pl.multiple_of(i * rows_per, rows_per), rows_per)
        o_ref[0, sl, :] = acc_ref[0, sl, :].astype(o_ref.dtype)
        return c

    lax.fori_loop(0, seq_len // rows_per, fin, 0)


def _attn(q, k, v, r0):
    b, s, _ = q.shape
    n_br = len(BRANCHES)
    blk = pl.BlockSpec((1, s, LANES), lambda i, j: (i, 0, j))
    state = pltpu.VMEM((n_br, s, LANES), F32)
    qd = pltpu.VMEM((s, LANES), BF16)
    kd = pltpu.VMEM((s + Q_BLOCK, LANES), BF16)
    return pl.pallas_call(
        functools.partial(_attn_kernel, seq_len=s),
        grid=(b, D_ATT // LANES),
        in_specs=[blk, blk, blk, _const_spec(r0.shape)],
        out_specs=blk,
        out_shape=jax.ShapeDtypeStruct((b, s, D_ATT), BF16),
        scratch_shapes=[pltpu.VMEM((n_br, 2, 2, Q_BLOCK, 2 * Q_BLOCK), F32),
                        state, state, state, qd, qd, kd, kd, kd,
                        pltpu.VMEM((ATTN_GROUP, 2, Q_BLOCK, 2 * Q_BLOCK), F32),
                        pltpu.VMEM((ATTN_GROUP, 2, Q_BLOCK, LANES), F32)],
        compiler_params=pltpu.CompilerParams(dimension_semantics=("arbitrary", "arbitrary"),
                                             vmem_limit_bytes=VMEM_LIMIT),
        name="attn",
    )(q, k, v, r0)


def _sattn_kernel(q_ref, kt_ref, vt_ref, ktn_ref, vtn_ref, logc_ref,
                  att_ref, kto_ref, vto_ref, *, wb, t_new):
    rows = t_new * N_HEADS
    contract_last = (((1,), (1,)), ((), ()))
    b = pl.program_id(0)
    shift = lax.bitwise_and(LANES - t_new * b, LANES - 1)
    ktn = pltpu.roll(ktn_ref[0], shift, 1)
    vtn = pltpu.roll(vtn_ref[0], shift, 1)
    q_tile = q_ref[0] * (HEAD_DIM ** -0.5)
    per_tile = 8 // t_new
    q = q_tile[0:t_new]
    for i in range(1, per_tile):
        q = jnp.where(lax.rem(b, per_tile) == i, q_tile[i * t_new:(i + 1) * t_new], q)
    qrep = jnp.broadcast_to(q[:, None, :], (t_new, N_HEADS, D_ATT)).reshape(rows, D_ATT)
    row_h = lax.broadcasted_iota(jnp.int32, (rows, D_ATT), 0) % N_HEADS
    col_h = lax.broadcasted_iota(jnp.int32, (rows, D_ATT), 1) // HEAD_DIM
    own = row_h == col_h
    qbd = jnp.where(own, qrep, 0.0).astype(BF16)

    s = jnp.dot(qbd, kt_ref[0].astype(BF16), preferred_element_type=F32) + logc_ref[:, :wb]
    sn = jnp.dot(qbd, ktn.astype(BF16), preferred_element_type=F32) + logc_ref[:, wb:]
    m = jnp.maximum(jnp.max(s, axis=-1, keepdims=True), jnp.max(sn, axis=-1, keepdims=True))
    p = jnp.exp(s - m)
    pn = jnp.exp(sn - m)
    l = jnp.sum(p, axis=-1, keepdims=True) + jnp.sum(pn, axis=-1, keepdims=True)
    o = (lax.dot_general(p.astype(BF16), vt_ref[0].astype(BF16), contract_last,
                         preferred_element_type=F32)
         + lax.dot_general(pn.astype(BF16), vtn.astype(BF16), contract_last,
                           preferred_element_type=F32))
    o = jnp.where(own, o, 0.0) / l
    att_ref[0] = jnp.sum(o.reshape(t_new, N_HEADS, D_ATT), axis=1)

    lane = lax.broadcasted_iota(jnp.int32, (1, LANES), 1)
    keep = lane < LANES - t_new
    n_tiles = wb // LANES
    for src, new, dst in ((kt_ref, ktn, kto_ref), (vt_ref, vtn, vto_ref)):
        cur = pltpu.roll(src[0, :, 0:LANES], LANES - t_new, 1)
        for j in range(n_tiles):
            if j + 1 < n_tiles:
                nxt = pltpu.roll(src[0, :, (j + 1) * LANES:(j + 2) * LANES], LANES - t_new, 1)
            else:
                nxt = pltpu.roll(new, LANES - t_new, 1)
            dst[0, :, j * LANES:(j + 1) * LANES] = jnp.where(keep, cur, nxt)
            cur = nxt


def _sattn(q, kt, vt, ktn, vtn, logc, *, t_new):
    bd, _, wb = kt.shape
    big = pl.BlockSpec((1, D_ATT, wb), lambda i: (i, 0, 0))
    return pl.pallas_call(
        functools.partial(_sattn_kernel, wb=wb, t_new=t_new),
        grid=(bd,),
        in_specs=[pl.BlockSpec((1, 8, D_ATT), lambda i: (i // (8 // t_new), 0, 0)), big, big,
                  _const_spec(ktn.shape), _const_spec(vtn.shape), _const_spec(logc.shape)],
        out_specs=[pl.BlockSpec((1, t_new, D_ATT), lambda i: (i, 0, 0)), big, big],
        out_shape=[jax.ShapeDtypeStruct((bd, t_new, D_ATT), F32),
                   jax.ShapeDtypeStruct(kt.shape, F32), jax.ShapeDtypeStruct(vt.shape, F32)],
        compiler_params=pltpu.CompilerParams(dimension_semantics=("arbitrary",),
                                             vmem_limit_bytes=VMEM_LIMIT),
        name="sattn",
    )(q, kt, vt, ktn, vtn, logc)


def _ffn_kernel(x_ref, att_ref, sg_ref, woa_ref, wos_ref, g2_ref, wg_ref, wu_ref, wd_ref, gf_ref,
                y_ref, a_ref):
    x1 = (x_ref[...]
          + jnp.dot(att_ref[...].astype(BF16), woa_ref[...], preferred_element_type=F32)
          + jnp.dot(sg_ref[...], wos_ref[...], preferred_element_type=F32))
    ms = jnp.mean(x1 * x1, axis=-1, keepdims=True)
    h = (x1 * lax.rsqrt(ms + EPS) * g2_ref[...]).astype(BF16)
    for c in range(N_FF_CHUNKS):
        cols = slice(c * FF_CHUNK, (c + 1) * FF_CHUNK)
        g = jnp.dot(h, wg_ref[:, cols], preferred_element_type=F32)
        u = jnp.dot(h, wu_ref[:, cols], preferred_element_type=F32)
        a_ref[:, cols] = (g * jax.nn.sigmoid(g) * u).astype(BF16)
    x2 = x1 + jnp.dot(a_ref[...], wd_ref[...], preferred_element_type=F32)
    ms2 = jnp.mean(x2 * x2, axis=-1, keepdims=True)
    y_ref[...] = x2 * lax.rsqrt(ms2 + EPS) * gf_ref[...]


def _ffn(x, att, sg, woa, wos, g2, wg, wu, wd, gf, *, tm):
    t = x.shape[0]
    row = lambda shape: pl.BlockSpec(shape, lambda i: (i, 0))
    return pl.pallas_call(
        _ffn_kernel,
        grid=(t // tm,),
        in_specs=[row((tm, D_MODEL)), row((tm, D_ATT)), row((tm, D_SGU)),
                  _const_spec(woa.shape), _const_spec(wos.shape), _const_spec(g2.shape),
                  _const_spec(wg.shape), _const_spec(wu.shape), _const_spec(wd.shape),
                  _const_spec(gf.shape)],
        out_specs=row((tm, D_MODEL)),
        out_shape=jax.ShapeDtypeStruct((t, D_MODEL), F32),
        scratch_shapes=[pltpu.VMEM((tm, D_FF), BF16)],
        compiler_params=pltpu.CompilerParams(dimension_semantics=("arbitrary",),
                                             vmem_limit_bytes=VMEM_LIMIT),
        name="ffn",
    )(x, att, sg, woa, wos, g2, wg, wu, wd, gf)


def _rel_bucket(dist):
    max_exact = N_BUCKETS // 2
    df = jnp.maximum(dist, 1).astype(F32)
    large = max_exact + (jnp.log(df / max_exact) / math.log(MAX_DISTANCE / max_exact)
                         * (N_BUCKETS - max_exact)).astype(jnp.int32)
    large = jnp.minimum(large, N_BUCKETS - 1)
    return jnp.where(dist < max_exact, dist, large)


def _branch_bias_reversed(rel_bias, w, d):
    nj = w // d + 1
    dist = (nj - 1 - jnp.arange(nj, dtype=jnp.int32)) * d
    return rel_bias[_rel_bucket(dist)].T.astype(F32)


def _prompt_bias_rows(rev):
    rows = [jnp.pad(r, ((0, 0), (0, 2 * Q_BLOCK - r.shape[1])), constant_values=NEG_INF)
            for r in rev]
    return jnp.stack(rows) * LOG2E


def _sample_bias_table(rev, wb, t_new):
    width = wb + LANES
    per_branch = []
    for (w, d), r in zip(BRANCHES, rev):
        nj = r.shape[1]
        if d > 1:
            fill = jnp.full((N_HEADS, nj, d - 1), NEG_INF, F32)
            r = jnp.concatenate([r[:, :, None], fill], axis=2).reshape(N_HEADS, nj * d)
        rows = []
        for t in range(t_new):
            base = wb + t - (nj - 1) * d
            rows.append(jnp.pad(r, ((0, 0), (base, width - base - nj * d)),
                                constant_values=NEG_INF))
        per_branch.append(jnp.stack(rows))
    x = jnp.stack(per_branch)
    mx = jnp.max(x, axis=0)
    logc = mx + jnp.log(jnp.sum(jnp.exp(x - mx), axis=0))
    return logc.reshape(t_new * N_HEADS, width)


def kernel(x_prompt, x_sample, cache_k_win, cache_v_win, norm1_g, w_in, sgu_ln_g, sgu_ln_b,
           sgu_w, sgu_b, w_out, norm2_g, w_gate, w_up, w_down, rel_bias, final_g):
    depth = w_in.shape[0]
    assert depth == 1
    b, s, _ = x_prompt.shape
    bd, t_new, _ = x_sample.shape
    wb = cache_k_win.shape[2]
    assert bd * t_new == CHUNK and s % (Q_BLOCK * 16) == 0 and wb == WINDOW
    assert (s // Q_BLOCK) % ATTN_GROUP == 0 and s % PREP_ROWS == 0

    l = 0
    w_in_b = w_in[l].astype(BF16)
    woa = w_out[l, :D_ATT].astype(BF16)
    wos = w_out[l, D_ATT:].astype(BF16)
    wg = w_gate[l].astype(BF16)
    wu = w_up[l].astype(BF16)
    wd = w_down[l].astype(BF16)
    g1 = norm1_g[l][None]
    g2 = norm2_g[l][None]
    gf = final_g[None]
    ln_g = sgu_ln_g[l][None]
    ln_b = sgu_ln_b[l][None]

    causal = jnp.tril(jnp.ones((CHUNK, CHUNK), F32))
    wm = sgu_w[l] * causal
    mix_p = wm.astype(BF16)
    mixb_p = jnp.repeat(sgu_b[l].T, HEAD_DIM, axis=1)
    eye = jnp.eye(bd, dtype=F32)
    mix_s = jnp.einsum('ab,gts->gatbs', eye, wm[:, :t_new, :t_new]).reshape(
        N_GROUPS, CHUNK, CHUNK).astype(BF16)
    mixb_s = jnp.tile(mixb_p[:t_new], (bd, 1))

    rev = [_branch_bias_reversed(rel_bias, w, d) for w, d in BRANCHES]

    xp = x_prompt.reshape(b * s, D_MODEL)
    q, k, v, sg, kt_p, vt_p = _proj(xp, g1, w_in_b, ln_g, ln_b, mix_p, mixb_p,
                                    tm=512, emit_vn=False, emit_t=True, seq_len=s)
    att = _attn(q.reshape(b, s, D_ATT), k.reshape(b, s, D_ATT), v.reshape(b, s, D_ATT),
                _prompt_bias_rows(rev))
    y_prompt = _ffn(xp, att.reshape(b * s, D_ATT), sg, woa, wos, g2, wg, wu, wd, gf,
                    tm=512).reshape(b, s, D_MODEL)
    nw = min(WINDOW, s)
    new_k_p = kt_p.reshape(1, b, N_HEADS, HEAD_DIM, nw).transpose(0, 1, 4, 2, 3)
    new_v_p = vt_p.reshape(1, b, N_HEADS, HEAD_DIM, nw).transpose(0, 1, 4, 2, 3)

    xs = x_sample.reshape(bd * t_new, D_MODEL)
    qs, _, _, sgs, vn_s, kt_n, vt_n = _proj(xs, g1, w_in_b, ln_g, ln_b, mix_s, mixb_s,
                                            tm=CHUNK, emit_vn=True, emit_t=True, seq_len=CHUNK)
    kt_c = cache_k_win[l].transpose(0, 2, 3, 1).reshape(bd, D_ATT, wb)
    vt_c = cache_v_win[l].transpose(0, 2, 3, 1).reshape(bd, D_ATT, wb)

    logc = _sample_bias_table(rev, wb, t_new)
    att_s, kt_o, vt_o = _sattn(qs.reshape(bd * t_new // 8, 8, D_ATT), kt_c, vt_c, kt_n, vt_n, logc,
                               t_new=t_new)
    y_sample = _ffn(xs, att_s.reshape(bd * t_new, D_ATT), sgs, woa, wos, g2, wg, wu,
                    wd, gf, tm=CHUNK).reshape(bd, t_new, D_MODEL)
    new_k_s = kt_o.reshape(1, bd, N_HEADS, HEAD_DIM, wb).transpose(0, 1, 4, 2, 3)
    new_v_s = vt_o.reshape(1, bd, N_HEADS, HEAD_DIM, wb).transpose(0, 1, 4, 2, 3)
    sgu_v = vn_s.reshape(1, bd, t_new, D_SGU)

    return (y_prompt, y_sample, new_k_p, new_v_p, new_k_s, new_v_s, sgu_v)
```

```python
import functools
import math

import jax
import jax.numpy as jnp
from jax import lax
from jax.experimental import pallas as pl
from jax.experimental.pallas import tpu as pltpu

D_MODEL = 1024
N_HEADS = 8
HEAD_DIM = 64
D_ATT = N_HEADS * HEAD_DIM
N_GROUPS = 8
D_SGU = 512
CHUNK = 128
BRANCHES = ((128, 1), (512, 4), (2048, 16))
WINDOW = 2048
Q_BLOCK = 128
N_BUCKETS = 32
MAX_DISTANCE = WINDOW
D_FF = 2816
EPS = 1e-6
NEG_INF = -1e30
LOG2E = math.log2(math.e)

LANES = 128
FF_CHUNK = 256
N_FF_CHUNKS = D_FF // FF_CHUNK
VMEM_LIMIT = 48 * 1024 * 1024
VMEM_LIMIT_FUSED = 60 * 1024 * 1024
SIDE_TILES_PER_STEP = 6
ATTN_GROUP = 8
PREP_ROWS = 256

F32 = jnp.float32
BF16 = jnp.bfloat16


def _const_spec(shape):
    nd = len(shape)
    return pl.BlockSpec(shape, lambda *_: (0,) * nd, pipeline_mode=pl.Buffered(1))


def _proj_kernel(x_ref, g1_ref, w_ref, lng_ref, lnb_ref, mix_ref, mixb_ref, *outs,
                 tm, emit_vn, emit_t):
    q_ref, k_ref, v_ref, sg_ref = outs[:4]
    rest = outs[4:]
    lane = lax.broadcasted_iota(jnp.int32, (1, LANES), 1)
    low = lane < HEAD_DIM

    x = x_ref[...]
    xg = (x * g1_ref[...]).astype(BF16)
    r = lax.rsqrt(jnp.mean(x * x, axis=-1, keepdims=True) + EPS)

    def proj(c0):
        return jnp.dot(xg, w_ref[:, c0:c0 + D_ATT], preferred_element_type=F32) * r

    u = proj(3 * D_ATT)
    vg = proj(3 * D_ATT + D_SGU)
    mu = jnp.mean(vg, axis=-1, keepdims=True)
    dv = vg - mu
    var = jnp.mean(dv * dv, axis=-1, keepdims=True)
    vn = dv * lax.rsqrt(var + EPS) * lng_ref[...] + lnb_ref[...]
    if emit_vn:
        rest[0][...] = vn

    for c0 in range(0, tm, CHUNK):
        for s in range(D_SGU // LANES):
            cols = slice(s * LANES, (s + 1) * LANES)
            slab = vn[c0:c0 + CHUNK, cols]
            lo = jnp.where(low, slab, 0.0).astype(BF16)
            hi = jnp.where(low, 0.0, slab).astype(BF16)
            gate = (jnp.dot(mix_ref[2 * s], lo, preferred_element_type=F32)
                    + jnp.dot(mix_ref[2 * s + 1], hi, preferred_element_type=F32)
                    + mixb_ref[:, cols])
            sg_ref[c0:c0 + CHUNK, cols] = (u[c0:c0 + CHUNK, cols] * gate).astype(BF16)

    q_ref[...] = proj(0)
    k = proj(D_ATT)
    v = proj(2 * D_ATT)
    k_ref[...] = k
    v_ref[...] = v
    if emit_t:
        kt_ref, vt_ref = rest[-2:]
        kt_ref[0] = k.T
        vt_ref[0] = v.T


def _proj(x, g1, w_in, ln_g, ln_b, mix, mixb, *, tm, emit_vn, emit_t, seq_len=None):
    t = x.shape[0]
    n_tiles = t // tm
    row = lambda shape: pl.BlockSpec(shape, lambda i: (i, 0))
    out_shape = [jax.ShapeDtypeStruct((t, D_ATT), F32)] * 3 + [jax.ShapeDtypeStruct((t, D_SGU), BF16)]
    out_specs = [row((tm, D_ATT))] * 3 + [row((tm, D_SGU))]
    if emit_vn:
        out_shape.append(jax.ShapeDtypeStruct((t, D_SGU), F32))
        out_specs.append(row((tm, D_SGU)))
    if emit_t:
        tiles_per_seq = seq_len // tm
        win = min(WINDOW, seq_len)
        first = tiles_per_seq - win // tm

        def t_map(i):
            return (i // tiles_per_seq, 0, jnp.maximum(i % tiles_per_seq - first, 0))

        for _ in range(2):
            out_shape.append(jax.ShapeDtypeStruct((t // seq_len, D_ATT, win), F32))
            out_specs.append(pl.BlockSpec((1, D_ATT, tm), t_map))
    kern = functools.partial(_proj_kernel, tm=tm, emit_vn=emit_vn, emit_t=emit_t)
    return pl.pallas_call(
        kern,
        grid=(n_tiles,),
        in_specs=[row((tm, D_MODEL)), _const_spec(g1.shape), _const_spec(w_in.shape),
                  _const_spec(ln_g.shape), _const_spec(ln_b.shape), _const_spec(mix.shape),
                  _const_spec(mixb.shape)],
        out_specs=out_specs,
        out_shape=out_shape,
        compiler_params=pltpu.CompilerParams(dimension_semantics=("arbitrary",),
                                             vmem_limit_bytes=VMEM_LIMIT),
        name="proj_t" if emit_t else "proj_s",
    )(x, g1, w_in, ln_g, ln_b, mix, mixb)


def _div_pow2(x, n):
    assert n & (n - 1) == 0
    return lax.shift_right_logical(x, n.bit_length() - 1)


def _mod_pow2(x, n):
    assert n & (n - 1) == 0
    return lax.bitwise_and(x, n - 1)


def _attn_kernel(q_ref, k_ref, v_ref, r0_ref, o_ref, tab_ref, acc_ref, m_ref, l_ref,
                 q0_ref, q1_ref, kd_ref, v0_ref, v1_ref, s_ref, mx_ref, *, seq_len):
    hp = pl.program_id(1)
    lane = lax.broadcasted_iota(jnp.int32, (1, LANES), 1)
    low = lane < HEAD_DIM
    contract_last = (((1,), (1,)), ((), ()))
    neg = NEG_INF * LOG2E
    quarter = seq_len // 4

    col = lax.broadcasted_iota(jnp.int32, (Q_BLOCK, 2 * Q_BLOCK), 1)
    for br in range(len(BRANCHES)):
        for h in range(2):
            base = jnp.broadcast_to(r0_ref[br, pl.ds(2 * hp + h, 1), :], (Q_BLOCK, 2 * Q_BLOCK))
            t = pltpu.roll(base, 0, 1, stride=1, stride_axis=0)
            tab_ref[br, 0, h] = t
            tab_ref[br, 1, h] = jnp.where(col >= Q_BLOCK, t, neg)
    zeros = jnp.zeros((Q_BLOCK, LANES), BF16)
    kd_ref[0:Q_BLOCK, :] = zeros
    v0_ref[0:Q_BLOCK, :] = zeros
    v1_ref[0:Q_BLOCK, :] = zeros

    def strided(start, n, d):
        return pl.ds(start, n) if d == 1 else pl.ds(start, n, stride=d)

    for br, (w, d) in enumerate(BRANCHES):
        sub_len = seq_len // d
        nb = sub_len // Q_BLOCK

        def prep(c, carry, d=d, sub_len=sub_len):
            de0 = c * PREP_ROWS
            dst = pl.ds(pl.multiple_of(de0, PREP_ROWS), PREP_ROWS)
            dstp = pl.ds(pl.multiple_of(de0 + Q_BLOCK, Q_BLOCK), PREP_ROWS)
            if d <= 4:
                src = strided(_div_pow2(de0, sub_len) + _mod_pow2(de0, sub_len) * d, PREP_ROWS, d)
                qq, kk, vv = q_ref[0, src, :], k_ref[0, src, :], v_ref[0, src, :]
            else:
                res, m0 = _div_pow2(de0, sub_len), _mod_pow2(de0, sub_len)
                src = pl.ds(_mod_pow2(res, 4) * quarter + m0 * (d // 4) + _div_pow2(res, 4),
                            PREP_ROWS, stride=d // 4)
                qq, kk, vv = acc_ref[2, src, :], m_ref[2, src, :], l_ref[2, src, :]
            if d == 4:
                acc_ref[2, dst, :] = qq
                m_ref[2, dst, :] = kk
                l_ref[2, dst, :] = vv
            qs = qq * (HEAD_DIM ** -0.5 * LOG2E)
            q0_ref[dst, :] = jnp.where(low, qs, 0.0).astype(BF16)
            q1_ref[dst, :] = jnp.where(low, 0.0, qs).astype(BF16)
            kd_ref[dstp, :] = kk.astype(BF16)
            v0_ref[dstp, :] = jnp.where(low, vv, 1.0).astype(BF16)
            v1_ref[dstp, :] = jnp.where(low, 1.0, vv).astype(BF16)
            return carry

        lax.fori_loop(0, seq_len // PREP_ROWS, prep, 0)

        def score(i, br=br, nb=nb):
            for u in range(ATTN_GROUP):
                g = i * ATTN_GROUP + u
                rq = pl.ds(pl.multiple_of(g * Q_BLOCK, Q_BLOCK), Q_BLOCK)
                rk = pl.ds(pl.multiple_of(g * Q_BLOCK, Q_BLOCK), 2 * Q_BLOCK)
                first = jnp.where(_mod_pow2(g, nb) == 0, 1, 0)
                kb = kd_ref[rk, :]
                for h, qh_ref in enumerate((q0_ref, q1_ref)):
                    s = lax.dot_general(qh_ref[rq, :], kb, contract_last,
                                        preferred_element_type=F32)
                    s = s + tab_ref[br, first, h]
                    s_ref[u, h] = s
                    mx_ref[u, h] = jnp.broadcast_to(jnp.max(s, axis=-1, keepdims=True),
                                                    (Q_BLOCK, LANES))

        def finish(i, br=br, d=d, nb=nb):
            for u in range(ATTN_GROUP):
                g = i * ATTN_GROUP + u
                rk = pl.ds(pl.multiple_of(g * Q_BLOCK, Q_BLOCK), 2 * Q_BLOCK)
                outs = []
                for h, vh_ref in enumerate((v0_ref, v1_ref)):
                    m = mx_ref[u, h]
                    p = jnp.concatenate([jnp.exp2(s_ref[u, h, :, :LANES] - m),
                                         jnp.exp2(s_ref[u, h, :, LANES:] - m)], axis=1)
                    outs.append((jnp.dot(p.astype(BF16), vh_ref[rk, :],
                                         preferred_element_type=F32), m))
                (o0, m0), (o1, m1) = outs
                if d <= 4:
                    idx = pl.ds(pl.multiple_of(g * Q_BLOCK, Q_BLOCK), Q_BLOCK)
                else:
                    res, blk = _div_pow2(g, nb), _mod_pow2(g, nb)
                    idx = pl.ds(_mod_pow2(res, 4) * quarter + blk * (Q_BLOCK * d // 4)
                                + _div_pow2(res, 4), Q_BLOCK, stride=d // 4)
                acc_ref[br, idx, :] = jnp.where(low, o0, o1)
                l_ref[br, idx, :] = pltpu.roll(jnp.where(low, o1, o0), HEAD_DIM, 1)
                m_ref[br, idx, :] = jnp.where(low, m0, m1)

        def step(i, carry):
            finish(i - 1)
            score(i)
            return carry

        n_groups = seq_len // Q_BLOCK // ATTN_GROUP
        score(0)
        lax.fori_loop(1, n_groups, step, 0)
        finish(n_groups - 1)

    def merge(c, carry):
        de0 = c * PREP_ROWS
        dil = pl.ds(pl.multiple_of(de0, PREP_ROWS), PREP_ROWS)
        nat = pl.ds(_div_pow2(de0, quarter) + _mod_pow2(de0, quarter) * 4, PREP_ROWS, stride=4)
        ms = (m_ref[0, nat, :], m_ref[1, dil, :], m_ref[2, dil, :])
        m_all = jnp.maximum(jnp.maximum(ms[0], ms[1]), ms[2])
        ws = [jnp.exp2(m - m_all) for m in ms]
        num = (acc_ref[0, nat, :] * ws[0] + acc_ref[1, dil, :] * ws[1] + acc_ref[2, dil, :] * ws[2])
        den = l_ref[0, nat, :] * ws[0] + l_ref[1, dil, :] * ws[1] + l_ref[2, dil, :] * ws[2]
        acc_ref[0, nat, :] = num / den
        return carry

    lax.fori_loop(0, seq_len // PREP_ROWS, merge, 0)

    rows_per = 512

    def fin(i, c):
        sl = pl.ds(pl.multiple_of(i * rows_per, rows_per), rows_per)
        o_ref[0, sl, :] = acc_ref[0, sl, :].astype(o_ref.dtype)
        return c

    lax.fori_loop(0, seq_len // rows_per, fin, 0)


def _attn(q, k, v, r0):
    b, s, _ = q.shape
    n_br = len(BRANCHES)
    blk = pl.BlockSpec((1, s, LANES), lambda i, j: (i, 0, j))
    state = pltpu.VMEM((n_br, s, LANES), F32)
    qd = pltpu.VMEM((s, LANES), BF16)
    kd = pltpu.VMEM((s + Q_BLOCK, LANES), BF16)
    return pl.pallas_call(
        functools.partial(_attn_kernel, seq_len=s),
        grid=(b, D_ATT // LANES),
        in_specs=[blk, blk, blk, _const_spec(r0.shape)],
        out_specs=blk,
        out_shape=jax.ShapeDtypeStruct((b, s, D_ATT), BF16),
        scratch_shapes=[pltpu.VMEM((n_br, 2, 2, Q_BLOCK, 2 * Q_BLOCK), F32),
                        state, state, state, qd, qd, kd, kd, kd,
                        pltpu.VMEM((ATTN_GROUP, 2, Q_BLOCK, 2 * Q_BLOCK), F32),
                        pltpu.VMEM((ATTN_GROUP, 2, Q_BLOCK, LANES), F32)],
        compiler_params=pltpu.CompilerParams(dimension_semantics=("arbitrary", "arbitrary"),
                                             vmem_limit_bytes=VMEM_LIMIT),
        name="attn",
    )(q, k, v, r0)


def _sample_side_steps(b, q_ref, ktn_ref, vtn_ref, logc_ref, kin_ref, vin_ref, kout_ref, vout_ref,
                       att_ref, *, wb, t_new):
    rows = t_new * N_HEADS
    contract_last = (((1,), (1,)), ((), ()))
    lane = lax.broadcasted_iota(jnp.int32, (1, LANES), 1)
    keep = lane < LANES - t_new
    n_tiles = wb // LANES
    st = {}

    def scores():
        shift = lax.bitwise_and(LANES - t_new * b, LANES - 1)
        st["ktn"] = pltpu.roll(ktn_ref[0], shift, 1)
        st["vtn"] = pltpu.roll(vtn_ref[0], shift, 1)
        q_tile = q_ref[0] * (HEAD_DIM ** -0.5)
        per_tile = 8 // t_new
        q = q_tile[0:t_new]
        for i in range(1, per_tile):
            q = jnp.where(lax.rem(b, per_tile) == i, q_tile[i * t_new:(i + 1) * t_new], q)
        qrep = jnp.broadcast_to(q[:, None, :], (t_new, N_HEADS, D_ATT)).reshape(rows, D_ATT)
        row_h = lax.broadcasted_iota(jnp.int32, (rows, D_ATT), 0) % N_HEADS
        col_h = lax.broadcasted_iota(jnp.int32, (rows, D_ATT), 1) // HEAD_DIM
        st["own"] = row_h == col_h
        qbd = jnp.where(st["own"], qrep, 0.0).astype(BF16)
        st["s"] = (jnp.dot(qbd, kin_ref[...].astype(BF16), preferred_element_type=F32)
                   + logc_ref[:, :wb])
        st["sn"] = (jnp.dot(qbd, st["ktn"].astype(BF16), preferred_element_type=F32)
                    + logc_ref[:, wb:])

    def softmax():
        s, sn = st["s"], st["sn"]
        m = jnp.maximum(jnp.max(s, axis=-1, keepdims=True), jnp.max(sn, axis=-1, keepdims=True))
        p = jnp.exp(s - m)
        pn = jnp.exp(sn - m)
        st["l"] = jnp.sum(p, axis=-1, keepdims=True) + jnp.sum(pn, axis=-1, keepdims=True)
        st["p"], st["pn"] = p.astype(BF16), pn.astype(BF16)

    def values():
        o = (lax.dot_general(st["p"], vin_ref[...].astype(BF16), contract_last,
                             preferred_element_type=F32)
             + lax.dot_general(st["pn"], st["vtn"].astype(BF16), contract_last,
                               preferred_element_type=F32))
        o = jnp.where(st["own"], o, 0.0) / st["l"]
        att_ref[0] = jnp.sum(o.reshape(t_new, N_HEADS, D_ATT), axis=1)

    def shift_tiles(src_ref, new_key, dst_ref, j0, j1):
        def run():
            cur = st.get(("cur", new_key))
            if cur is None:
                cur = pltpu.roll(src_ref[:, 0:LANES], LANES - t_new, 1)
            for j in range(j0, j1):
                if j + 1 < n_tiles:
                    nxt = pltpu.roll(src_ref[:, (j + 1) * LANES:(j + 2) * LANES], LANES - t_new, 1)
                else:
                    nxt = pltpu.roll(st[new_key], LANES - t_new, 1)
                dst_ref[:, j * LANES:(j + 1) * LANES] = jnp.where(keep, cur, nxt)
                cur = nxt
            st[("cur", new_key)] = cur
        return run

    per = SIDE_TILES_PER_STEP
    k_steps = [scores] + [shift_tiles(kin_ref, "ktn", kout_ref, j0, min(j0 + per, n_tiles))
                          for j0 in range(0, n_tiles, per)]
    v_steps = [softmax, values] + [shift_tiles(vin_ref, "vtn", vout_ref, j0, min(j0 + per, n_tiles))
                                   for j0 in range(0, n_tiles, per)]
    return k_steps, v_steps


def _ffn_body(x_ref, att_ref, sg_ref, woa_ref, wos_ref, g2_ref, wg_ref, wu_ref, wd_ref, gf_ref,
              y_ref, a_ref, side_steps=()):
    side = list(side_steps)
    assert len(side) < N_FF_CHUNKS
    x1 = (x_ref[...]
          + jnp.dot(att_ref[...].astype(BF16), woa_ref[...], preferred_element_type=F32)
          + jnp.dot(sg_ref[...], wos_ref[...], preferred_element_type=F32))
    ms = jnp.mean(x1 * x1, axis=-1, keepdims=True)
    h = (x1 * lax.rsqrt(ms + EPS) * g2_ref[...]).astype(BF16)
    for c in range(N_FF_CHUNKS):
        cols = slice(c * FF_CHUNK, (c + 1) * FF_CHUNK)
        g = jnp.dot(h, wg_ref[:, cols], preferred_element_type=F32)
        u = jnp.dot(h, wu_ref[:, cols], preferred_element_type=F32)
        a_ref[:, cols] = (g * jax.nn.sigmoid(g) * u).astype(BF16)
        if side:
            side.pop(0)()
    x2 = x1 + jnp.dot(a_ref[...], wd_ref[...], preferred_element_type=F32)
    ms2 = jnp.mean(x2 * x2, axis=-1, keepdims=True)
    y_ref[...] = x2 * lax.rsqrt(ms2 + EPS) * gf_ref[...]


def _ffn_kernel(*refs):
    _ffn_body(*refs)


def _ffn_side_kernel(x_ref, att_ref, sg_ref, woa_ref, wos_ref, g2_ref, wg_ref, wu_ref, wd_ref, gf_ref,
                     q_ref, ktn_ref, vtn_ref, logc_ref, kt_hbm, vt_hbm,
                     y_ref, atts_ref, kto_hbm, vto_hbm,
                     a_ref, kin_ref, vin_ref, kout_ref, vout_ref, sems, *, wb, t_new):
    i = pl.program_id(0)
    last = pl.num_programs(0) - 1

    def fetch(j):
        return (pltpu.make_async_copy(kt_hbm.at[j], kin_ref, sems.at[0]),
                pltpu.make_async_copy(vt_hbm.at[j], vin_ref, sems.at[1]))

    def flush(j):
        return (pltpu.make_async_copy(kout_ref, kto_hbm.at[j], sems.at[2]),
                pltpu.make_async_copy(vout_ref, vto_hbm.at[j], sems.at[3]))

    @pl.when(i == 0)
    def _():
        for cp in fetch(0):
            cp.start()

    for cp in fetch(i):
        cp.wait()

    @pl.when(i > 0)
    def _():
        for cp in flush(i - 1):
            cp.wait()

    def then_swap(step, which):
        def run():
            step()
            flush(i)[which].start()
            fetch(jnp.minimum(i + 1, last))[which].start()
        return run

    k_steps, v_steps = _sample_side_steps(i, q_ref, ktn_ref, vtn_ref, logc_ref, kin_ref, vin_ref,
                                          kout_ref, vout_ref, atts_ref, wb=wb, t_new=t_new)
    k_steps[-1] = then_swap(k_steps[-1], 0)
    v_steps[-1] = then_swap(v_steps[-1], 1)
    _ffn_body(x_ref, att_ref, sg_ref, woa_ref, wos_ref, g2_ref, wg_ref, wu_ref, wd_ref, gf_ref,
              y_ref, a_ref, side_steps=k_steps + v_steps)

    @pl.when(i == last)
    def _():
        for cp in fetch(i) + flush(i):
            cp.wait()


def _ffn_specs(tm, weights):
    row = lambda shape: pl.BlockSpec(shape, lambda i: (i, 0))
    return ([row((tm, D_MODEL)), row((tm, D_ATT)), row((tm, D_SGU))]
            + [_const_spec(w.shape) for w in weights]), row((tm, D_MODEL))


def _ffn(x, att, sg, weights, *, tm):
    t = x.shape[0]
    in_specs, out_spec = _ffn_specs(tm, weights)
    return pl.pallas_call(
        _ffn_kernel,
        grid=(t // tm,),
        in_specs=in_specs,
        out_specs=out_spec,
        out_shape=jax.ShapeDtypeStruct((t, D_MODEL), F32),
        scratch_shapes=[pltpu.VMEM((tm, D_FF), BF16)],
        compiler_params=pltpu.CompilerParams(dimension_semantics=("arbitrary",),
                                             vmem_limit_bytes=VMEM_LIMIT),
        name="ffn",
    )(x, att, sg, *weights)


def _ffn_with_sample_side(x, att, sg, weights, q_s, ktn, vtn, logc, kt, vt, *, tm, t_new):
    t = x.shape[0]
    bd, _, wb = kt.shape
    assert t // tm == bd
    in_specs, out_spec = _ffn_specs(tm, weights)
    any_spec = pl.BlockSpec(memory_space=pl.ANY)
    window = pltpu.VMEM((D_ATT, wb), F32)
    return pl.pallas_call(
        functools.partial(_ffn_side_kernel, wb=wb, t_new=t_new),
        grid=(bd,),
        in_specs=in_specs + [pl.BlockSpec((1, 8, D_ATT), lambda i: (i // (8 // t_new), 0, 0)),
                             _const_spec(ktn.shape), _const_spec(vtn.shape),
                             _const_spec(logc.shape), any_spec, any_spec],
        out_specs=[out_spec, pl.BlockSpec((1, t_new, D_ATT), lambda i: (i, 0, 0)),
                   any_spec, any_spec],
        out_shape=[jax.ShapeDtypeStruct((t, D_MODEL), F32),
                   jax.ShapeDtypeStruct((bd, t_new, D_ATT), F32),
                   jax.ShapeDtypeStruct(kt.shape, F32), jax.ShapeDtypeStruct(vt.shape, F32)],
        scratch_shapes=[pltpu.VMEM((tm, D_FF), BF16), window, window, window, window,
                        pltpu.SemaphoreType.DMA((4,))],
        compiler_params=pltpu.CompilerParams(dimension_semantics=("arbitrary",),
                                             vmem_limit_bytes=VMEM_LIMIT_FUSED),
        name="ffn_side",
    )(x, att, sg, *weights, q_s, ktn, vtn, logc, kt, vt)


def _rel_bucket(dist):
    max_exact = N_BUCKETS // 2
    df = jnp.maximum(dist, 1).astype(F32)
    large = max_exact + (jnp.log(df / max_exact) / math.log(MAX_DISTANCE / max_exact)
                         * (N_BUCKETS - max_exact)).astype(jnp.int32)
    large = jnp.minimum(large, N_BUCKETS - 1)
    return jnp.where(dist < max_exact, dist, large)


def _branch_bias_reversed(rel_bias, w, d):
    nj = w // d + 1
    dist = (nj - 1 - jnp.arange(nj, dtype=jnp.int32)) * d
    return rel_bias[_rel_bucket(dist)].T.astype(F32)


def _prompt_bias_rows(rev):
    rows = [jnp.pad(r, ((0, 0), (0, 2 * Q_BLOCK - r.shape[1])), constant_values=NEG_INF)
            for r in rev]
    return jnp.stack(rows) * LOG2E


def _sample_bias_table(rev, wb, t_new):
    width = wb + LANES
    per_branch = []
    for (w, d), r in zip(BRANCHES, rev):
        nj = r.shape[1]
        if d > 1:
            fill = jnp.full((N_HEADS, nj, d - 1), NEG_INF, F32)
            r = jnp.concatenate([r[:, :, None], fill], axis=2).reshape(N_HEADS, nj * d)
        rows = []
        for t in range(t_new):
            base = wb + t - (nj - 1) * d
            rows.append(jnp.pad(r, ((0, 0), (base, width - base - nj * d)),
                                constant_values=NEG_INF))
        per_branch.append(jnp.stack(rows))
    x = jnp.stack(per_branch)
    mx = jnp.max(x, axis=0)
    logc = mx + jnp.log(jnp.sum(jnp.exp(x - mx), axis=0))
    return logc.reshape(t_new * N_HEADS, width)


def kernel(x_prompt, x_sample, cache_k_win, cache_v_win, norm1_g, w_in, sgu_ln_g, sgu_ln_b,
           sgu_w, sgu_b, w_out, norm2_g, w_gate, w_up, w_down, rel_bias, final_g):
    depth = w_in.shape[0]
    assert depth == 1
    b, s, _ = x_prompt.shape
    bd, t_new, _ = x_sample.shape
    wb = cache_k_win.shape[2]
    assert bd * t_new == CHUNK and s % (Q_BLOCK * 16) == 0 and wb == WINDOW
    assert (s // Q_BLOCK) % ATTN_GROUP == 0 and s % PREP_ROWS == 0

    l = 0
    w_in_b = w_in[l].astype(BF16)
    woa = w_out[l, :D_ATT].astype(BF16)
    wos = w_out[l, D_ATT:].astype(BF16)
    wg = w_gate[l].astype(BF16)
    wu = w_up[l].astype(BF16)
    wd = w_down[l].astype(BF16)
    g1 = norm1_g[l][None]
    g2 = norm2_g[l][None]
    gf = final_g[None]
    ln_g = sgu_ln_g[l][None]
    ln_b = sgu_ln_b[l][None]

    causal = jnp.tril(jnp.ones((CHUNK, CHUNK), F32))
    wm = sgu_w[l] * causal
    mix_p = wm.astype(BF16)
    mixb_p = jnp.repeat(sgu_b[l].T, HEAD_DIM, axis=1)
    eye = jnp.eye(bd, dtype=F32)
    mix_s = jnp.einsum('ab,gts->gatbs', eye, wm[:, :t_new, :t_new]).reshape(
        N_GROUPS, CHUNK, CHUNK).astype(BF16)
    mixb_s = jnp.tile(mixb_p[:t_new], (bd, 1))

    rev = [_branch_bias_reversed(rel_bias, w, d) for w, d in BRANCHES]

    xp = x_prompt.reshape(b * s, D_MODEL)
    q, k, v, sg, kt_p, vt_p = _proj(xp, g1, w_in_b, ln_g, ln_b, mix_p, mixb_p,
                                    tm=512, emit_vn=False, emit_t=True, seq_len=s)
    att = _attn(q.reshape(b, s, D_ATT), k.reshape(b, s, D_ATT), v.reshape(b, s, D_ATT),
                _prompt_bias_rows(rev))
    nw = min(WINDOW, s)
    new_k_p = kt_p.reshape(1, b, N_HEADS, HEAD_DIM, nw).transpose(0, 1, 4, 2, 3)
    new_v_p = vt_p.reshape(1, b, N_HEADS, HEAD_DIM, nw).transpose(0, 1, 4, 2, 3)

    xs = x_sample.reshape(bd * t_new, D_MODEL)
    qs, _, _, sgs, vn_s, kt_n, vt_n = _proj(xs, g1, w_in_b, ln_g, ln_b, mix_s, mixb_s,
                                            tm=CHUNK, emit_vn=True, emit_t=True, seq_len=CHUNK)
    kt_c = cache_k_win[l].transpose(0, 2, 3, 1).reshape(bd, D_ATT, wb)
    vt_c = cache_v_win[l].transpose(0, 2, 3, 1).reshape(bd, D_ATT, wb)

    logc = _sample_bias_table(rev, wb, t_new)
    weights = (woa, wos, g2, wg, wu, wd, gf)
    y_prompt, att_s, kt_o, vt_o = _ffn_with_sample_side(
        xp, att.reshape(b * s, D_ATT), sg, weights, qs.reshape(bd * t_new // 8, 8, D_ATT),
        kt_n, vt_n, logc, kt_c, vt_c, tm=(b * s) // bd, t_new=t_new)
    y_prompt = y_prompt.reshape(b, s, D_MODEL)
    y_sample = _ffn(xs, att_s.reshape(bd * t_new, D_ATT), sgs, weights,
                    tm=CHUNK).reshape(bd, t_new, D_MODEL)
    new_k_s = kt_o.reshape(1, bd, N_HEADS, HEAD_DIM, wb).transpose(0, 1, 4, 2, 3)
    new_v_s = vt_o.reshape(1, bd, N_HEADS, HEAD_DIM, wb).transpose(0, 1, 4, 2, 3)
    sgu_v = vn_s.reshape(1, bd, t_new, D_SGU)

    return (y_prompt, y_sample, new_k_p, new_v_p, new_k_s, new_v_s, sgu_v)
```

```python
import functools
import math

import jax
import jax.numpy as jnp
from jax import lax
from jax.experimental import pallas as pl
from jax.experimental.pallas import tpu as pltpu

D_MODEL = 1024
N_HEADS = 8
HEAD_DIM = 64
D_ATT = N_HEADS * HEAD_DIM
N_GROUPS = 8
D_SGU = 512
CHUNK = 128
BRANCHES = ((128, 1), (512, 4), (2048, 16))
WINDOW = 2048
Q_BLOCK = 128
N_BUCKETS = 32
MAX_DISTANCE = WINDOW
D_FF = 2816
EPS = 1e-6
NEG_INF = -1e30
LOG2E = math.log2(math.e)

LANES = 128
FF_CHUNK = 256
N_FF_CHUNKS = D_FF // FF_CHUNK
VMEM_LIMIT = 48 * 1024 * 1024
VMEM_LIMIT_FUSED = 60 * 1024 * 1024
SIDE_TILES_PER_STEP = 6
ATTN_GROUP = 8
PREP_ROWS = 256

F32 = jnp.float32
BF16 = jnp.bfloat16


def _const_spec(shape):
    nd = len(shape)
    return pl.BlockSpec(shape, lambda *_: (0,) * nd, pipeline_mode=pl.Buffered(1))


def _proj_kernel(x_ref, g1_ref, w_ref, lng_ref, lnb_ref, mix_ref, mixb_ref, *outs,
                 tm, emit_vn, emit_t):
    q_ref, k_ref, v_ref, sg_ref = outs[:4]
    rest = outs[4:]
    lane = lax.broadcasted_iota(jnp.int32, (1, LANES), 1)
    low = lane < HEAD_DIM

    x = x_ref[...]
    xg = (x * g1_ref[...]).astype(BF16)
    r = lax.rsqrt(jnp.mean(x * x, axis=-1, keepdims=True) + EPS)

    def proj(c0):
        return jnp.dot(xg, w_ref[:, c0:c0 + D_ATT], preferred_element_type=F32) * r

    vg = proj(3 * D_ATT + D_SGU)
    u = proj(3 * D_ATT)
    mu = jnp.mean(vg, axis=-1, keepdims=True)
    dv = vg - mu
    var = jnp.mean(dv * dv, axis=-1, keepdims=True)
    vn = dv * lax.rsqrt(var + EPS) * lng_ref[...] + lnb_ref[...]
    if emit_vn:
        rest[0][...] = vn
    q_ref[...] = proj(0)

    def gating(c0):
        for s in range(D_SGU // LANES):
            cols = slice(s * LANES, (s + 1) * LANES)
            slab = vn[c0:c0 + CHUNK, cols]
            lo = jnp.where(low, slab, 0.0).astype(BF16)
            hi = jnp.where(low, 0.0, slab).astype(BF16)
            gate = (jnp.dot(mix_ref[2 * s], lo, preferred_element_type=F32)
                    + jnp.dot(mix_ref[2 * s + 1], hi, preferred_element_type=F32)
                    + mixb_ref[:, cols])
            sg_ref[c0:c0 + CHUNK, cols] = (u[c0:c0 + CHUNK, cols] * gate).astype(BF16)

    chunks = list(range(0, tm, CHUNK))
    for c0 in chunks[:len(chunks) // 2]:
        gating(c0)
    k = proj(D_ATT)
    k_ref[...] = k
    for c0 in chunks[len(chunks) // 2:]:
        gating(c0)
    v = proj(2 * D_ATT)
    v_ref[...] = v
    if emit_t:
        kt_ref, vt_ref = rest[-2:]
        kt_ref[0] = k.T
        vt_ref[0] = v.T


def _proj(x, g1, w_in, ln_g, ln_b, mix, mixb, *, tm, emit_vn, emit_t, seq_len=None):
    t = x.shape[0]
    n_tiles = t // tm
    row = lambda shape: pl.BlockSpec(shape, lambda i: (i, 0))
    out_shape = [jax.ShapeDtypeStruct((t, D_ATT), F32)] * 3 + [jax.ShapeDtypeStruct((t, D_SGU), BF16)]
    out_specs = [row((tm, D_ATT))] * 3 + [row((tm, D_SGU))]
    if emit_vn:
        out_shape.append(jax.ShapeDtypeStruct((t, D_SGU), F32))
        out_specs.append(row((tm, D_SGU)))
    if emit_t:
        tiles_per_seq = seq_len // tm
        win = min(WINDOW, seq_len)
        first = tiles_per_seq - win // tm

        def t_map(i):
            return (i // tiles_per_seq, 0, jnp.maximum(i % tiles_per_seq - first, 0))

        for _ in range(2):
            out_shape.append(jax.ShapeDtypeStruct((t // seq_len, D_ATT, win), F32))
            out_specs.append(pl.BlockSpec((1, D_ATT, tm), t_map))
    kern = functools.partial(_proj_kernel, tm=tm, emit_vn=emit_vn, emit_t=emit_t)
    return pl.pallas_call(
        kern,
        grid=(n_tiles,),
        in_specs=[row((tm, D_MODEL)), _const_spec(g1.shape), _const_spec(w_in.shape),
                  _const_spec(ln_g.shape), _const_spec(ln_b.shape), _const_spec(mix.shape),
                  _const_spec(mixb.shape)],
        out_specs=out_specs,
        out_shape=out_shape,
        compiler_params=pltpu.CompilerParams(dimension_semantics=("arbitrary",),
                                             vmem_limit_bytes=VMEM_LIMIT),
        name="proj_t" if emit_t else "proj_s",
    )(x, g1, w_in, ln_g, ln_b, mix, mixb)


def _div_pow2(x, n):
    assert n & (n - 1) == 0
    return lax.shift_right_logical(x, n.bit_length() - 1)


def _mod_pow2(x, n):
    assert n & (n - 1) == 0
    return lax.bitwise_and(x, n - 1)


def _attn_kernel(q_ref, k_ref, v_ref, r0_ref, o_ref, tab_ref, acc_ref, m_ref, l_ref,
                 q0_ref, q1_ref, kd_ref, v0_ref, v1_ref, s_ref, mx_ref, *, seq_len):
    hp = pl.program_id(1)
    lane = lax.broadcasted_iota(jnp.int32, (1, LANES), 1)
    low = lane < HEAD_DIM
    contract_last = (((1,), (1,)), ((), ()))
    neg = NEG_INF * LOG2E
    quarter = seq_len // 4
    lo_f = jnp.where(low, 1.0, 0.0).astype(F32)
    hi_f = 1.0 - lo_f
    q_scale = HEAD_DIM ** -0.5 * LOG2E

    col = lax.broadcasted_iota(jnp.int32, (Q_BLOCK, 2 * Q_BLOCK), 1)
    for br in range(len(BRANCHES)):
        for h in range(2):
            base = jnp.broadcast_to(r0_ref[br, pl.ds(2 * hp + h, 1), :], (Q_BLOCK, 2 * Q_BLOCK))
            t = pltpu.roll(base, 0, 1, stride=1, stride_axis=0)
            tab_ref[br, 0, h] = t
            tab_ref[br, 1, h] = jnp.where(col >= Q_BLOCK, t, neg)
    zeros = jnp.zeros((Q_BLOCK, LANES), BF16)
    kd_ref[0:Q_BLOCK, :] = zeros
    v0_ref[0:Q_BLOCK, :] = zeros
    v1_ref[0:Q_BLOCK, :] = zeros

    def strided(start, n, d):
        return pl.ds(start, n) if d == 1 else pl.ds(start, n, stride=d)

    for br, (w, d) in enumerate(BRANCHES):
        sub_len = seq_len // d
        nb = sub_len // Q_BLOCK

        def prep(c, carry, d=d, sub_len=sub_len):
            de0 = c * PREP_ROWS
            dst = pl.ds(pl.multiple_of(de0, PREP_ROWS), PREP_ROWS)
            dstp = pl.ds(pl.multiple_of(de0 + Q_BLOCK, Q_BLOCK), PREP_ROWS)
            if d <= 4:
                src = strided(_div_pow2(de0, sub_len) + _mod_pow2(de0, sub_len) * d, PREP_ROWS, d)
                qq, kk, vv = q_ref[0, src, :], k_ref[0, src, :], v_ref[0, src, :]
            else:
                res, m0 = _div_pow2(de0, sub_len), _mod_pow2(de0, sub_len)
                src = pl.ds(_mod_pow2(res, 4) * quarter + m0 * (d // 4) + _div_pow2(res, 4),
                            PREP_ROWS, stride=d // 4)
                qq, kk, vv = acc_ref[2, src, :], m_ref[2, src, :], l_ref[2, src, :]
            if d == 4:
                acc_ref[2, dst, :] = qq
                m_ref[2, dst, :] = kk
                l_ref[2, dst, :] = vv
            q0_ref[dst, :] = (qq * (lo_f * q_scale)).astype(BF16)
            q1_ref[dst, :] = (qq * (hi_f * q_scale)).astype(BF16)
            kd_ref[dstp, :] = kk.astype(BF16)
            v0_ref[dstp, :] = (vv * lo_f + hi_f).astype(BF16)
            v1_ref[dstp, :] = (vv * hi_f + lo_f).astype(BF16)
            return carry

        lax.fori_loop(0, seq_len // PREP_ROWS, prep, 0)

        def score(i, br=br, nb=nb):
            for u in range(ATTN_GROUP):
                g = i * ATTN_GROUP + u
                rq = pl.ds(pl.multiple_of(g * Q_BLOCK, Q_BLOCK), Q_BLOCK)
                rk = pl.ds(pl.multiple_of(g * Q_BLOCK, Q_BLOCK), 2 * Q_BLOCK)
                first = jnp.where(_mod_pow2(g, nb) == 0, 1, 0)
                kb = kd_ref[rk, :]
                for h, qh_ref in enumerate((q0_ref, q1_ref)):
                    s = lax.dot_general(qh_ref[rq, :], kb, contract_last,
                                        preferred_element_type=F32)
                    s = s + tab_ref[br, first, h]
                    s_ref[u, h] = s
                    mx_ref[u, h] = jnp.broadcast_to(jnp.max(s, axis=-1, keepdims=True),
                                                    (Q_BLOCK, LANES))

        def finish(i, br=br, d=d, nb=nb):
            for u in range(ATTN_GROUP):
                g = i * ATTN_GROUP + u
                rk = pl.ds(pl.multiple_of(g * Q_BLOCK, Q_BLOCK), 2 * Q_BLOCK)
                outs = []
                for h, vh_ref in enumerate((v0_ref, v1_ref)):
                    m = mx_ref[u, h]
                    p = jnp.concatenate([jnp.exp2(s_ref[u, h, :, :LANES] - m),
                                         jnp.exp2(s_ref[u, h, :, LANES:] - m)], axis=1)
                    outs.append((jnp.dot(p.astype(BF16), vh_ref[rk, :],
                                         preferred_element_type=F32), m))
                (o0, m0), (o1, m1) = outs
                if d <= 4:
                    idx = pl.ds(pl.multiple_of(g * Q_BLOCK, Q_BLOCK), Q_BLOCK)
                else:
                    res, blk = _div_pow2(g, nb), _mod_pow2(g, nb)
                    idx = pl.ds(_mod_pow2(res, 4) * quarter + blk * (Q_BLOCK * d // 4)
                                + _div_pow2(res, 4), Q_BLOCK, stride=d // 4)
                acc_ref[br, idx, :] = jnp.where(low, o0, o1)
                l_ref[br, idx, :] = pltpu.roll(jnp.where(low, o1, o0), HEAD_DIM, 1)
                m_ref[br, idx, :] = jnp.where(low, m0, m1)

        def step(i, carry):
            finish(i - 1)
            score(i)
            return carry

        n_groups = seq_len // Q_BLOCK // ATTN_GROUP
        score(0)
        lax.fori_loop(1, n_groups, step, 0)
        finish(n_groups - 1)

    def merge(c, carry):
        de0 = c * PREP_ROWS
        dil = pl.ds(pl.multiple_of(de0, PREP_ROWS), PREP_ROWS)
        nat = pl.ds(_div_pow2(de0, quarter) + _mod_pow2(de0, quarter) * 4, PREP_ROWS, stride=4)
        ms = (m_ref[0, nat, :], m_ref[1, dil, :], m_ref[2, dil, :])
        m_all = jnp.maximum(jnp.maximum(ms[0], ms[1]), ms[2])
        ws = [jnp.exp2(m - m_all) for m in ms]
        num = (acc_ref[0, nat, :] * ws[0] + acc_ref[1, dil, :] * ws[1] + acc_ref[2, dil, :] * ws[2])
        den = l_ref[0, nat, :] * ws[0] + l_ref[1, dil, :] * ws[1] + l_ref[2, dil, :] * ws[2]
        acc_ref[0, nat, :] = num / den
        return carry

    lax.fori_loop(0, seq_len // PREP_ROWS, merge, 0)

    rows_per = 512

    def fin(i, c):
        sl = pl.ds(pl.multiple_of(i * rows_per, rows_per), rows_per)
        o_ref[0, sl, :] = acc_ref[0, sl, :].astype(o_ref.dtype)
        return c

    lax.fori_loop(0, seq_len // rows_per, fin, 0)


def _attn(q, k, v, r0):
    b, s, _ = q.shape
    n_br = len(BRANCHES)
    blk = pl.BlockSpec((1, s, LANES), lambda i, j: (i, 0, j))
    state = pltpu.VMEM((n_br, s, LANES), F32)
    qd = pltpu.VMEM((s, LANES), BF16)
    kd = pltpu.VMEM((s + Q_BLOCK, LANES), BF16)
    return pl.pallas_call(
        functools.partial(_attn_kernel, seq_len=s),
        grid=(b, D_ATT // LANES),
        in_specs=[blk, blk, blk, _const_spec(r0.shape)],
        out_specs=blk,
        out_shape=jax.ShapeDtypeStruct((b, s, D_ATT), BF16),
        scratch_shapes=[pltpu.VMEM((n_br, 2, 2, Q_BLOCK, 2 * Q_BLOCK), F32),
                        state, state, state, qd, qd, kd, kd, kd,
                        pltpu.VMEM((ATTN_GROUP, 2, Q_BLOCK, 2 * Q_BLOCK), F32),
                        pltpu.VMEM((ATTN_GROUP, 2, Q_BLOCK, LANES), F32)],
        compiler_params=pltpu.CompilerParams(dimension_semantics=("arbitrary", "arbitrary"),
                                             vmem_limit_bytes=VMEM_LIMIT),
        name="attn",
    )(q, k, v, r0)


def _sample_side_steps(b, q_ref, ktn_ref, vtn_ref, logc_ref, kin_ref, vin_ref, kout_ref, vout_ref,
                       att_ref, *, wb, t_new):
    rows = t_new * N_HEADS
    contract_last = (((1,), (1,)), ((), ()))
    lane = lax.broadcasted_iota(jnp.int32, (1, LANES), 1)
    keep = lane < LANES - t_new
    n_tiles = wb // LANES
    st = {}

    def scores():
        shift = lax.bitwise_and(LANES - t_new * b, LANES - 1)
        st["ktn"] = pltpu.roll(ktn_ref[0], shift, 1)
        st["vtn"] = pltpu.roll(vtn_ref[0], shift, 1)
        q_tile = q_ref[0] * (HEAD_DIM ** -0.5)
        per_tile = 8 // t_new
        q = q_tile[0:t_new]
        for i in range(1, per_tile):
            q = jnp.where(lax.rem(b, per_tile) == i, q_tile[i * t_new:(i + 1) * t_new], q)
        qrep = jnp.broadcast_to(q[:, None, :], (t_new, N_HEADS, D_ATT)).reshape(rows, D_ATT)
        row_h = lax.broadcasted_iota(jnp.int32, (rows, D_ATT), 0) % N_HEADS
        col_h = lax.broadcasted_iota(jnp.int32, (rows, D_ATT), 1) // HEAD_DIM
        st["own"] = row_h == col_h
        qbd = jnp.where(st["own"], qrep, 0.0).astype(BF16)
        st["s"] = (jnp.dot(qbd, kin_ref[...].astype(BF16), preferred_element_type=F32)
                   + logc_ref[:, :wb])
        st["sn"] = (jnp.dot(qbd, st["ktn"].astype(BF16), preferred_element_type=F32)
                    + logc_ref[:, wb:])

    def softmax():
        s, sn = st["s"], st["sn"]
        m = jnp.maximum(jnp.max(s, axis=-1, keepdims=True), jnp.max(sn, axis=-1, keepdims=True))
        p = jnp.exp(s - m)
        pn = jnp.exp(sn - m)
        st["l"] = jnp.sum(p, axis=-1, keepdims=True) + jnp.sum(pn, axis=-1, keepdims=True)
        st["p"], st["pn"] = p.astype(BF16), pn.astype(BF16)

    def values():
        o = (lax.dot_general(st["p"], vin_ref[...].astype(BF16), contract_last,
                             preferred_element_type=F32)
             + lax.dot_general(st["pn"], st["vtn"].astype(BF16), contract_last,
                               preferred_element_type=F32))
        o = jnp.where(st["own"], o, 0.0) / st["l"]
        att_ref[0] = jnp.sum(o.reshape(t_new, N_HEADS, D_ATT), axis=1)

    def shift_tiles(src_ref, new_key, dst_ref, j0, j1):
        def run():
            cur = st.get(("cur", new_key))
            if cur is None:
                cur = pltpu.roll(src_ref[:, 0:LANES], LANES - t_new, 1)
            for j in range(j0, j1):
                if j + 1 < n_tiles:
                    nxt = pltpu.roll(src_ref[:, (j + 1) * LANES:(j + 2) * LANES], LANES - t_new, 1)
                else:
                    nxt = pltpu.roll(st[new_key], LANES - t_new, 1)
                dst_ref[:, j * LANES:(j + 1) * LANES] = jnp.where(keep, cur, nxt)
                cur = nxt
            st[("cur", new_key)] = cur
        return run

    per = SIDE_TILES_PER_STEP
    k_steps = [scores] + [shift_tiles(kin_ref, "ktn", kout_ref, j0, min(j0 + per, n_tiles))
                          for j0 in range(0, n_tiles, per)]
    v_steps = [softmax, values] + [shift_tiles(vin_ref, "vtn", vout_ref, j0, min(j0 + per, n_tiles))
                                   for j0 in range(0, n_tiles, per)]
    return k_steps, v_steps


def _ffn_body(x_ref, att_ref, sg_ref, woa_ref, wos_ref, g2_ref, wg_ref, wu_ref, wd_ref, gf_ref,
              y_ref, a_ref, side_steps=()):
    side = list(side_steps)
    assert len(side) < N_FF_CHUNKS
    x1 = (x_ref[...]
          + jnp.dot(att_ref[...].astype(BF16), woa_ref[...], preferred_element_type=F32)
          + jnp.dot(sg_ref[...], wos_ref[...], preferred_element_type=F32))
    ms = jnp.mean(x1 * x1, axis=-1, keepdims=True)
    h = (x1 * lax.rsqrt(ms + EPS) * g2_ref[...]).astype(BF16)
    for c in range(N_FF_CHUNKS):
        cols = slice(c * FF_CHUNK, (c + 1) * FF_CHUNK)
        g = jnp.dot(h, wg_ref[:, cols], preferred_element_type=F32)
        u = jnp.dot(h, wu_ref[:, cols], preferred_element_type=F32)
        a_ref[:, cols] = (g * jax.nn.sigmoid(g) * u).astype(BF16)
        if side:
            side.pop(0)()
    x2 = x1 + jnp.dot(a_ref[...], wd_ref[...], preferred_element_type=F32)
    ms2 = jnp.mean(x2 * x2, axis=-1, keepdims=True)
    y_ref[...] = x2 * lax.rsqrt(ms2 + EPS) * gf_ref[...]


def _ffn_kernel(*refs):
    _ffn_body(*refs)


def _ffn_side_kernel(x_ref, att_ref, sg_ref, woa_ref, wos_ref, g2_ref, wg_ref, wu_ref, wd_ref, gf_ref,
                     q_ref, ktn_ref, vtn_ref, logc_ref, kt_hbm, vt_hbm,
                     y_ref, atts_ref, kto_hbm, vto_hbm,
                     a_ref, kin_ref, vin_ref, kout_ref, vout_ref, sems, *, wb, t_new):
    i = pl.program_id(0)
    last = pl.num_programs(0) - 1

    def fetch(j):
        return (pltpu.make_async_copy(kt_hbm.at[j], kin_ref, sems.at[0]),
                pltpu.make_async_copy(vt_hbm.at[j], vin_ref, sems.at[1]))

    def flush(j):
        return (pltpu.make_async_copy(kout_ref, kto_hbm.at[j], sems.at[2]),
                pltpu.make_async_copy(vout_ref, vto_hbm.at[j], sems.at[3]))

    @pl.when(i == 0)
    def _():
        for cp in fetch(0):
            cp.start()

    for cp in fetch(i):
        cp.wait()

    @pl.when(i > 0)
    def _():
        for cp in flush(i - 1):
            cp.wait()

    def then_swap(step, which):
        def run():
            step()
            flush(i)[which].start()
            fetch(jnp.minimum(i + 1, last))[which].start()
        return run

    k_steps, v_steps = _sample_side_steps(i, q_ref, ktn_ref, vtn_ref, logc_ref, kin_ref, vin_ref,
                                          kout_ref, vout_ref, atts_ref, wb=wb, t_new=t_new)
    k_steps[-1] = then_swap(k_steps[-1], 0)
    v_steps[-1] = then_swap(v_steps[-1], 1)
    _ffn_body(x_ref, att_ref, sg_ref, woa_ref, wos_ref, g2_ref, wg_ref, wu_ref, wd_ref, gf_ref,
              y_ref, a_ref, side_steps=k_steps + v_steps)

    @pl.when(i == last)
    def _():
        for cp in fetch(i) + flush(i):
            cp.wait()


def _ffn_specs(tm, weights):
    row = lambda shape: pl.BlockSpec(shape, lambda i: (i, 0))
    return ([row((tm, D_MODEL)), row((tm, D_ATT)), row((tm, D_SGU))]
            + [_const_spec(w.shape) for w in weights]), row((tm, D_MODEL))


def _ffn(x, att, sg, weights, *, tm):
    t = x.shape[0]
    in_specs, out_spec = _ffn_specs(tm, weights)
    return pl.pallas_call(
        _ffn_kernel,
        grid=(t // tm,),
        in_specs=in_specs,
        out_specs=out_spec,
        out_shape=jax.ShapeDtypeStruct((t, D_MODEL), F32),
        scratch_shapes=[pltpu.VMEM((tm, D_FF), BF16)],
        compiler_params=pltpu.CompilerParams(dimension_semantics=("arbitrary",),
                                             vmem_limit_bytes=VMEM_LIMIT),
        name="ffn",
    )(x, att, sg, *weights)


def _ffn_with_sample_side(x, att, sg, weights, q_s, ktn, vtn, logc, kt, vt, *, tm, t_new):
    t = x.shape[0]
    bd, _, wb = kt.shape
    assert t // tm == bd
    in_specs, out_spec = _ffn_specs(tm, weights)
    any_spec = pl.BlockSpec(memory_space=pl.ANY)
    window = pltpu.VMEM((D_ATT, wb), F32)
    return pl.pallas_call(
        functools.partial(_ffn_side_kernel, wb=wb, t_new=t_new),
        grid=(bd,),
        in_specs=in_specs + [pl.BlockSpec((1, 8, D_ATT), lambda i: (i // (8 // t_new), 0, 0)),
                             _const_spec(ktn.shape), _const_spec(vtn.shape),
                             _const_spec(logc.shape), any_spec, any_spec],
        out_specs=[out_spec, pl.BlockSpec((1, t_new, D_ATT), lambda i: (i, 0, 0)),
                   any_spec, any_spec],
        out_shape=[jax.ShapeDtypeStruct((t, D_MODEL), F32),
                   jax.ShapeDtypeStruct((bd, t_new, D_ATT), F32),
                   jax.ShapeDtypeStruct(kt.shape, F32), jax.ShapeDtypeStruct(vt.shape, F32)],
        scratch_shapes=[pltpu.VMEM((tm, D_FF), BF16), window, window, window, window,
                        pltpu.SemaphoreType.DMA((4,))],
        compiler_params=pltpu.CompilerParams(dimension_semantics=("arbitrary",),
                                             vmem_limit_bytes=VMEM_LIMIT_FUSED),
        name="ffn_side",
    )(x, att, sg, *weights, q_s, ktn, vtn, logc, kt, vt)


def _rel_bucket(dist):
    max_exact = N_BUCKETS // 2
    df = jnp.maximum(dist, 1).astype(F32)
    large = max_exact + (jnp.log(df / max_exact) / math.log(MAX_DISTANCE / max_exact)
                         * (N_BUCKETS - max_exact)).astype(jnp.int32)
    large = jnp.minimum(large, N_BUCKETS - 1)
    return jnp.where(dist < max_exact, dist, large)


def _branch_bias_reversed(rel_bias, w, d):
    nj = w // d + 1
    dist = (nj - 1 - jnp.arange(nj, dtype=jnp.int32)) * d
    return rel_bias[_rel_bucket(dist)].T.astype(F32)


def _prompt_bias_rows(rev):
    rows = [jnp.pad(r, ((0, 0), (0, 2 * Q_BLOCK - r.shape[1])), constant_values=NEG_INF)
            for r in rev]
    return jnp.stack(rows) * LOG2E


def _sample_bias_table(rev, wb, t_new):
    width = wb + LANES
    per_branch = []
    for (w, d), r in zip(BRANCHES, rev):
        nj = r.shape[1]
        if d > 1:
            fill = jnp.full((N_HEADS, nj, d - 1), NEG_INF, F32)
            r = jnp.concatenate([r[:, :, None], fill], axis=2).reshape(N_HEADS, nj * d)
        rows = []
        for t in range(t_new):
            base = wb + t - (nj - 1) * d
            rows.append(jnp.pad(r, ((0, 0), (base, width - base - nj * d)),
                                constant_values=NEG_INF))
        per_branch.append(jnp.stack(rows))
    x = jnp.stack(per_branch)
    mx = jnp.max(x, axis=0)
    logc = mx + jnp.log(jnp.sum(jnp.exp(x - mx), axis=0))
    return logc.reshape(t_new * N_HEADS, width)


def kernel(x_prompt, x_sample, cache_k_win, cache_v_win, norm1_g, w_in, sgu_ln_g, sgu_ln_b,
           sgu_w, sgu_b, w_out, norm2_g, w_gate, w_up, w_down, rel_bias, final_g):
    depth = w_in.shape[0]
    assert depth == 1
    b, s, _ = x_prompt.shape
    bd, t_new, _ = x_sample.shape
    wb = cache_k_win.shape[2]
    assert bd * t_new == CHUNK and s % (Q_BLOCK * 16) == 0 and wb == WINDOW
    assert (s // Q_BLOCK) % ATTN_GROUP == 0 and s % PREP_ROWS == 0

    l = 0
    w_in_b = w_in[l].astype(BF16)
    woa = w_out[l, :D_ATT].astype(BF16)
    wos = w_out[l, D_ATT:].astype(BF16)
    wg = w_gate[l].astype(BF16)
    wu = w_up[l].astype(BF16)
    wd = w_down[l].astype(BF16)
    g1 = norm1_g[l][None]
    g2 = norm2_g[l][None]
    gf = final_g[None]
    ln_g = sgu_ln_g[l][None]
    ln_b = sgu_ln_b[l][None]

    causal = jnp.tril(jnp.ones((CHUNK, CHUNK), F32))
    wm = sgu_w[l] * causal
    mix_p = wm.astype(BF16)
    mixb_p = jnp.repeat(sgu_b[l].T, HEAD_DIM, axis=1)
    eye = jnp.eye(bd, dtype=F32)
    mix_s = jnp.einsum('ab,gts->gatbs', eye, wm[:, :t_new, :t_new]).reshape(
        N_GROUPS, CHUNK, CHUNK).astype(BF16)
    mixb_s = jnp.tile(mixb_p[:t_new], (bd, 1))

    rev = [_branch_bias_reversed(rel_bias, w, d) for w, d in BRANCHES]

    xp = x_prompt.reshape(b * s, D_MODEL)
    q, k, v, sg, kt_p, vt_p = _proj(xp, g1, w_in_b, ln_g, ln_b, mix_p, mixb_p,
                                    tm=512, emit_vn=False, emit_t=True, seq_len=s)
    att = _attn(q.reshape(b, s, D_ATT), k.reshape(b, s, D_ATT), v.reshape(b, s, D_ATT),
                _prompt_bias_rows(rev))
    nw = min(WINDOW, s)
    new_k_p = kt_p.reshape(1, b, N_HEADS, HEAD_DIM, nw).transpose(0, 1, 4, 2, 3)
    new_v_p = vt_p.reshape(1, b, N_HEADS, HEAD_DIM, nw).transpose(0, 1, 4, 2, 3)

    xs = x_sample.reshape(bd * t_new, D_MODEL)
    qs, _, _, sgs, vn_s, kt_n, vt_n = _proj(xs, g1, w_in_b, ln_g, ln_b, mix_s, mixb_s,
                                            tm=CHUNK, emit_vn=True, emit_t=True, seq_len=CHUNK)
    kt_c = cache_k_win[l].transpose(0, 2, 3, 1).reshape(bd, D_ATT, wb)
    vt_c = cache_v_win[l].transpose(0, 2, 3, 1).reshape(bd, D_ATT, wb)

    logc = _sample_bias_table(rev, wb, t_new)
    weights = (woa, wos, g2, wg, wu, wd, gf)
    y_prompt, att_s, kt_o, vt_o = _ffn_with_sample_side(
        xp, att.reshape(b * s, D_ATT), sg, weights, qs.reshape(bd * t_new // 8, 8, D_ATT),
        kt_n, vt_n, logc, kt_c, vt_c, tm=(b * s) // bd, t_new=t_new)
    y_prompt = y_prompt.reshape(b, s, D_MODEL)
    y_sample = _ffn(xs, att_s.reshape(bd * t_new, D_ATT), sgs, weights,
                    tm=CHUNK).reshape(bd, t_new, D_MODEL)
    new_k_s = kt_o.reshape(1, bd, N_HEADS, HEAD_DIM, wb).transpose(0, 1, 4, 2, 3)
    new_v_s = vt_o.reshape(1, bd, N_HEADS, HEAD_DIM, wb).transpose(0, 1, 4, 2, 3)
    sgu_v = vn_s.reshape(1, bd, t_new, D_SGU)

    return (y_prompt, y_sample, new_k_p, new_v_p, new_k_s, new_v_s, sgu_v)
```

```python
import functools
import math

import jax
import jax.numpy as jnp
from jax import lax
from jax.experimental import pallas as pl
from jax.experimental.pallas import tpu as pltpu

D_MODEL = 1024
N_HEADS = 8
HEAD_DIM = 64
D_ATT = N_HEADS * HEAD_DIM
N_GROUPS = 8
D_SGU = 512
CHUNK = 128
BRANCHES = ((128, 1), (512, 4), (2048, 16))
WINDOW = 2048
Q_BLOCK = 128
N_BUCKETS = 32
MAX_DISTANCE = WINDOW
D_FF = 2816
EPS = 1e-6
NEG_INF = -1e30
LOG2E = math.log2(math.e)

LANES = 128
FF_CHUNK = 256
N_FF_CHUNKS = D_FF // FF_CHUNK
VMEM_LIMIT = 48 * 1024 * 1024
VMEM_LIMIT_FUSED = 60 * 1024 * 1024
SIDE_TILES_PER_STEP = 8
ATTN_GROUP = 8
PREP_ROWS = 256
CAST_STEPS = 16

F32 = jnp.float32
BF16 = jnp.bfloat16


def _const_spec(shape):
    nd = len(shape)
    return pl.BlockSpec(shape, lambda *_: (0,) * nd, pipeline_mode=pl.Buffered(1))


def _proj_kernel(x_ref, g1_ref, w_ref, lng_ref, lnb_ref, mix_ref, mixb_ref, *refs,
                 tm, emit_vn, emit_t, n_cast):
    cast_in, outs = refs[:n_cast], refs[n_cast:]
    cast_out = outs[len(outs) - n_cast:]
    q_ref, k_ref, v_ref, sg_ref = outs[:4]
    rest = outs[4:len(outs) - n_cast]
    lane = lax.broadcasted_iota(jnp.int32, (1, LANES), 1)
    low = lane < HEAD_DIM

    x = x_ref[...]
    xg = (x * g1_ref[...]).astype(BF16)
    r = lax.rsqrt(jnp.mean(x * x, axis=-1, keepdims=True) + EPS)

    def proj(c0):
        return jnp.dot(xg, w_ref[:, c0:c0 + D_ATT], preferred_element_type=F32) * r

    vg = proj(3 * D_ATT + D_SGU)
    u = proj(3 * D_ATT)
    mu = jnp.mean(vg, axis=-1, keepdims=True)
    dv = vg - mu
    var = jnp.mean(dv * dv, axis=-1, keepdims=True)
    vn = dv * lax.rsqrt(var + EPS) * lng_ref[...] + lnb_ref[...]
    if emit_vn:
        rest[0][...] = vn
    q_ref[...] = proj(0)

    def gating(c0):
        for s in range(D_SGU // LANES):
            cols = slice(s * LANES, (s + 1) * LANES)
            slab = vn[c0:c0 + CHUNK, cols]
            lo = jnp.where(low, slab, 0.0).astype(BF16)
            hi = jnp.where(low, 0.0, slab).astype(BF16)
            gate = (jnp.dot(mix_ref[2 * s], lo, preferred_element_type=F32)
                    + jnp.dot(mix_ref[2 * s + 1], hi, preferred_element_type=F32)
                    + mixb_ref[:, cols])
            sg_ref[c0:c0 + CHUNK, cols] = (u[c0:c0 + CHUNK, cols] * gate).astype(BF16)

    chunks = list(range(0, tm, CHUNK))
    for c0 in chunks[:len(chunks) // 2]:
        gating(c0)
    k = proj(D_ATT)
    k_ref[...] = k
    for src, dst in zip(cast_in, cast_out):
        dst[...] = src[...].astype(BF16)
    for c0 in chunks[len(chunks) // 2:]:
        gating(c0)
    v = proj(2 * D_ATT)
    v_ref[...] = v
    if emit_t:
        kt_ref, vt_ref = rest[-2:]
        kt_ref[0] = k.T
        vt_ref[0] = v.T


def _proj(x, g1, w_in, ln_g, ln_b, mix, mixb, *, tm, emit_vn, emit_t, seq_len=None, cast=()):
    t = x.shape[0]
    n_tiles = t // tm
    row = lambda shape: pl.BlockSpec(shape, lambda i: (i, 0))
    out_shape = [jax.ShapeDtypeStruct((t, D_ATT), F32)] * 3 + [jax.ShapeDtypeStruct((t, D_SGU), BF16)]
    out_specs = [row((tm, D_ATT))] * 3 + [row((tm, D_SGU))]
    if emit_vn:
        out_shape.append(jax.ShapeDtypeStruct((t, D_SGU), F32))
        out_specs.append(row((tm, D_SGU)))
    if emit_t:
        tiles_per_seq = seq_len // tm
        win = min(WINDOW, seq_len)
        first = tiles_per_seq - win // tm

        def t_map(i):
            return (i // tiles_per_seq, 0, jnp.maximum(i % tiles_per_seq - first, 0))

        for _ in range(2):
            out_shape.append(jax.ShapeDtypeStruct((t // seq_len, D_ATT, win), F32))
            out_specs.append(pl.BlockSpec((1, D_ATT, tm), t_map))
    cast_specs = []
    for w in cast:
        assert n_tiles >= CAST_STEPS and w.shape[0] % (CAST_STEPS * 16) == 0
        spec = pl.BlockSpec((w.shape[0] // CAST_STEPS, w.shape[1]),
                            lambda i: (jnp.minimum(i, CAST_STEPS - 1), 0))
        cast_specs.append(spec)
        out_shape.append(jax.ShapeDtypeStruct(w.shape, BF16))
        out_specs.append(spec)
    kern = functools.partial(_proj_kernel, tm=tm, emit_vn=emit_vn, emit_t=emit_t, n_cast=len(cast))
    return pl.pallas_call(
        kern,
        grid=(n_tiles,),
        in_specs=[row((tm, D_MODEL)), _const_spec(g1.shape), _const_spec(w_in.shape),
                  _const_spec(ln_g.shape), _const_spec(ln_b.shape), _const_spec(mix.shape),
                  _const_spec(mixb.shape)] + cast_specs,
        out_specs=out_specs,
        out_shape=out_shape,
        compiler_params=pltpu.CompilerParams(dimension_semantics=("arbitrary",),
                                             vmem_limit_bytes=VMEM_LIMIT),
        name="proj_cast" if cast else "proj",
    )(x, g1, w_in, ln_g, ln_b, mix, mixb, *cast)


def _div_pow2(x, n):
    assert n & (n - 1) == 0
    return lax.shift_right_logical(x, n.bit_length() - 1)


def _mod_pow2(x, n):
    assert n & (n - 1) == 0
    return lax.bitwise_and(x, n - 1)


def _attn_kernel(q_ref, k_ref, v_ref, r0_ref, o_ref, tab_ref, acc_ref, m_ref, l_ref,
                 q0_ref, q1_ref, kd_ref, v0_ref, v1_ref, s_ref, mx_ref, *, seq_len):
    hp = pl.program_id(1)
    lane = lax.broadcasted_iota(jnp.int32, (1, LANES), 1)
    low = lane < HEAD_DIM
    contract_last = (((1,), (1,)), ((), ()))
    neg = NEG_INF * LOG2E
    quarter = seq_len // 4
    lo_f = jnp.where(low, 1.0, 0.0).astype(F32)
    hi_f = 1.0 - lo_f
    q_scale = HEAD_DIM ** -0.5 * LOG2E

    col = lax.broadcasted_iota(jnp.int32, (Q_BLOCK, 2 * Q_BLOCK), 1)
    for br in range(len(BRANCHES)):
        for h in range(2):
            base = jnp.broadcast_to(r0_ref[br, pl.ds(2 * hp + h, 1), :], (Q_BLOCK, 2 * Q_BLOCK))
            t = pltpu.roll(base, 0, 1, stride=1, stride_axis=0)
            tab_ref[br, 0, h] = t
            tab_ref[br, 1, h] = jnp.where(col >= Q_BLOCK, t, neg)
    zeros = jnp.zeros((Q_BLOCK, LANES), BF16)
    kd_ref[0:Q_BLOCK, :] = zeros
    v0_ref[0:Q_BLOCK, :] = zeros
    v1_ref[0:Q_BLOCK, :] = zeros

    def strided(start, n, d):
        return pl.ds(start, n) if d == 1 else pl.ds(start, n, stride=d)

    for br, (w, d) in enumerate(BRANCHES):
        sub_len = seq_len // d
        nb = sub_len // Q_BLOCK

        def prep(c, carry, d=d, sub_len=sub_len):
            de0 = c * PREP_ROWS
            dst = pl.ds(pl.multiple_of(de0, PREP_ROWS), PREP_ROWS)
            dstp = pl.ds(pl.multiple_of(de0 + Q_BLOCK, Q_BLOCK), PREP_ROWS)
            if d <= 4:
                src = strided(_div_pow2(de0, sub_len) + _mod_pow2(de0, sub_len) * d, PREP_ROWS, d)
                qq, kk, vv = q_ref[0, src, :], k_ref[0, src, :], v_ref[0, src, :]
            else:
                res, m0 = _div_pow2(de0, sub_len), _mod_pow2(de0, sub_len)
                src = pl.ds(_mod_pow2(res, 4) * quarter + m0 * (d // 4) + _div_pow2(res, 4),
                            PREP_ROWS, stride=d // 4)
                qq, kk, vv = acc_ref[2, src, :], m_ref[2, src, :], l_ref[2, src, :]
            if d == 4:
                acc_ref[2, dst, :] = qq
                m_ref[2, dst, :] = kk
                l_ref[2, dst, :] = vv
            q0_ref[dst, :] = (qq * (lo_f * q_scale)).astype(BF16)
            q1_ref[dst, :] = (qq * (hi_f * q_scale)).astype(BF16)
            kd_ref[dstp, :] = kk.astype(BF16)
            v0_ref[dstp, :] = (vv * lo_f + hi_f).astype(BF16)
            v1_ref[dstp, :] = (vv * hi_f + lo_f).astype(BF16)
            return carry

        lax.fori_loop(0, seq_len // PREP_ROWS, prep, 0)

        def score(i, br=br, nb=nb):
            for u in range(ATTN_GROUP):
                g = i * ATTN_GROUP + u
                rq = pl.ds(pl.multiple_of(g * Q_BLOCK, Q_BLOCK), Q_BLOCK)
                rk = pl.ds(pl.multiple_of(g * Q_BLOCK, Q_BLOCK), 2 * Q_BLOCK)
                first = jnp.where(_mod_pow2(g, nb) == 0, 1, 0)
                kb = kd_ref[rk, :]
                for h, qh_ref in enumerate((q0_ref, q1_ref)):
                    s = lax.dot_general(qh_ref[rq, :], kb, contract_last,
                                        preferred_element_type=F32)
                    s = s + tab_ref[br, first, h]
                    s_ref[u, h] = s
                    mx_ref[u, h] = jnp.broadcast_to(jnp.max(s, axis=-1, keepdims=True),
                                                    (Q_BLOCK, LANES))

        def finish(i, br=br, d=d, nb=nb):
            for u in range(ATTN_GROUP):
                g = i * ATTN_GROUP + u
                rk = pl.ds(pl.multiple_of(g * Q_BLOCK, Q_BLOCK), 2 * Q_BLOCK)
                outs = []
                for h, vh_ref in enumerate((v0_ref, v1_ref)):
                    m = mx_ref[u, h]
                    p = jnp.concatenate([jnp.exp2(s_ref[u, h, :, :LANES] - m),
                                         jnp.exp2(s_ref[u, h, :, LANES:] - m)], axis=1)
                    outs.append((jnp.dot(p.astype(BF16), vh_ref[rk, :],
                                         preferred_element_type=F32), m))
                (o0, m0), (o1, m1) = outs
                if d <= 4:
                    idx = pl.ds(pl.multiple_of(g * Q_BLOCK, Q_BLOCK), Q_BLOCK)
                else:
                    res, blk = _div_pow2(g, nb), _mod_pow2(g, nb)
                    idx = pl.ds(_mod_pow2(res, 4) * quarter + blk * (Q_BLOCK * d // 4)
                                + _div_pow2(res, 4), Q_BLOCK, stride=d // 4)
                acc_ref[br, idx, :] = jnp.where(low, o0, o1)
                l_ref[br, idx, :] = pltpu.roll(jnp.where(low, o1, o0), HEAD_DIM, 1)
                m_ref[br, idx, :] = jnp.where(low, m0, m1)

        def step(i, carry):
            finish(i - 1)
            score(i)
            return carry

        n_groups = seq_len // Q_BLOCK // ATTN_GROUP
        score(0)
        lax.fori_loop(1, n_groups, step, 0)
        finish(n_groups - 1)

    def merge(c, carry):
        de0 = c * PREP_ROWS
        dil = pl.ds(pl.multiple_of(de0, PREP_ROWS), PREP_ROWS)
        nat = pl.ds(_div_pow2(de0, quarter) + _mod_pow2(de0, quarter) * 4, PREP_ROWS, stride=4)
        ms = (m_ref[0, nat, :], m_ref[1, dil, :], m_ref[2, dil, :])
        m_all = jnp.maximum(jnp.maximum(ms[0], ms[1]), ms[2])
        ws = [jnp.exp2(m - m_all) for m in ms]
        num = (acc_ref[0, nat, :] * ws[0] + acc_ref[1, dil, :] * ws[1] + acc_ref[2, dil, :] * ws[2])
        den = l_ref[0, nat, :] * ws[0] + l_ref[1, dil, :] * ws[1] + l_ref[2, dil, :] * ws[2]
        acc_ref[0, nat, :] = num / den
        return carry

    lax.fori_loop(0, seq_len // PREP_ROWS, merge, 0)

    rows_per = 512

    def fin(i, c):
        sl = pl.ds(pl.multiple_of(i * rows_per, rows_per), rows_per)
        o_ref[0, sl, :] = acc_ref[0, sl, :].astype(o_ref.dtype)
        return c

    lax.fori_loop(0, seq_len // rows_per, fin, 0)


def _attn(q, k, v, r0):
    b, s, _ = q.shape
    n_br = len(BRANCHES)
    blk = pl.BlockSpec((1, s, LANES), lambda i, j: (i, 0, j))
    state = pltpu.VMEM((n_br, s, LANES), F32)
    qd = pltpu.VMEM((s, LANES), BF16)
    kd = pltpu.VMEM((s + Q_BLOCK, LANES), BF16)
    return pl.pallas_call(
        functools.partial(_attn_kernel, seq_len=s),
        grid=(b, D_ATT // LANES),
        in_specs=[blk, blk, blk, _const_spec(r0.shape)],
        out_specs=blk,
        out_shape=jax.ShapeDtypeStruct((b, s, D_ATT), BF16),
        scratch_shapes=[pltpu.VMEM((n_br, 2, 2, Q_BLOCK, 2 * Q_BLOCK), F32),
                        state, state, state, qd, qd, kd, kd, kd,
                        pltpu.VMEM((ATTN_GROUP, 2, Q_BLOCK, 2 * Q_BLOCK), F32),
                        pltpu.VMEM((ATTN_GROUP, 2, Q_BLOCK, LANES), F32)],
        compiler_params=pltpu.CompilerParams(dimension_semantics=("arbitrary", "arbitrary"),
                                             vmem_limit_bytes=VMEM_LIMIT),
        name="attn",
    )(q, k, v, r0)


def _sample_side_steps(b, q_ref, ktn_ref, vtn_ref, logc_ref, kin_ref, vin_ref, kout_ref, vout_ref,
                       att_ref, *, wb, t_new):
    rows = t_new * N_HEADS
    contract_last = (((1,), (1,)), ((), ()))
    lane = lax.broadcasted_iota(jnp.int32, (1, LANES), 1)
    keep = lane < LANES - t_new
    n_tiles = wb // LANES
    st = {}

    def scores():
        shift = lax.bitwise_and(LANES - t_new * b, LANES - 1)
        st["ktn"] = pltpu.roll(ktn_ref[0], shift, 1)
        st["vtn"] = pltpu.roll(vtn_ref[0], shift, 1)
        q_tile = q_ref[0] * (HEAD_DIM ** -0.5)
        per_tile = 8 // t_new
        q = q_tile[0:t_new]
        for i in range(1, per_tile):
            q = jnp.where(lax.rem(b, per_tile) == i, q_tile[i * t_new:(i + 1) * t_new], q)
        qrep = jnp.broadcast_to(q[:, None, :], (t_new, N_HEADS, D_ATT)).reshape(rows, D_ATT)
        row_h = lax.broadcasted_iota(jnp.int32, (rows, D_ATT), 0) % N_HEADS
        col_h = lax.broadcasted_iota(jnp.int32, (rows, D_ATT), 1) // HEAD_DIM
        st["own"] = row_h == col_h
        qbd = jnp.where(st["own"], qrep, 0.0).astype(BF16)
        st["s"] = (jnp.dot(qbd, kin_ref[...].astype(BF16), preferred_element_type=F32)
                   + logc_ref[:, :wb])
        st["sn"] = (jnp.dot(qbd, st["ktn"].astype(BF16), preferred_element_type=F32)
                    + logc_ref[:, wb:])

    def softmax():
        s, sn = st["s"], st["sn"]
        m = jnp.maximum(jnp.max(s, axis=-1, keepdims=True), jnp.max(sn, axis=-1, keepdims=True))
        p = jnp.exp(s - m)
        pn = jnp.exp(sn - m)
        st["l"] = jnp.sum(p, axis=-1, keepdims=True) + jnp.sum(pn, axis=-1, keepdims=True)
        st["p"], st["pn"] = p.astype(BF16), pn.astype(BF16)

    def values():
        o = (lax.dot_general(st["p"], vin_ref[...].astype(BF16), contract_last,
                             preferred_element_type=F32)
             + lax.dot_general(st["pn"], st["vtn"].astype(BF16), contract_last,
                               preferred_element_type=F32))
        o = jnp.where(st["own"], o, 0.0) / st["l"]
        att_ref[0] = jnp.sum(o.reshape(t_new, N_HEADS, D_ATT), axis=1)

    def shift_tiles(src_ref, new_key, dst_ref, j0, j1):
        def run():
            cur = st.get(("cur", new_key))
            if cur is None:
                cur = pltpu.roll(src_ref[:, 0:LANES], LANES - t_new, 1)
            for j in range(j0, j1):
                if j + 1 < n_tiles:
                    nxt = pltpu.roll(src_ref[:, (j + 1) * LANES:(j + 2) * LANES], LANES - t_new, 1)
                else:
                    nxt = pltpu.roll(st[new_key], LANES - t_new, 1)
                dst_ref[:, j * LANES:(j + 1) * LANES] = jnp.where(keep, cur, nxt)
                cur = nxt
            st[("cur", new_key)] = cur
        return run

    per = SIDE_TILES_PER_STEP
    k_steps = [scores] + [shift_tiles(kin_ref, "ktn", kout_ref, j0, min(j0 + per, n_tiles))
                          for j0 in range(0, n_tiles, per)]
    v_steps = [softmax, values] + [shift_tiles(vin_ref, "vtn", vout_ref, j0, min(j0 + per, n_tiles))
                                   for j0 in range(0, n_tiles, per)]
    return k_steps, v_steps


def _ffn_body(x_ref, att_ref, sg_ref, wo_ref, g2_ref, wg_ref, wu_ref, wd_ref, gf_ref,
              y_ref, a_ref, side_steps=()):
    side = list(side_steps)
    assert len(side) < N_FF_CHUNKS
    x1 = (x_ref[...]
          + jnp.dot(att_ref[...].astype(BF16), wo_ref[:D_ATT, :], preferred_element_type=F32)
          + jnp.dot(sg_ref[...], wo_ref[D_ATT:, :], preferred_element_type=F32))
    ms = jnp.mean(x1 * x1, axis=-1, keepdims=True)
    h = (x1 * lax.rsqrt(ms + EPS) * g2_ref[...]).astype(BF16)
    for c in range(N_FF_CHUNKS):
        cols = slice(c * FF_CHUNK, (c + 1) * FF_CHUNK)
        g = jnp.dot(h, wg_ref[:, cols], preferred_element_type=F32)
        u = jnp.dot(h, wu_ref[:, cols], preferred_element_type=F32)
        a_ref[:, cols] = (g * jax.nn.sigmoid(g) * u).astype(BF16)
        if side:
            side.pop(0)()
    x2 = x1 + jnp.dot(a_ref[...], wd_ref[...], preferred_element_type=F32)
    ms2 = jnp.mean(x2 * x2, axis=-1, keepdims=True)
    y_ref[...] = x2 * lax.rsqrt(ms2 + EPS) * gf_ref[...]


def _ffn_kernel(*refs):
    _ffn_body(*refs)


def _ffn_side_kernel(x_ref, att_ref, sg_ref, wo_ref, g2_ref, wg_ref, wu_ref, wd_ref, gf_ref,
                     q_ref, ktn_ref, vtn_ref, logc_ref, kt_hbm, vt_hbm,
                     y_ref, atts_ref, kto_hbm, vto_hbm,
                     a_ref, kin_ref, vin_ref, kout_ref, vout_ref, sems, *, wb, t_new):
    i = pl.program_id(0)
    last = pl.num_programs(0) - 1

    def fetch(j):
        return (pltpu.make_async_copy(kt_hbm.at[j], kin_ref, sems.at[0]),
                pltpu.make_async_copy(vt_hbm.at[j], vin_ref, sems.at[1]))

    def flush(j):
        return (pltpu.make_async_copy(kout_ref, kto_hbm.at[j], sems.at[2]),
                pltpu.make_async_copy(vout_ref, vto_hbm.at[j], sems.at[3]))

    @pl.when(i == 0)
    def _():
        for cp in fetch(0):
            cp.start()

    for cp in fetch(i):
        cp.wait()

    @pl.when(i > 0)
    def _():
        for cp in flush(i - 1):
            cp.wait()

    def then_swap(step, which):
        def run():
            step()
            flush(i)[which].start()
            fetch(jnp.minimum(i + 1, last))[which].start()
        return run

    k_steps, v_steps = _sample_side_steps(i, q_ref, ktn_ref, vtn_ref, logc_ref, kin_ref, vin_ref,
                                          kout_ref, vout_ref, atts_ref, wb=wb, t_new=t_new)
    k_steps[-1] = then_swap(k_steps[-1], 0)
    v_steps[-1] = then_swap(v_steps[-1], 1)
    _ffn_body(x_ref, att_ref, sg_ref, wo_ref, g2_ref, wg_ref, wu_ref, wd_ref, gf_ref,
              y_ref, a_ref, side_steps=k_steps + v_steps)

    @pl.when(i == last)
    def _():
        for cp in fetch(i) + flush(i):
            cp.wait()


def _ffn_specs(tm, weights):
    row = lambda shape: pl.BlockSpec(shape, lambda i: (i, 0))
    return ([row((tm, D_MODEL)), row((tm, D_ATT)), row((tm, D_SGU))]
            + [_const_spec(w.shape) for w in weights]), row((tm, D_MODEL))


def _ffn(x, att, sg, weights, *, tm):
    t = x.shape[0]
    in_specs, out_spec = _ffn_specs(tm, weights)
    return pl.pallas_call(
        _ffn_kernel,
        grid=(t // tm,),
        in_specs=in_specs,
        out_specs=out_spec,
        out_shape=jax.ShapeDtypeStruct((t, D_MODEL), F32),
        scratch_shapes=[pltpu.VMEM((tm, D_FF), BF16)],
        compiler_params=pltpu.CompilerParams(dimension_semantics=("arbitrary",),
                                             vmem_limit_bytes=VMEM_LIMIT),
        name="ffn",
    )(x, att, sg, *weights)


def _ffn_with_sample_side(x, att, sg, weights, q_s, ktn, vtn, logc, kt, vt, *, tm, t_new):
    t = x.shape[0]
    bd, _, wb = kt.shape
    assert t // tm == bd
    in_specs, out_spec = _ffn_specs(tm, weights)
    any_spec = pl.BlockSpec(memory_space=pl.ANY)
    window = pltpu.VMEM((D_ATT, wb), F32)
    return pl.pallas_call(
        functools.partial(_ffn_side_kernel, wb=wb, t_new=t_new),
        grid=(bd,),
        in_specs=in_specs + [pl.BlockSpec((1, 8, D_ATT), lambda i: (i // (8 // t_new), 0, 0)),
                             _const_spec(ktn.shape), _const_spec(vtn.shape),
                             _const_spec(logc.shape), any_spec, any_spec],
        out_specs=[out_spec, pl.BlockSpec((1, t_new, D_ATT), lambda i: (i, 0, 0)),
                   any_spec, any_spec],
        out_shape=[jax.ShapeDtypeStruct((t, D_MODEL), F32),
                   jax.ShapeDtypeStruct((bd, t_new, D_ATT), F32),
                   jax.ShapeDtypeStruct(kt.shape, F32), jax.ShapeDtypeStruct(vt.shape, F32)],
        scratch_shapes=[pltpu.VMEM((tm, D_FF), BF16), window, window, window, window,
                        pltpu.SemaphoreType.DMA((4,))],
        compiler_params=pltpu.CompilerParams(dimension_semantics=("arbitrary",),
                                             vmem_limit_bytes=VMEM_LIMIT_FUSED),
        name="ffn_side",
    )(x, att, sg, *weights, q_s, ktn, vtn, logc, kt, vt)


def _rel_bucket(dist):
    max_exact = N_BUCKETS // 2
    df = jnp.maximum(dist, 1).astype(F32)
    large = max_exact + (jnp.log(df / max_exact) / math.log(MAX_DISTANCE / max_exact)
                         * (N_BUCKETS - max_exact)).astype(jnp.int32)
    large = jnp.minimum(large, N_BUCKETS - 1)
    return jnp.where(dist < max_exact, dist, large)


def _branch_bias_reversed(rel_bias, w, d):
    nj = w // d + 1
    dist = (nj - 1 - jnp.arange(nj, dtype=jnp.int32)) * d
    return rel_bias[_rel_bucket(dist)].T.astype(F32)


def _prompt_bias_rows(rev):
    rows = [jnp.pad(r, ((0, 0), (0, 2 * Q_BLOCK - r.shape[1])), constant_values=NEG_INF)
            for r in rev]
    return jnp.stack(rows) * LOG2E


def _sample_bias_table(rev, wb, t_new):
    width = wb + LANES
    per_branch = []
    for (w, d), r in zip(BRANCHES, rev):
        nj = r.shape[1]
        if d > 1:
            fill = jnp.full((N_HEADS, nj, d - 1), NEG_INF, F32)
            r = jnp.concatenate([r[:, :, None], fill], axis=2).reshape(N_HEADS, nj * d)
        rows = []
        for t in range(t_new):
            base = wb + t - (nj - 1) * d
            rows.append(jnp.pad(r, ((0, 0), (base, width - base - nj * d)),
                                constant_values=NEG_INF))
        per_branch.append(jnp.stack(rows))
    x = jnp.stack(per_branch)
    mx = jnp.max(x, axis=0)
    logc = mx + jnp.log(jnp.sum(jnp.exp(x - mx), axis=0))
    return logc.reshape(t_new * N_HEADS, width)


def kernel(x_prompt, x_sample, cache_k_win, cache_v_win, norm1_g, w_in, sgu_ln_g, sgu_ln_b,
           sgu_w, sgu_b, w_out, norm2_g, w_gate, w_up, w_down, rel_bias, final_g):
    depth = w_in.shape[0]
    assert depth == 1
    b, s, _ = x_prompt.shape
    bd, t_new, _ = x_sample.shape
    wb = cache_k_win.shape[2]
    assert bd * t_new == CHUNK and s % (Q_BLOCK * 16) == 0 and wb == WINDOW
    assert (s // Q_BLOCK) % ATTN_GROUP == 0 and s % PREP_ROWS == 0

    l = 0
    w_in_b = w_in[l].astype(BF16)
    g1 = norm1_g[l][None]
    g2 = norm2_g[l][None]
    gf = final_g[None]
    ln_g = sgu_ln_g[l][None]
    ln_b = sgu_ln_b[l][None]

    causal = jnp.tril(jnp.ones((CHUNK, CHUNK), F32))
    wm = sgu_w[l] * causal
    mix_p = wm.astype(BF16)
    mixb_p = jnp.repeat(sgu_b[l].T, HEAD_DIM, axis=1)
    eye = jnp.eye(bd, dtype=F32)
    mix_s = jnp.einsum('ab,gts->gatbs', eye, wm[:, :t_new, :t_new]).reshape(
        N_GROUPS, CHUNK, CHUNK).astype(BF16)
    mixb_s = jnp.tile(mixb_p[:t_new], (bd, 1))

    rev = [_branch_bias_reversed(rel_bias, w, d) for w, d in BRANCHES]

    xp = x_prompt.reshape(b * s, D_MODEL)
    q, k, v, sg, kt_p, vt_p, wo, wg, wu, wd = _proj(
        xp, g1, w_in_b, ln_g, ln_b, mix_p, mixb_p, tm=512, emit_vn=False, emit_t=True, seq_len=s,
        cast=(w_out[l], w_gate[l], w_up[l], w_down[l]))
    att = _attn(q.reshape(b, s, D_ATT), k.reshape(b, s, D_ATT), v.reshape(b, s, D_ATT),
                _prompt_bias_rows(rev))
    nw = min(WINDOW, s)
    new_k_p = kt_p.reshape(1, b, N_HEADS, HEAD_DIM, nw).transpose(0, 1, 4, 2, 3)
    new_v_p = vt_p.reshape(1, b, N_HEADS, HEAD_DIM, nw).transpose(0, 1, 4, 2, 3)

    xs = x_sample.reshape(bd * t_new, D_MODEL)
    qs, _, _, sgs, vn_s, kt_n, vt_n = _proj(xs, g1, w_in_b, ln_g, ln_b, mix_s, mixb_s,
                                            tm=CHUNK, emit_vn=True, emit_t=True, seq_len=CHUNK)
    kt_c = cache_k_win[l].transpose(0, 2, 3, 1).reshape(bd, D_ATT, wb)
    vt_c = cache_v_win[l].transpose(0, 2, 3, 1).reshape(bd, D_ATT, wb)

    logc = _sample_bias_table(rev, wb, t_new)
    weights = (wo, g2, wg, wu, wd, gf)
    y_prompt, att_s, kt_o, vt_o = _ffn_with_sample_side(
        xp, att.reshape(b * s, D_ATT), sg, weights, qs.reshape(bd * t_new // 8, 8, D_ATT),
        kt_n, vt_n, logc, kt_c, vt_c, tm=(b * s) // bd, t_new=t_new)
    y_prompt = y_prompt.reshape(b, s, D_MODEL)
    y_sample = _ffn(xs, att_s.reshape(bd * t_new, D_ATT), sgs, weights,
                    tm=CHUNK).reshape(bd, t_new, D_MODEL)
    new_k_s = kt_o.reshape(1, bd, N_HEADS, HEAD_DIM, wb).transpose(0, 1, 4, 2, 3)
    new_v_s = vt_o.reshape(1, bd, N_HEADS, HEAD_DIM, wb).transpose(0, 1, 4, 2, 3)
    sgu_v = vn_s.reshape(1, bd, t_new, D_SGU)

    return (y_prompt, y_sample, new_k_p, new_v_p, new_k_s, new_v_s, sgu_v)
```

```python
import functools
import math

import jax
import jax.numpy as jnp
from jax import lax
from jax.experimental import pallas as pl
from jax.experimental.pallas import tpu as pltpu

D_MODEL = 1024
N_HEADS = 8
HEAD_DIM = 64
D_ATT = N_HEADS * HEAD_DIM
N_GROUPS = 8
D_SGU = 512
CHUNK = 128
BRANCHES = ((128, 1), (512, 4), (2048, 16))
WINDOW = 2048
Q_BLOCK = 128
N_BUCKETS = 32
MAX_DISTANCE = WINDOW
D_FF = 2816
EPS = 1e-6
NEG_INF = -1e30
LOG2E = math.log2(math.e)

LANES = 128
FF_CHUNK = 256
N_FF_CHUNKS = D_FF // FF_CHUNK
VMEM_LIMIT = 48 * 1024 * 1024
VMEM_LIMIT_FUSED = 60 * 1024 * 1024
VMEM_LIMIT_ATTN = 56 * 1024 * 1024
SIDE_TILES_PER_STEP = 8
ATTN_GROUP = 8
PREP_ROWS = 256
BF16_ROWS = 16

F32 = jnp.float32
BF16 = jnp.bfloat16


def _const_spec(shape):
    nd = len(shape)
    return pl.BlockSpec(shape, lambda *_: (0,) * nd, pipeline_mode=pl.Buffered(1))


def _proj_kernel(x_ref, g1_ref, w_ref, lng_ref, lnb_ref, mix_ref, mixb_ref, *refs,
                 tm, emit_vn, emit_t, n_cast):
    cast_in, outs = refs[:n_cast], refs[n_cast:]
    cast_out = outs[len(outs) - n_cast:]
    q_ref, k_ref, v_ref, sg_ref = outs[:4]
    rest = outs[4:len(outs) - n_cast]
    lane = lax.broadcasted_iota(jnp.int32, (1, LANES), 1)
    low = lane < HEAD_DIM

    x = x_ref[...]
    xg = (x * g1_ref[...]).astype(BF16)
    r = lax.rsqrt(jnp.mean(x * x, axis=-1, keepdims=True) + EPS)

    def proj(c0):
        return jnp.dot(xg, w_ref[:, c0:c0 + D_ATT], preferred_element_type=F32) * r

    vg = proj(3 * D_ATT + D_SGU)
    u = proj(3 * D_ATT)
    mu = jnp.mean(vg, axis=-1, keepdims=True)
    dv = vg - mu
    var = jnp.mean(dv * dv, axis=-1, keepdims=True)
    vn = dv * lax.rsqrt(var + EPS) * lng_ref[...] + lnb_ref[...]
    if emit_vn:
        rest[0][...] = vn
    q_ref[...] = proj(0)

    def gating(c0):
        for s in range(D_SGU // LANES):
            cols = slice(s * LANES, (s + 1) * LANES)
            slab = vn[c0:c0 + CHUNK, cols]
            lo = jnp.where(low, slab, 0.0).astype(BF16)
            hi = jnp.where(low, 0.0, slab).astype(BF16)
            gate = (jnp.dot(mix_ref[2 * s], lo, preferred_element_type=F32)
                    + jnp.dot(mix_ref[2 * s + 1], hi, preferred_element_type=F32)
                    + mixb_ref[:, cols])
            sg_ref[c0:c0 + CHUNK, cols] = (u[c0:c0 + CHUNK, cols] * gate).astype(BF16)

    chunks = list(range(0, tm, CHUNK))
    for c0 in chunks[:len(chunks) // 2]:
        gating(c0)
    k = proj(D_ATT)
    k_ref[...] = k
    for src, dst in zip(cast_in, cast_out):
        dst[...] = src[...].astype(BF16)
    for c0 in chunks[len(chunks) // 2:]:
        gating(c0)
    v = proj(2 * D_ATT)
    v_ref[...] = v
    if emit_t:
        kt_ref, vt_ref = rest[-2:]
        kt_ref[0] = k.T
        vt_ref[0] = v.T


def _proj(x, g1, w_in, ln_g, ln_b, mix, mixb, *, tm, emit_vn, emit_t, seq_len=None, cast=()):
    t = x.shape[0]
    n_tiles = t // tm
    row = lambda shape: pl.BlockSpec(shape, lambda i: (i, 0))
    out_shape = [jax.ShapeDtypeStruct((t, D_ATT), F32)] * 3 + [jax.ShapeDtypeStruct((t, D_SGU), BF16)]
    out_specs = [row((tm, D_ATT))] * 3 + [row((tm, D_SGU))]
    if emit_vn:
        out_shape.append(jax.ShapeDtypeStruct((t, D_SGU), F32))
        out_specs.append(row((tm, D_SGU)))
    if emit_t:
        tiles_per_seq = seq_len // tm
        win = min(WINDOW, seq_len)
        first = tiles_per_seq - win // tm

        def t_map(i):
            return (i // tiles_per_seq, 0, jnp.maximum(i % tiles_per_seq - first, 0))

        for _ in range(2):
            out_shape.append(jax.ShapeDtypeStruct((t // seq_len, D_ATT, win), F32))
            out_specs.append(pl.BlockSpec((1, D_ATT, tm), t_map))
    cast_specs = []
    for w in cast:
        steps = max(n for n in range(1, n_tiles + 1)
                    if w.shape[0] % n == 0 and (w.shape[0] // n) % BF16_ROWS == 0)
        spec = pl.BlockSpec((w.shape[0] // steps, w.shape[1]),
                            lambda i, steps=steps: (jnp.minimum(i, steps - 1), 0))
        cast_specs.append(spec)
        out_shape.append(jax.ShapeDtypeStruct(w.shape, BF16))
        out_specs.append(spec)
    kern = functools.partial(_proj_kernel, tm=tm, emit_vn=emit_vn, emit_t=emit_t, n_cast=len(cast))
    return pl.pallas_call(
        kern,
        grid=(n_tiles,),
        in_specs=[row((tm, D_MODEL)), _const_spec(g1.shape), _const_spec(w_in.shape),
                  _const_spec(ln_g.shape), _const_spec(ln_b.shape), _const_spec(mix.shape),
                  _const_spec(mixb.shape)] + cast_specs,
        out_specs=out_specs,
        out_shape=out_shape,
        compiler_params=pltpu.CompilerParams(dimension_semantics=("arbitrary",),
                                             vmem_limit_bytes=VMEM_LIMIT),
        name="proj_cast" if cast else "proj",
    )(x, g1, w_in, ln_g, ln_b, mix, mixb, *cast)


def _div_pow2(x, n):
    assert n & (n - 1) == 0
    return lax.shift_right_logical(x, n.bit_length() - 1)


def _mod_pow2(x, n):
    assert n & (n - 1) == 0
    return lax.bitwise_and(x, n - 1)


def _attn_kernel(q_ref, k_ref, v_ref, r0_ref, o_ref, tab_ref, acc_ref, m_ref, l_ref,
                 q0_ref, q1_ref, kd_ref, v0_ref, v1_ref, s_ref, mx_ref, *, seq_len):
    hp = pl.program_id(1)
    lane = lax.broadcasted_iota(jnp.int32, (1, LANES), 1)
    low = lane < HEAD_DIM
    contract_last = (((1,), (1,)), ((), ()))
    neg = NEG_INF * LOG2E
    quarter = seq_len // 4
    lo_f = jnp.where(low, 1.0, 0.0).astype(F32)
    hi_f = 1.0 - lo_f
    q_scale = HEAD_DIM ** -0.5 * LOG2E

    col = lax.broadcasted_iota(jnp.int32, (Q_BLOCK, 2 * Q_BLOCK), 1)
    for br in range(len(BRANCHES)):
        for h in range(2):
            base = jnp.broadcast_to(r0_ref[br, pl.ds(2 * hp + h, 1), :], (Q_BLOCK, 2 * Q_BLOCK))
            t = pltpu.roll(base, 0, 1, stride=1, stride_axis=0)
            tab_ref[br, 0, h] = t
            tab_ref[br, 1, h] = jnp.where(col >= Q_BLOCK, t, neg)
    zeros = jnp.zeros((Q_BLOCK, LANES), BF16)
    for slot in range(2):
        kd_ref[slot, 0:Q_BLOCK, :] = zeros
        v0_ref[slot, 0:Q_BLOCK, :] = zeros
        v1_ref[slot, 0:Q_BLOCK, :] = zeros

    def strided(start, n, d):
        return pl.ds(start, n) if d == 1 else pl.ds(start, n, stride=d)

    dilation = [d for _, d in BRANCHES]
    n_groups = seq_len // Q_BLOCK // ATTN_GROUP

    def prep(br):
        d, slot = dilation[br], br % 2
        sub_len = seq_len // d

        def chunk(c, carry):
            de0 = c * PREP_ROWS
            dst = pl.ds(pl.multiple_of(de0, PREP_ROWS), PREP_ROWS)
            dstp = pl.ds(pl.multiple_of(de0 + Q_BLOCK, Q_BLOCK), PREP_ROWS)
            if d <= 4:
                src = strided(_div_pow2(de0, sub_len) + _mod_pow2(de0, sub_len) * d, PREP_ROWS, d)
                qq, kk, vv = q_ref[0, src, :], k_ref[0, src, :], v_ref[0, src, :]
            else:
                res, m0 = _div_pow2(de0, sub_len), _mod_pow2(de0, sub_len)
                src = pl.ds(_mod_pow2(res, 4) * quarter + m0 * (d // 4) + _div_pow2(res, 4),
                            PREP_ROWS, stride=d // 4)
                qq, kk, vv = acc_ref[2, src, :], m_ref[2, src, :], l_ref[2, src, :]
            if d == 4:
                acc_ref[2, dst, :] = qq
                m_ref[2, dst, :] = kk
                l_ref[2, dst, :] = vv
            q0_ref[slot, dst, :] = (qq * (lo_f * q_scale)).astype(BF16)
            q1_ref[slot, dst, :] = (qq * (hi_f * q_scale)).astype(BF16)
            kd_ref[slot, dstp, :] = kk.astype(BF16)
            v0_ref[slot, dstp, :] = (vv * lo_f + hi_f).astype(BF16)
            v1_ref[slot, dstp, :] = (vv * hi_f + lo_f).astype(BF16)
            return carry

        lax.fori_loop(0, seq_len // PREP_ROWS, chunk, 0)

    def score(br, i):
        slot, nb = br % 2, seq_len // dilation[br] // Q_BLOCK
        for u in range(ATTN_GROUP):
            g = i * ATTN_GROUP + u
            rq = pl.ds(pl.multiple_of(g * Q_BLOCK, Q_BLOCK), Q_BLOCK)
            rk = pl.ds(pl.multiple_of(g * Q_BLOCK, Q_BLOCK), 2 * Q_BLOCK)
            first = jnp.where(_mod_pow2(g, nb) == 0, 1, 0)
            kb = kd_ref[slot, rk, :]
            for h, qh_ref in enumerate((q0_ref, q1_ref)):
                s = lax.dot_general(qh_ref[slot, rq, :], kb, contract_last,
                                    preferred_element_type=F32)
                s = s + tab_ref[br, first, h]
                s_ref[u, h] = s
                mx_ref[u, h] = jnp.broadcast_to(jnp.max(s, axis=-1, keepdims=True),
                                                (Q_BLOCK, LANES))

    def finish(br, i):
        d, slot = dilation[br], br % 2
        nb = seq_len // d // Q_BLOCK
        for u in range(ATTN_GROUP):
            g = i * ATTN_GROUP + u
            rk = pl.ds(pl.multiple_of(g * Q_BLOCK, Q_BLOCK), 2 * Q_BLOCK)
            outs = []
            for h, vh_ref in enumerate((v0_ref, v1_ref)):
                m = mx_ref[u, h]
                p = jnp.concatenate([jnp.exp2(s_ref[u, h, :, :LANES] - m),
                                     jnp.exp2(s_ref[u, h, :, LANES:] - m)], axis=1)
                outs.append((jnp.dot(p.astype(BF16), vh_ref[slot, rk, :],
                                     preferred_element_type=F32), m))
            (o0, m0), (o1, m1) = outs
            if d <= 4:
                idx = pl.ds(pl.multiple_of(g * Q_BLOCK, Q_BLOCK), Q_BLOCK)
            else:
                res, blk = _div_pow2(g, nb), _mod_pow2(g, nb)
                idx = pl.ds(_mod_pow2(res, 4) * quarter + blk * (Q_BLOCK * d // 4)
                            + _div_pow2(res, 4), Q_BLOCK, stride=d // 4)
            acc_ref[br, idx, :] = jnp.where(low, o0, o1)
            l_ref[br, idx, :] = pltpu.roll(jnp.where(low, o1, o0), HEAD_DIM, 1)
            m_ref[br, idx, :] = jnp.where(low, m0, m1)

    prep(0)
    score(0, 0)
    for br in range(len(BRANCHES)):
        def step(i, carry, br=br):
            finish(br, i - 1)
            score(br, i)
            return carry

        lax.fori_loop(1, n_groups, step, 0)
        if br + 1 < len(BRANCHES):
            prep(br + 1)
            finish(br, n_groups - 1)
            score(br + 1, 0)
        else:
            finish(br, n_groups - 1)

    def merge(c, carry):
        de0 = c * PREP_ROWS
        dil = pl.ds(pl.multiple_of(de0, PREP_ROWS), PREP_ROWS)
        nat = pl.ds(_div_pow2(de0, quarter) + _mod_pow2(de0, quarter) * 4, PREP_ROWS, stride=4)
        ms = (m_ref[0, nat, :], m_ref[1, dil, :], m_ref[2, dil, :])
        m_all = jnp.maximum(jnp.maximum(ms[0], ms[1]), ms[2])
        ws = [jnp.exp2(m - m_all) for m in ms]
        num = (acc_ref[0, nat, :] * ws[0] + acc_ref[1, dil, :] * ws[1] + acc_ref[2, dil, :] * ws[2])
        den = l_ref[0, nat, :] * ws[0] + l_ref[1, dil, :] * ws[1] + l_ref[2, dil, :] * ws[2]
        acc_ref[0, nat, :] = num / den
        return carry

    lax.fori_loop(0, seq_len // PREP_ROWS, merge, 0)

    rows_per = 512

    def fin(i, c):
        sl = pl.ds(pl.multiple_of(i * rows_per, rows_per), rows_per)
        o_ref[0, sl, :] = acc_ref[0, sl, :].astype(o_ref.dtype)
        return c

    lax.fori_loop(0, seq_len // rows_per, fin, 0)


def _attn(q, k, v, r0):
    b, s, _ = q.shape
    n_br = len(BRANCHES)
    blk = pl.BlockSpec((1, s, LANES), lambda i, j: (i, 0, j))
    state = pltpu.VMEM((n_br, s, LANES), F32)
    qd = pltpu.VMEM((2, s, LANES), BF16)
    kd = pltpu.VMEM((2, s + Q_BLOCK, LANES), BF16)
    return pl.pallas_call(
        functools.partial(_attn_kernel, seq_len=s),
        grid=(b, D_ATT // LANES),
        in_specs=[blk, blk, blk, _const_spec(r0.shape)],
        out_specs=blk,
        out_shape=jax.ShapeDtypeStruct((b, s, D_ATT), BF16),
        scratch_shapes=[pltpu.VMEM((n_br, 2, 2, Q_BLOCK, 2 * Q_BLOCK), F32),
                        state, state, state, qd, qd, kd, kd, kd,
                        pltpu.VMEM((ATTN_GROUP, 2, Q_BLOCK, 2 * Q_BLOCK), F32),
                        pltpu.VMEM((ATTN_GROUP, 2, Q_BLOCK, LANES), F32)],
        compiler_params=pltpu.CompilerParams(dimension_semantics=("arbitrary", "arbitrary"),
                                             vmem_limit_bytes=VMEM_LIMIT_ATTN),
        name="attn",
    )(q, k, v, r0)


def _sample_side_steps(b, q_ref, ktn_ref, vtn_ref, logc_ref, kin_ref, vin_ref, kout_ref, vout_ref,
                       att_ref, *, wb, t_new):
    rows = t_new * N_HEADS
    contract_last = (((1,), (1,)), ((), ()))
    lane = lax.broadcasted_iota(jnp.int32, (1, LANES), 1)
    keep = lane < LANES - t_new
    n_tiles = wb // LANES
    st = {}

    def scores():
        shift = lax.bitwise_and(LANES - t_new * b, LANES - 1)
        st["ktn"] = pltpu.roll(ktn_ref[0], shift, 1)
        st["vtn"] = pltpu.roll(vtn_ref[0], shift, 1)
        q_tile = q_ref[0] * (HEAD_DIM ** -0.5)
        per_tile = 8 // t_new
        q = q_tile[0:t_new]
        for i in range(1, per_tile):
            q = jnp.where(lax.rem(b, per_tile) == i, q_tile[i * t_new:(i + 1) * t_new], q)
        qrep = jnp.broadcast_to(q[:, None, :], (t_new, N_HEADS, D_ATT)).reshape(rows, D_ATT)
        row_h = lax.broadcasted_iota(jnp.int32, (rows, D_ATT), 0) % N_HEADS
        col_h = lax.broadcasted_iota(jnp.int32, (rows, D_ATT), 1) // HEAD_DIM
        st["own"] = row_h == col_h
        qbd = jnp.where(st["own"], qrep, 0.0).astype(BF16)
        st["s"] = (jnp.dot(qbd, kin_ref[...].astype(BF16), preferred_element_type=F32)
                   + logc_ref[:, :wb])
        st["sn"] = (jnp.dot(qbd, st["ktn"].astype(BF16), preferred_element_type=F32)
                    + logc_ref[:, wb:])

    def softmax():
        s, sn = st["s"], st["sn"]
        m = jnp.maximum(jnp.max(s, axis=-1, keepdims=True), jnp.max(sn, axis=-1, keepdims=True))
        p = jnp.exp(s - m)
        pn = jnp.exp(sn - m)
        st["l"] = jnp.sum(p, axis=-1, keepdims=True) + jnp.sum(pn, axis=-1, keepdims=True)
        st["p"], st["pn"] = p.astype(BF16), pn.astype(BF16)

    def values():
        o = (lax.dot_general(st["p"], vin_ref[...].astype(BF16), contract_last,
                             preferred_element_type=F32)
             + lax.dot_general(st["pn"], st["vtn"].astype(BF16), contract_last,
                               preferred_element_type=F32))
        o = jnp.where(st["own"], o, 0.0) / st["l"]
        att_ref[0] = jnp.sum(o.reshape(t_new, N_HEADS, D_ATT), axis=1)

    def shift_tiles(src_ref, new_key, dst_ref, j0, j1):
        def run():
            cur = st.get(("cur", new_key))
            if cur is None:
                cur = pltpu.roll(src_ref[:, 0:LANES], LANES - t_new, 1)
            for j in range(j0, j1):
                if j + 1 < n_tiles:
                    nxt = pltpu.roll(src_ref[:, (j + 1) * LANES:(j + 2) * LANES], LANES - t_new, 1)
                else:
                    nxt = pltpu.roll(st[new_key], LANES - t_new, 1)
                dst_ref[:, j * LANES:(j + 1) * LANES] = jnp.where(keep, cur, nxt)
                cur = nxt
            st[("cur", new_key)] = cur
        return run

    per = SIDE_TILES_PER_STEP
    k_steps = [scores] + [shift_tiles(kin_ref, "ktn", kout_ref, j0, min(j0 + per, n_tiles))
                          for j0 in range(0, n_tiles, per)]
    v_steps = [softmax, values] + [shift_tiles(vin_ref, "vtn", vout_ref, j0, min(j0 + per, n_tiles))
                                   for j0 in range(0, n_tiles, per)]
    return k_steps, v_steps


def _ffn_body(x_ref, att_ref, sg_ref, wo_ref, g2_ref, wg_ref, wu_ref, wd_ref, gf_ref,
              y_ref, a_ref, side_steps=(), head_steps=0):
    side = list(side_steps)
    x1 = (x_ref[...]
          + jnp.dot(att_ref[...].astype(BF16), wo_ref[:D_ATT, :], preferred_element_type=F32)
          + jnp.dot(sg_ref[...], wo_ref[D_ATT:, :], preferred_element_type=F32))
    for _ in range(min(head_steps, len(side))):
        side.pop(0)()
    assert len(side) < N_FF_CHUNKS
    ms = jnp.mean(x1 * x1, axis=-1, keepdims=True)
    h = (x1 * lax.rsqrt(ms + EPS) * g2_ref[...]).astype(BF16)
    for c in range(N_FF_CHUNKS):
        cols = slice(c * FF_CHUNK, (c + 1) * FF_CHUNK)
        g = jnp.dot(h, wg_ref[:, cols], preferred_element_type=F32)
        u = jnp.dot(h, wu_ref[:, cols], preferred_element_type=F32)
        a_ref[:, cols] = (g * jax.nn.sigmoid(g) * u).astype(BF16)
        if side:
            side.pop(0)()
    x2 = x1 + jnp.dot(a_ref[...], wd_ref[...], preferred_element_type=F32)
    ms2 = jnp.mean(x2 * x2, axis=-1, keepdims=True)
    y_ref[...] = x2 * lax.rsqrt(ms2 + EPS) * gf_ref[...]


def _ffn_kernel(*refs):
    _ffn_body(*refs)


def _ffn_side_kernel(x_ref, att_ref, sg_ref, wo_ref, g2_ref, wg_ref, wu_ref, wd_ref, gf_ref,
                     q_ref, ktn_ref, vtn_ref, logc_ref, kt_hbm, vt_hbm,
                     y_ref, atts_ref, kto_hbm, vto_hbm,
                     a_ref, kin_ref, vin_ref, kout_ref, vout_ref, sems, *, wb, t_new):
    i = pl.program_id(0)
    last = pl.num_programs(0) - 1

    def fetch(j):
        return (pltpu.make_async_copy(kt_hbm.at[j], kin_ref, sems.at[0]),
                pltpu.make_async_copy(vt_hbm.at[j], vin_ref, sems.at[1]))

    def flush(j):
        return (pltpu.make_async_copy(kout_ref, kto_hbm.at[j], sems.at[2]),
                pltpu.make_async_copy(vout_ref, vto_hbm.at[j], sems.at[3]))

    @pl.when(i == 0)
    def _():
        for cp in fetch(0):
            cp.start()

    for cp in fetch(i):
        cp.wait()

    @pl.when(i > 0)
    def _():
        for cp in flush(i - 1):
            cp.wait()

    def then_swap(step, which):
        def run():
            step()
            flush(i)[which].start()
            fetch(jnp.minimum(i + 1, last))[which].start()
        return run

    k_steps, v_steps = _sample_side_steps(i, q_ref, ktn_ref, vtn_ref, logc_ref, kin_ref, vin_ref,
                                          kout_ref, vout_ref, atts_ref, wb=wb, t_new=t_new)
    k_steps[-1] = then_swap(k_steps[-1], 0)
    v_steps[-1] = then_swap(v_steps[-1], 1)
    steps = k_steps[:1] + v_steps[:2] + k_steps[1:] + v_steps[2:]
    _ffn_body(x_ref, att_ref, sg_ref, wo_ref, g2_ref, wg_ref, wu_ref, wd_ref, gf_ref,
              y_ref, a_ref, side_steps=steps, head_steps=3)

    @pl.when(i == last)
    def _():
        for cp in fetch(i) + flush(i):
            cp.wait()


def _ffn_specs(tm, weights):
    row = lambda shape: pl.BlockSpec(shape, lambda i: (i, 0))
    return ([row((tm, D_MODEL)), row((tm, D_ATT)), row((tm, D_SGU))]
            + [_const_spec(w.shape) for w in weights]), row((tm, D_MODEL))


def _ffn(x, att, sg, weights, *, tm):
    t = x.shape[0]
    in_specs, out_spec = _ffn_specs(tm, weights)
    return pl.pallas_call(
        _ffn_kernel,
        grid=(t // tm,),
        in_specs=in_specs,
        out_specs=out_spec,
        out_shape=jax.ShapeDtypeStruct((t, D_MODEL), F32),
        scratch_shapes=[pltpu.VMEM((tm, D_FF), BF16)],
        compiler_params=pltpu.CompilerParams(dimension_semantics=("arbitrary",),
                                             vmem_limit_bytes=VMEM_LIMIT),
        name="ffn",
    )(x, att, sg, *weights)


def _ffn_with_sample_side(x, att, sg, weights, q_s, ktn, vtn, logc, kt, vt, *, tm, t_new):
    t = x.shape[0]
    bd, _, wb = kt.shape
    assert t // tm == bd
    in_specs, out_spec = _ffn_specs(tm, weights)
    any_spec = pl.BlockSpec(memory_space=pl.ANY)
    window = pltpu.VMEM((D_ATT, wb), F32)
    return pl.pallas_call(
        functools.partial(_ffn_side_kernel, wb=wb, t_new=t_new),
        grid=(bd,),
        in_specs=in_specs + [pl.BlockSpec((1, 8, D_ATT), lambda i: (i // (8 // t_new), 0, 0)),
                             _const_spec(ktn.shape), _const_spec(vtn.shape),
                             _const_spec(logc.shape), any_spec, any_spec],
        out_specs=[out_spec, pl.BlockSpec((1, t_new, D_ATT), lambda i: (i, 0, 0)),
                   any_spec, any_spec],
        out_shape=[jax.ShapeDtypeStruct((t, D_MODEL), F32),
                   jax.ShapeDtypeStruct((bd, t_new, D_ATT), F32),
                   jax.ShapeDtypeStruct(kt.shape, F32), jax.ShapeDtypeStruct(vt.shape, F32)],
        scratch_shapes=[pltpu.VMEM((tm, D_FF), BF16), window, window, window, window,
                        pltpu.SemaphoreType.DMA((4,))],
        compiler_params=pltpu.CompilerParams(dimension_semantics=("arbitrary",),
                                             vmem_limit_bytes=VMEM_LIMIT_FUSED),
        name="ffn_side",
    )(x, att, sg, *weights, q_s, ktn, vtn, logc, kt, vt)


def _rel_bucket(dist):
    max_exact = N_BUCKETS // 2
    df = jnp.maximum(dist, 1).astype(F32)
    large = max_exact + (jnp.log(df / max_exact) / math.log(MAX_DISTANCE / max_exact)
                         * (N_BUCKETS - max_exact)).astype(jnp.int32)
    large = jnp.minimum(large, N_BUCKETS - 1)
    return jnp.where(dist < max_exact, dist, large)


def _branch_bias_reversed(rel_bias, w, d):
    nj = w // d + 1
    dist = (nj - 1 - jnp.arange(nj, dtype=jnp.int32)) * d
    return rel_bias[_rel_bucket(dist)].T.astype(F32)


def _prompt_bias_rows(rev):
    rows = [jnp.pad(r, ((0, 0), (0, 2 * Q_BLOCK - r.shape[1])), constant_values=NEG_INF)
            for r in rev]
    return jnp.stack(rows) * LOG2E


def _sample_bias_table(rev, wb, t_new):
    width = wb + LANES
    per_branch = []
    for (w, d), r in zip(BRANCHES, rev):
        nj = r.shape[1]
        if d > 1:
            fill = jnp.full((N_HEADS, nj, d - 1), NEG_INF, F32)
            r = jnp.concatenate([r[:, :, None], fill], axis=2).reshape(N_HEADS, nj * d)
        rows = []
        for t in range(t_new):
            base = wb + t - (nj - 1) * d
            rows.append(jnp.pad(r, ((0, 0), (base, width - base - nj * d)),
                                constant_values=NEG_INF))
        per_branch.append(jnp.stack(rows))
    x = jnp.stack(per_branch)
    mx = jnp.max(x, axis=0)
    logc = mx + jnp.log(jnp.sum(jnp.exp(x - mx), axis=0))
    return logc.reshape(t_new * N_HEADS, width)


def kernel(x_prompt, x_sample, cache_k_win, cache_v_win, norm1_g, w_in, sgu_ln_g, sgu_ln_b,
           sgu_w, sgu_b, w_out, norm2_g, w_gate, w_up, w_down, rel_bias, final_g):
    depth = w_in.shape[0]
    assert depth == 1
    b, s, _ = x_prompt.shape
    bd, t_new, _ = x_sample.shape
    wb = cache_k_win.shape[2]
    assert bd * t_new == CHUNK and s % (Q_BLOCK * 16) == 0 and wb == WINDOW
    assert (s // Q_BLOCK) % ATTN_GROUP == 0 and s % PREP_ROWS == 0

    l = 0
    w_in_b = w_in[l].astype(BF16)
    g1 = norm1_g[l][None]
    g2 = norm2_g[l][None]
    gf = final_g[None]
    ln_g = sgu_ln_g[l][None]
    ln_b = sgu_ln_b[l][None]

    causal = jnp.tril(jnp.ones((CHUNK, CHUNK), F32))
    wm = sgu_w[l] * causal
    mix_p = wm.astype(BF16)
    mixb_p = jnp.repeat(sgu_b[l].T, HEAD_DIM, axis=1)
    eye = jnp.eye(bd, dtype=F32)
    mix_s = jnp.einsum('ab,gts->gatbs', eye, wm[:, :t_new, :t_new]).reshape(
        N_GROUPS, CHUNK, CHUNK).astype(BF16)
    mixb_s = jnp.tile(mixb_p[:t_new], (bd, 1))

    rev = [_branch_bias_reversed(rel_bias, w, d) for w, d in BRANCHES]

    xp = x_prompt.reshape(b * s, D_MODEL)
    q, k, v, sg, kt_p, vt_p, wo, wg, wu, wd = _proj(
        xp, g1, w_in_b, ln_g, ln_b, mix_p, mixb_p, tm=512, emit_vn=False, emit_t=True, seq_len=s,
        cast=(w_out[l], w_gate[l], w_up[l], w_down[l]))
    att = _attn(q.reshape(b, s, D_ATT), k.reshape(b, s, D_ATT), v.reshape(b, s, D_ATT),
                _prompt_bias_rows(rev))
    nw = min(WINDOW, s)
    new_k_p = kt_p.reshape(1, b, N_HEADS, HEAD_DIM, nw).transpose(0, 1, 4, 2, 3)
    new_v_p = vt_p.reshape(1, b, N_HEADS, HEAD_DIM, nw).transpose(0, 1, 4, 2, 3)

    xs = x_sample.reshape(bd * t_new, D_MODEL)
    qs, _, _, sgs, vn_s, kt_n, vt_n = _proj(xs, g1, w_in_b, ln_g, ln_b, mix_s, mixb_s,
                                            tm=CHUNK, emit_vn=True, emit_t=True, seq_len=CHUNK)
    kt_c = cache_k_win[l].transpose(0, 2, 3, 1).reshape(bd, D_ATT, wb)
    vt_c = cache_v_win[l].transpose(0, 2, 3, 1).reshape(bd, D_ATT, wb)

    logc = _sample_bias_table(rev, wb, t_new)
    weights = (wo, g2, wg, wu, wd, gf)
    y_prompt, att_s, kt_o, vt_o = _ffn_with_sample_side(
        xp, att.reshape(b * s, D_ATT), sg, weights, qs.reshape(bd * t_new // 8, 8, D_ATT),
        kt_n, vt_n, logc, kt_c, vt_c, tm=(b * s) // bd, t_new=t_new)
    y_prompt = y_prompt.reshape(b, s, D_MODEL)
    y_sample = _ffn(xs, att_s.reshape(bd * t_new, D_ATT), sgs, weights,
                    tm=CHUNK).reshape(bd, t_new, D_MODEL)
    new_k_s = kt_o.reshape(1, bd, N_HEADS, HEAD_DIM, wb).transpose(0, 1, 4, 2, 3)
    new_v_s = vt_o.reshape(1, bd, N_HEADS, HEAD_DIM, wb).transpose(0, 1, 4, 2, 3)
    sgu_v = vn_s.reshape(1, bd, t_new, D_SGU)

    return (y_prompt, y_sample, new_k_p, new_v_p, new_k_s, new_v_s, sgu_v)
```

```python
import functools
import math

import jax
import jax.numpy as jnp
from jax import lax
from jax.experimental import pallas as pl
from jax.experimental.pallas import tpu as pltpu

D_MODEL = 1024
N_HEADS = 8
HEAD_DIM = 64
D_ATT = N_HEADS * HEAD_DIM
N_GROUPS = 8
D_SGU = 512
CHUNK = 128
BRANCHES = ((128, 1), (512, 4), (2048, 16))
WINDOW = 2048
Q_BLOCK = 128
N_BUCKETS = 32
MAX_DISTANCE = WINDOW
D_FF = 2816
EPS = 1e-6
NEG_INF = -1e30
LOG2E = math.log2(math.e)

LANES = 128
FF_CHUNK = 256
N_FF_CHUNKS = D_FF // FF_CHUNK
VMEM_LIMIT = 48 * 1024 * 1024
VMEM_LIMIT_FUSED = 60 * 1024 * 1024
VMEM_LIMIT_ATTN = 56 * 1024 * 1024
SIDE_TILES_PER_STEP = 8
ATTN_GROUP = 8
PREP_ROWS = 256
BF16_ROWS = 16
FFN_MIN_HALF = 128

F32 = jnp.float32
BF16 = jnp.bfloat16


def _const_spec(shape):
    nd = len(shape)
    return pl.BlockSpec(shape, lambda *_: (0,) * nd, pipeline_mode=pl.Buffered(1))


def _proj_kernel(x_ref, g1_ref, w_ref, lng_ref, lnb_ref, mix_ref, mixb_ref, *refs,
                 tm, emit_vn, emit_t, n_cast):
    cast_in, outs = refs[:n_cast], refs[n_cast:]
    cast_out = outs[len(outs) - n_cast:]
    q_ref, k_ref, v_ref, sg_ref = outs[:4]
    rest = outs[4:len(outs) - n_cast]
    lane = lax.broadcasted_iota(jnp.int32, (1, LANES), 1)
    low = lane < HEAD_DIM

    x = x_ref[...]
    xg = (x * g1_ref[...]).astype(BF16)
    r = lax.rsqrt(jnp.mean(x * x, axis=-1, keepdims=True) + EPS)

    def proj(c0):
        return jnp.dot(xg, w_ref[:, c0:c0 + D_ATT], preferred_element_type=F32) * r

    vg = proj(3 * D_ATT + D_SGU)
    u = proj(3 * D_ATT)
    mu = jnp.mean(vg, axis=-1, keepdims=True)
    dv = vg - mu
    var = jnp.mean(dv * dv, axis=-1, keepdims=True)
    vn = dv * lax.rsqrt(var + EPS) * lng_ref[...] + lnb_ref[...]
    if emit_vn:
        rest[0][...] = vn
    q_ref[...] = proj(0)

    def gating(c0):
        for s in range(D_SGU // LANES):
            cols = slice(s * LANES, (s + 1) * LANES)
            slab = vn[c0:c0 + CHUNK, cols]
            lo = jnp.where(low, slab, 0.0).astype(BF16)
            hi = jnp.where(low, 0.0, slab).astype(BF16)
            gate = (jnp.dot(mix_ref[2 * s], lo, preferred_element_type=F32)
                    + jnp.dot(mix_ref[2 * s + 1], hi, preferred_element_type=F32)
                    + mixb_ref[:, cols])
            sg_ref[c0:c0 + CHUNK, cols] = (u[c0:c0 + CHUNK, cols] * gate).astype(BF16)

    chunks = list(range(0, tm, CHUNK))
    for c0 in chunks[:len(chunks) // 2]:
        gating(c0)
    k = proj(D_ATT)
    k_ref[...] = k
    for src, dst in zip(cast_in, cast_out):
        dst[...] = src[...].astype(BF16)
    for c0 in chunks[len(chunks) // 2:]:
        gating(c0)
    v = proj(2 * D_ATT)
    v_ref[...] = v
    if emit_t:
        kt_ref, vt_ref = rest[-2:]
        kt_ref[0] = k.T
        vt_ref[0] = v.T


def _proj(x, g1, w_in, ln_g, ln_b, mix, mixb, *, tm, emit_vn, emit_t, seq_len=None, cast=()):
    t = x.shape[0]
    n_tiles = t // tm
    row = lambda shape: pl.BlockSpec(shape, lambda i: (i, 0))
    out_shape = [jax.ShapeDtypeStruct((t, D_ATT), F32)] * 3 + [jax.ShapeDtypeStruct((t, D_SGU), BF16)]
    out_specs = [row((tm, D_ATT))] * 3 + [row((tm, D_SGU))]
    if emit_vn:
        out_shape.append(jax.ShapeDtypeStruct((t, D_SGU), F32))
        out_specs.append(row((tm, D_SGU)))
    if emit_t:
        tiles_per_seq = seq_len // tm
        win = min(WINDOW, seq_len)
        first = tiles_per_seq - win // tm

        def t_map(i):
            return (i // tiles_per_seq, 0, jnp.maximum(i % tiles_per_seq - first, 0))

        for _ in range(2):
            out_shape.append(jax.ShapeDtypeStruct((t // seq_len, D_ATT, win), F32))
            out_specs.append(pl.BlockSpec((1, D_ATT, tm), t_map))
    cast_specs = []
    for w in cast:
        steps = max(n for n in range(1, n_tiles + 1)
                    if w.shape[0] % n == 0 and (w.shape[0] // n) % BF16_ROWS == 0)
        spec = pl.BlockSpec((w.shape[0] // steps, w.shape[1]),
                            lambda i, steps=steps: (jnp.minimum(i, steps - 1), 0))
        cast_specs.append(spec)
        out_shape.append(jax.ShapeDtypeStruct(w.shape, BF16))
        out_specs.append(spec)
    kern = functools.partial(_proj_kernel, tm=tm, emit_vn=emit_vn, emit_t=emit_t, n_cast=len(cast))
    return pl.pallas_call(
        kern,
        grid=(n_tiles,),
        in_specs=[row((tm, D_MODEL)), _const_spec(g1.shape), _const_spec(w_in.shape),
                  _const_spec(ln_g.shape), _const_spec(ln_b.shape), _const_spec(mix.shape),
                  _const_spec(mixb.shape)] + cast_specs,
        out_specs=out_specs,
        out_shape=out_shape,
        compiler_params=pltpu.CompilerParams(dimension_semantics=("arbitrary",),
                                             vmem_limit_bytes=VMEM_LIMIT),
        name="proj_cast" if cast else "proj",
    )(x, g1, w_in, ln_g, ln_b, mix, mixb, *cast)


def _div_pow2(x, n):
    assert n & (n - 1) == 0
    return lax.shift_right_logical(x, n.bit_length() - 1)


def _mod_pow2(x, n):
    assert n & (n - 1) == 0
    return lax.bitwise_and(x, n - 1)


def _attn_kernel(q_ref, k_ref, v_ref, r0_ref, o_ref, tab_ref, acc_ref, m_ref, l_ref,
                 q0_ref, q1_ref, kd_ref, v0_ref, v1_ref, s_ref, mx_ref, *, seq_len):
    hp = pl.program_id(1)
    lane = lax.broadcasted_iota(jnp.int32, (1, LANES), 1)
    low = lane < HEAD_DIM
    contract_last = (((1,), (1,)), ((), ()))
    neg = NEG_INF * LOG2E
    quarter = seq_len // 4
    lo_f = jnp.where(low, 1.0, 0.0).astype(F32)
    hi_f = 1.0 - lo_f
    q_scale = HEAD_DIM ** -0.5 * LOG2E

    col = lax.broadcasted_iota(jnp.int32, (Q_BLOCK, 2 * Q_BLOCK), 1)
    for br in range(len(BRANCHES)):
        for h in range(2):
            base = jnp.broadcast_to(r0_ref[br, pl.ds(2 * hp + h, 1), :], (Q_BLOCK, 2 * Q_BLOCK))
            t = pltpu.roll(base, 0, 1, stride=1, stride_axis=0)
            tab_ref[br, 0, h] = t
            tab_ref[br, 1, h] = jnp.where(col >= Q_BLOCK, t, neg)
    zeros = jnp.zeros((Q_BLOCK, LANES), BF16)
    for slot in range(2):
        kd_ref[slot, 0:Q_BLOCK, :] = zeros
        v0_ref[slot, 0:Q_BLOCK, :] = zeros
        v1_ref[slot, 0:Q_BLOCK, :] = zeros

    def strided(start, n, d):
        return pl.ds(start, n) if d == 1 else pl.ds(start, n, stride=d)

    dilation = [d for _, d in BRANCHES]
    n_groups = seq_len // Q_BLOCK // ATTN_GROUP

    def prep(br):
        d, slot = dilation[br], br % 2
        sub_len = seq_len // d

        def chunk(c, carry):
            de0 = c * PREP_ROWS
            dst = pl.ds(pl.multiple_of(de0, PREP_ROWS), PREP_ROWS)
            dstp = pl.ds(pl.multiple_of(de0 + Q_BLOCK, Q_BLOCK), PREP_ROWS)
            if d <= 4:
                src = strided(_div_pow2(de0, sub_len) + _mod_pow2(de0, sub_len) * d, PREP_ROWS, d)
                qq, kk, vv = q_ref[0, src, :], k_ref[0, src, :], v_ref[0, src, :]
            else:
                res, m0 = _div_pow2(de0, sub_len), _mod_pow2(de0, sub_len)
                src = pl.ds(_mod_pow2(res, 4) * quarter + m0 * (d // 4) + _div_pow2(res, 4),
                            PREP_ROWS, stride=d // 4)
                qq, kk, vv = acc_ref[2, src, :], m_ref[2, src, :], l_ref[2, src, :]
            if d == 4:
                acc_ref[2, dst, :] = qq
                m_ref[2, dst, :] = kk
                l_ref[2, dst, :] = vv
            q0_ref[slot, dst, :] = (qq * (lo_f * q_scale)).astype(BF16)
            q1_ref[slot, dst, :] = (qq * (hi_f * q_scale)).astype(BF16)
            kd_ref[slot, dstp, :] = kk.astype(BF16)
            v0_ref[slot, dstp, :] = (vv * lo_f + hi_f).astype(BF16)
            v1_ref[slot, dstp, :] = (vv * hi_f + lo_f).astype(BF16)
            return carry

        lax.fori_loop(0, seq_len // PREP_ROWS, chunk, 0)

    def score(br, i):
        slot, nb = br % 2, seq_len // dilation[br] // Q_BLOCK
        for u in range(ATTN_GROUP):
            g = i * ATTN_GROUP + u
            rq = pl.ds(pl.multiple_of(g * Q_BLOCK, Q_BLOCK), Q_BLOCK)
            rk = pl.ds(pl.multiple_of(g * Q_BLOCK, Q_BLOCK), 2 * Q_BLOCK)
            first = jnp.where(_mod_pow2(g, nb) == 0, 1, 0)
            kb = kd_ref[slot, rk, :]
            for h, qh_ref in enumerate((q0_ref, q1_ref)):
                s = lax.dot_general(qh_ref[slot, rq, :], kb, contract_last,
                                    preferred_element_type=F32)
                s = s + tab_ref[br, first, h]
                s_ref[u, h] = s
                mx_ref[u, h] = jnp.broadcast_to(jnp.max(s, axis=-1, keepdims=True),
                                                (Q_BLOCK, LANES))

    def finish(br, i):
        d, slot = dilation[br], br % 2
        nb = seq_len // d // Q_BLOCK
        for u in range(ATTN_GROUP):
            g = i * ATTN_GROUP + u
            rk = pl.ds(pl.multiple_of(g * Q_BLOCK, Q_BLOCK), 2 * Q_BLOCK)
            outs = []
            for h, vh_ref in enumerate((v0_ref, v1_ref)):
                m = mx_ref[u, h]
                p = jnp.concatenate([jnp.exp2(s_ref[u, h, :, :LANES] - m),
                                     jnp.exp2(s_ref[u, h, :, LANES:] - m)], axis=1)
                outs.append((jnp.dot(p.astype(BF16), vh_ref[slot, rk, :],
                                     preferred_element_type=F32), m))
            (o0, m0), (o1, m1) = outs
            if d <= 4:
                idx = pl.ds(pl.multiple_of(g * Q_BLOCK, Q_BLOCK), Q_BLOCK)
            else:
                res, blk = _div_pow2(g, nb), _mod_pow2(g, nb)
                idx = pl.ds(_mod_pow2(res, 4) * quarter + blk * (Q_BLOCK * d // 4)
                            + _div_pow2(res, 4), Q_BLOCK, stride=d // 4)
            acc_ref[br, idx, :] = jnp.where(low, o0, o1)
            l_ref[br, idx, :] = pltpu.roll(jnp.where(low, o1, o0), HEAD_DIM, 1)
            m_ref[br, idx, :] = jnp.where(low, m0, m1)

    prep(0)
    score(0, 0)
    for br in range(len(BRANCHES)):
        def step(i, carry, br=br):
            finish(br, i - 1)
            score(br, i)
            return carry

        lax.fori_loop(1, n_groups, step, 0)
        if br + 1 < len(BRANCHES):
            prep(br + 1)
            finish(br, n_groups - 1)
            score(br + 1, 0)
        else:
            finish(br, n_groups - 1)

    def merge(c, carry):
        de0 = c * PREP_ROWS
        dil = pl.ds(pl.multiple_of(de0, PREP_ROWS), PREP_ROWS)
        nat = pl.ds(_div_pow2(de0, quarter) + _mod_pow2(de0, quarter) * 4, PREP_ROWS, stride=4)
        ms = (m_ref[0, nat, :], m_ref[1, dil, :], m_ref[2, dil, :])
        m_all = jnp.maximum(jnp.maximum(ms[0], ms[1]), ms[2])
        ws = [jnp.exp2(m - m_all) for m in ms]
        num = (acc_ref[0, nat, :] * ws[0] + acc_ref[1, dil, :] * ws[1] + acc_ref[2, dil, :] * ws[2])
        den = l_ref[0, nat, :] * ws[0] + l_ref[1, dil, :] * ws[1] + l_ref[2, dil, :] * ws[2]
        acc_ref[0, nat, :] = num / den
        return carry

    lax.fori_loop(0, seq_len // PREP_ROWS, merge, 0)

    rows_per = 512

    def fin(i, c):
        sl = pl.ds(pl.multiple_of(i * rows_per, rows_per), rows_per)
        o_ref[0, sl, :] = acc_ref[0, sl, :].astype(o_ref.dtype)
        return c

    lax.fori_loop(0, seq_len // rows_per, fin, 0)


def _attn(q, k, v, r0):
    b, s, _ = q.shape
    n_br = len(BRANCHES)
    blk = pl.BlockSpec((1, s, LANES), lambda i, j: (i, 0, j))
    state = pltpu.VMEM((n_br, s, LANES), F32)
    qd = pltpu.VMEM((2, s, LANES), BF16)
    kd = pltpu.VMEM((2, s + Q_BLOCK, LANES), BF16)
    return pl.pallas_call(
        functools.partial(_attn_kernel, seq_len=s),
        grid=(b, D_ATT // LANES),
        in_specs=[blk, blk, blk, _const_spec(r0.shape)],
        out_specs=blk,
        out_shape=jax.ShapeDtypeStruct((b, s, D_ATT), BF16),
        scratch_shapes=[pltpu.VMEM((n_br, 2, 2, Q_BLOCK, 2 * Q_BLOCK), F32),
                        state, state, state, qd, qd, kd, kd, kd,
                        pltpu.VMEM((ATTN_GROUP, 2, Q_BLOCK, 2 * Q_BLOCK), F32),
                        pltpu.VMEM((ATTN_GROUP, 2, Q_BLOCK, LANES), F32)],
        compiler_params=pltpu.CompilerParams(dimension_semantics=("arbitrary", "arbitrary"),
                                             vmem_limit_bytes=VMEM_LIMIT_ATTN),
        name="attn",
    )(q, k, v, r0)


def _sample_side_steps(b, q_ref, ktn_ref, vtn_ref, logc_ref, kin_ref, vin_ref, kout_ref, vout_ref,
                       att_ref, *, wb, t_new):
    rows = t_new * N_HEADS
    contract_last = (((1,), (1,)), ((), ()))
    lane = lax.broadcasted_iota(jnp.int32, (1, LANES), 1)
    keep = lane < LANES - t_new
    n_tiles = wb // LANES
    st = {}

    def scores():
        shift = lax.bitwise_and(LANES - t_new * b, LANES - 1)
        st["ktn"] = pltpu.roll(ktn_ref[0], shift, 1)
        st["vtn"] = pltpu.roll(vtn_ref[0], shift, 1)
        q_tile = q_ref[0] * (HEAD_DIM ** -0.5)
        per_tile = 8 // t_new
        q = q_tile[0:t_new]
        for i in range(1, per_tile):
            q = jnp.where(lax.rem(b, per_tile) == i, q_tile[i * t_new:(i + 1) * t_new], q)
        qrep = jnp.broadcast_to(q[:, None, :], (t_new, N_HEADS, D_ATT)).reshape(rows, D_ATT)
        row_h = lax.broadcasted_iota(jnp.int32, (rows, D_ATT), 0) % N_HEADS
        col_h = lax.broadcasted_iota(jnp.int32, (rows, D_ATT), 1) // HEAD_DIM
        st["own"] = row_h == col_h
        qbd = jnp.where(st["own"], qrep, 0.0).astype(BF16)
        st["s"] = (jnp.dot(qbd, kin_ref[...].astype(BF16), preferred_element_type=F32)
                   + logc_ref[:, :wb])
        st["sn"] = (jnp.dot(qbd, st["ktn"].astype(BF16), preferred_element_type=F32)
                    + logc_ref[:, wb:])

    def softmax():
        s, sn = st["s"], st["sn"]
        m = jnp.maximum(jnp.max(s, axis=-1, keepdims=True), jnp.max(sn, axis=-1, keepdims=True))
        p = jnp.exp(s - m)
        pn = jnp.exp(sn - m)
        st["l"] = jnp.sum(p, axis=-1, keepdims=True) + jnp.sum(pn, axis=-1, keepdims=True)
        st["p"], st["pn"] = p.astype(BF16), pn.astype(BF16)

    def values():
        o = (lax.dot_general(st["p"], vin_ref[...].astype(BF16), contract_last,
                             preferred_element_type=F32)
             + lax.dot_general(st["pn"], st["vtn"].astype(BF16), contract_last,
                               preferred_element_type=F32))
        o = jnp.where(st["own"], o, 0.0) / st["l"]
        att_ref[0] = jnp.sum(o.reshape(t_new, N_HEADS, D_ATT), axis=1)

    def shift_tiles(src_ref, new_key, dst_ref, j0, j1):
        def run():
            cur = st.get(("cur", new_key))
            if cur is None:
                cur = pltpu.roll(src_ref[:, 0:LANES], LANES - t_new, 1)
            for j in range(j0, j1):
                if j + 1 < n_tiles:
                    nxt = pltpu.roll(src_ref[:, (j + 1) * LANES:(j + 2) * LANES], LANES - t_new, 1)
                else:
                    nxt = pltpu.roll(st[new_key], LANES - t_new, 1)
                dst_ref[:, j * LANES:(j + 1) * LANES] = jnp.where(keep, cur, nxt)
                cur = nxt
            st[("cur", new_key)] = cur
        return run

    per = SIDE_TILES_PER_STEP
    k_steps = [scores] + [shift_tiles(kin_ref, "ktn", kout_ref, j0, min(j0 + per, n_tiles))
                          for j0 in range(0, n_tiles, per)]
    v_steps = [softmax, values] + [shift_tiles(vin_ref, "vtn", vout_ref, j0, min(j0 + per, n_tiles))
                                   for j0 in range(0, n_tiles, per)]
    return k_steps, v_steps


def _ffn_body(x_ref, att_ref, sg_ref, wo_ref, g2_ref, wg_ref, wu_ref, wd_ref, gf_ref,
              y_ref, a_ref, side_steps=()):
    side = list(side_steps)
    assert len(side) < N_FF_CHUNKS
    tm = x_ref.shape[0]
    halves = [slice(0, tm // 2), slice(tm // 2, tm)] if tm >= 2 * FFN_MIN_HALF else [slice(0, tm)]

    def gate_up(rows, c, h):
        cols = slice(c * FF_CHUNK, (c + 1) * FF_CHUNK)
        g = jnp.dot(h, wg_ref[:, cols], preferred_element_type=F32)
        u = jnp.dot(h, wu_ref[:, cols], preferred_element_type=F32)
        a_ref[rows, cols] = (g * jax.nn.sigmoid(g) * u).astype(BF16)

    x1s, hs = [], []
    for rows in halves:
        x1 = (x_ref[rows, :]
              + jnp.dot(att_ref[rows, :].astype(BF16), wo_ref[:D_ATT, :], preferred_element_type=F32)
              + jnp.dot(sg_ref[rows, :], wo_ref[D_ATT:, :], preferred_element_type=F32))
        ms = jnp.mean(x1 * x1, axis=-1, keepdims=True)
        x1s.append(x1)
        hs.append((x1 * lax.rsqrt(ms + EPS) * g2_ref[...]).astype(BF16))
    for rows, h in zip(halves, hs):
        gate_up(rows, 0, h)
    h = hs[0] if len(hs) == 1 else jnp.concatenate(hs, axis=0)
    for c in range(1, N_FF_CHUNKS):
        if side:
            side.pop(0)()
        gate_up(slice(0, tm), c, h)
    for rows, x1 in zip(halves, x1s):
        x2 = x1 + jnp.dot(a_ref[rows, :], wd_ref[...], preferred_element_type=F32)
        ms2 = jnp.mean(x2 * x2, axis=-1, keepdims=True)
        y_ref[rows, :] = x2 * lax.rsqrt(ms2 + EPS) * gf_ref[...]


def _ffn_kernel(*refs):
    _ffn_body(*refs)


def _ffn_side_kernel(x_ref, att_ref, sg_ref, wo_ref, g2_ref, wg_ref, wu_ref, wd_ref, gf_ref,
                     q_ref, ktn_ref, vtn_ref, logc_ref, kt_hbm, vt_hbm,
                     y_ref, atts_ref, kto_hbm, vto_hbm,
                     a_ref, kin_ref, vin_ref, kout_ref, vout_ref, sems, *, wb, t_new):
    i = pl.program_id(0)
    last = pl.num_programs(0) - 1

    def fetch(j):
        return (pltpu.make_async_copy(kt_hbm.at[j], kin_ref, sems.at[0]),
                pltpu.make_async_copy(vt_hbm.at[j], vin_ref, sems.at[1]))

    def flush(j):
        return (pltpu.make_async_copy(kout_ref, kto_hbm.at[j], sems.at[2]),
                pltpu.make_async_copy(vout_ref, vto_hbm.at[j], sems.at[3]))

    @pl.when(i == 0)
    def _():
        for cp in fetch(0):
            cp.start()

    for cp in fetch(i):
        cp.wait()

    @pl.when(i > 0)
    def _():
        for cp in flush(i - 1):
            cp.wait()

    def then_swap(step, which):
        def run():
            step()
            flush(i)[which].start()
            fetch(jnp.minimum(i + 1, last))[which].start()
        return run

    k_steps, v_steps = _sample_side_steps(i, q_ref, ktn_ref, vtn_ref, logc_ref, kin_ref, vin_ref,
                                          kout_ref, vout_ref, atts_ref, wb=wb, t_new=t_new)
    k_steps[-1] = then_swap(k_steps[-1], 0)
    v_steps[-1] = then_swap(v_steps[-1], 1)
    steps = k_steps[:1] + v_steps[:2] + k_steps[1:] + v_steps[2:]
    _ffn_body(x_ref, att_ref, sg_ref, wo_ref, g2_ref, wg_ref, wu_ref, wd_ref, gf_ref,
              y_ref, a_ref, side_steps=steps)

    @pl.when(i == last)
    def _():
        for cp in fetch(i) + flush(i):
            cp.wait()


def _ffn_specs(tm, weights):
    row = lambda shape: pl.BlockSpec(shape, lambda i: (i, 0))
    return ([row((tm, D_MODEL)), row((tm, D_ATT)), row((tm, D_SGU))]
            + [_const_spec(w.shape) for w in weights]), row((tm, D_MODEL))


def _ffn(x, att, sg, weights, *, tm):
    t = x.shape[0]
    in_specs, out_spec = _ffn_specs(tm, weights)
    return pl.pallas_call(
        _ffn_kernel,
        grid=(t // tm,),
        in_specs=in_specs,
        out_specs=out_spec,
        out_shape=jax.ShapeDtypeStruct((t, D_MODEL), F32),
        scratch_shapes=[pltpu.VMEM((tm, D_FF), BF16)],
        compiler_params=pltpu.CompilerParams(dimension_semantics=("arbitrary",),
                                             vmem_limit_bytes=VMEM_LIMIT),
        name="ffn",
    )(x, att, sg, *weights)


def _ffn_with_sample_side(x, att, sg, weights, q_s, ktn, vtn, logc, kt, vt, *, tm, t_new):
    t = x.shape[0]
    bd, _, wb = kt.shape
    assert t // tm == bd
    in_specs, out_spec = _ffn_specs(tm, weights)
    any_spec = pl.BlockSpec(memory_space=pl.ANY)
    window = pltpu.VMEM((D_ATT, wb), F32)
    return pl.pallas_call(
        functools.partial(_ffn_side_kernel, wb=wb, t_new=t_new),
        grid=(bd,),
        in_specs=in_specs + [pl.BlockSpec((1, 8, D_ATT), lambda i: (i // (8 // t_new), 0, 0)),
                             _const_spec(ktn.shape), _const_spec(vtn.shape),
                             _const_spec(logc.shape), any_spec, any_spec],
        out_specs=[out_spec, pl.BlockSpec((1, t_new, D_ATT), lambda i: (i, 0, 0)),
                   any_spec, any_spec],
        out_shape=[jax.ShapeDtypeStruct((t, D_MODEL), F32),
                   jax.ShapeDtypeStruct((bd, t_new, D_ATT), F32),
                   jax.ShapeDtypeStruct(kt.shape, F32), jax.ShapeDtypeStruct(vt.shape, F32)],
        scratch_shapes=[pltpu.VMEM((tm, D_FF), BF16), window, window, window, window,
                        pltpu.SemaphoreType.DMA((4,))],
        compiler_params=pltpu.CompilerParams(dimension_semantics=("arbitrary",),
                                             vmem_limit_bytes=VMEM_LIMIT_FUSED),
        name="ffn_side",
    )(x, att, sg, *weights, q_s, ktn, vtn, logc, kt, vt)


def _rel_bucket(dist):
    max_exact = N_BUCKETS // 2
    df = jnp.maximum(dist, 1).astype(F32)
    large = max_exact + (jnp.log(df / max_exact) / math.log(MAX_DISTANCE / max_exact)
                         * (N_BUCKETS - max_exact)).astype(jnp.int32)
    large = jnp.minimum(large, N_BUCKETS - 1)
    return jnp.where(dist < max_exact, dist, large)


def _branch_bias_reversed(rel_bias, w, d):
    nj = w // d + 1
    dist = (nj - 1 - jnp.arange(nj, dtype=jnp.int32)) * d
    return rel_bias[_rel_bucket(dist)].T.astype(F32)


def _prompt_bias_rows(rev):
    rows = [jnp.pad(r, ((0, 0), (0, 2 * Q_BLOCK - r.shape[1])), constant_values=NEG_INF)
            for r in rev]
    return jnp.stack(rows) * LOG2E


def _sample_bias_table(rev, wb, t_new):
    width = wb + LANES
    per_branch = []
    for (w, d), r in zip(BRANCHES, rev):
        nj = r.shape[1]
        if d > 1:
            fill = jnp.full((N_HEADS, nj, d - 1), NEG_INF, F32)
            r = jnp.concatenate([r[:, :, None], fill], axis=2).reshape(N_HEADS, nj * d)
        rows = []
        for t in range(t_new):
            base = wb + t - (nj - 1) * d
            rows.append(jnp.pad(r, ((0, 0), (base, width - base - nj * d)),
                                constant_values=NEG_INF))
        per_branch.append(jnp.stack(rows))
    x = jnp.stack(per_branch)
    mx = jnp.max(x, axis=0)
    logc = mx + jnp.log(jnp.sum(jnp.exp(x - mx), axis=0))
    return logc.reshape(t_new * N_HEADS, width)


def kernel(x_prompt, x_sample, cache_k_win, cache_v_win, norm1_g, w_in, sgu_ln_g, sgu_ln_b,
           sgu_w, sgu_b, w_out, norm2_g, w_gate, w_up, w_down, rel_bias, final_g):
    depth = w_in.shape[0]
    assert depth == 1
    b, s, _ = x_prompt.shape
    bd, t_new, _ = x_sample.shape
    wb = cache_k_win.shape[2]
    assert bd * t_new == CHUNK and s % (Q_BLOCK * 16) == 0 and wb == WINDOW
    assert (s // Q_BLOCK) % ATTN_GROUP == 0 and s % PREP_ROWS == 0

    l = 0
    w_in_b = w_in[l].astype(BF16)
    g1 = norm1_g[l][None]
    g2 = norm2_g[l][None]
    gf = final_g[None]
    ln_g = sgu_ln_g[l][None]
    ln_b = sgu_ln_b[l][None]

    causal = jnp.tril(jnp.ones((CHUNK, CHUNK), F32))
    wm = sgu_w[l] * causal
    mix_p = wm.astype(BF16)
    mixb_p = jnp.repeat(sgu_b[l].T, HEAD_DIM, axis=1)
    rep = jnp.tile(jnp.eye(t_new, dtype=F32), (bd, 1))
    same_batch = jnp.kron(jnp.eye(bd, dtype=F32), jnp.ones((t_new, t_new), F32))
    mix_s = (jnp.einsum('it,gts,js->gij', rep, wm[:, :t_new, :t_new], rep)
             * same_batch).astype(BF16)
    mixb_s = jnp.tile(mixb_p[:t_new], (bd, 1))

    rev = [_branch_bias_reversed(rel_bias, w, d) for w, d in BRANCHES]

    xp = x_prompt.reshape(b * s, D_MODEL)
    q, k, v, sg, kt_p, vt_p, wo, wg, wu, wd = _proj(
        xp, g1, w_in_b, ln_g, ln_b, mix_p, mixb_p, tm=512, emit_vn=False, emit_t=True, seq_len=s,
        cast=(w_out[l], w_gate[l], w_up[l], w_down[l]))
    att = _attn(q.reshape(b, s, D_ATT), k.reshape(b, s, D_ATT), v.reshape(b, s, D_ATT),
                _prompt_bias_rows(rev))
    nw = min(WINDOW, s)
    new_k_p = kt_p.reshape(1, b, N_HEADS, HEAD_DIM, nw).transpose(0, 1, 4, 2, 3)
    new_v_p = vt_p.reshape(1, b, N_HEADS, HEAD_DIM, nw).transpose(0, 1, 4, 2, 3)

    xs = x_sample.reshape(bd * t_new, D_MODEL)
    qs, _, _, sgs, vn_s, kt_n, vt_n = _proj(xs, g1, w_in_b, ln_g, ln_b, mix_s, mixb_s,
                                            tm=CHUNK, emit_vn=True, emit_t=True, seq_len=CHUNK)
    kt_c = cache_k_win[l].transpose(0, 2, 3, 1).reshape(bd, D_ATT, wb)
    vt_c = cache_v_win[l].transpose(0, 2, 3, 1).reshape(bd, D_ATT, wb)

    logc = _sample_bias_table(rev, wb, t_new)
    weights = (wo, g2, wg, wu, wd, gf)
    y_prompt, att_s, kt_o, vt_o = _ffn_with_sample_side(
        xp, att.reshape(b * s, D_ATT), sg, weights, qs.reshape(bd * t_new // 8, 8, D_ATT),
        kt_n, vt_n, logc, kt_c, vt_c, tm=(b * s) // bd, t_new=t_new)
    y_prompt = y_prompt.reshape(b, s, D_MODEL)
    y_sample = _ffn(xs, att_s.reshape(bd * t_new, D_ATT), sgs, weights,
                    tm=CHUNK).reshape(bd, t_new, D_MODEL)
    new_k_s = kt_o.reshape(1, bd, N_HEADS, HEAD_DIM, wb).transpose(0, 1, 4, 2, 3)
    new_v_s = vt_o.reshape(1, bd, N_HEADS, HEAD_DIM, wb).transpose(0, 1, 4, 2, 3)
    sgu_v = vn_s.reshape(1, bd, t_new, D_SGU)

    return (y_prompt, y_sample, new_k_p, new_v_p, new_k_s, new_v_s, sgu_v)
```

```python
import functools
import math

import jax
import jax.numpy as jnp
from jax import lax
from jax.experimental import pallas as pl
from jax.experimental.pallas import tpu as pltpu

D_MODEL = 1024
N_HEADS = 8
HEAD_DIM = 64
D_ATT = N_HEADS * HEAD_DIM
N_GROUPS = 8
D_SGU = 512
CHUNK = 128
BRANCHES = ((128, 1), (512, 4), (2048, 16))
WINDOW = 2048
Q_BLOCK = 128
N_BUCKETS = 32
MAX_DISTANCE = WINDOW
D_FF = 2816
EPS = 1e-6
NEG_INF = -1e30
LOG2E = math.log2(math.e)

LANES = 128
FF_CHUNK = 256
N_FF_CHUNKS = D_FF // FF_CHUNK
VMEM_LIMIT = 48 * 1024 * 1024
VMEM_LIMIT_FUSED = 60 * 1024 * 1024
VMEM_LIMIT_ATTN = 56 * 1024 * 1024
VMEM_LIMIT_PROJ = 56 * 1024 * 1024
PROJ_TM = 1024
SIDE_TILES_PER_STEP = 8
ATTN_GROUP = 8
PREP_ROWS = 256
BF16_ROWS = 16
FFN_MIN_HALF = 128

F32 = jnp.float32
BF16 = jnp.bfloat16


def _const_spec(shape):
    nd = len(shape)
    return pl.BlockSpec(shape, lambda *_: (0,) * nd, pipeline_mode=pl.Buffered(1))


def _proj_kernel(x_ref, g1_ref, w_ref, lng_ref, lnb_ref, mix_ref, mixb_ref, *refs,
                 tm, emit_vn, emit_t, n_cast):
    cast_in, outs = refs[:n_cast], refs[n_cast:]
    cast_out = outs[len(outs) - n_cast:]
    q_ref, k_ref, v_ref, sg_ref = outs[:4]
    rest = outs[4:len(outs) - n_cast]
    lane = lax.broadcasted_iota(jnp.int32, (1, LANES), 1)
    low = lane < HEAD_DIM

    x = x_ref[...]
    xg = (x * g1_ref[...]).astype(BF16)
    r = lax.rsqrt(jnp.mean(x * x, axis=-1, keepdims=True) + EPS)

    def proj(c0):
        return jnp.dot(xg, w_ref[:, c0:c0 + D_ATT], preferred_element_type=F32) * r

    vg = proj(3 * D_ATT + D_SGU)
    u = proj(3 * D_ATT)
    mu = jnp.mean(vg, axis=-1, keepdims=True)
    dv = vg - mu
    var = jnp.mean(dv * dv, axis=-1, keepdims=True)
    vn = dv * lax.rsqrt(var + EPS) * lng_ref[...] + lnb_ref[...]
    if emit_vn:
        rest[0][...] = vn
    q_ref[...] = proj(0)

    def gating(c0):
        for s in range(D_SGU // LANES):
            cols = slice(s * LANES, (s + 1) * LANES)
            slab = vn[c0:c0 + CHUNK, cols]
            lo = jnp.where(low, slab, 0.0).astype(BF16)
            hi = jnp.where(low, 0.0, slab).astype(BF16)
            gate = (jnp.dot(mix_ref[2 * s], lo, preferred_element_type=F32)
                    + jnp.dot(mix_ref[2 * s + 1], hi, preferred_element_type=F32)
                    + mixb_ref[:, cols])
            sg_ref[c0:c0 + CHUNK, cols] = (u[c0:c0 + CHUNK, cols] * gate).astype(BF16)

    chunks = list(range(0, tm, CHUNK))
    for c0 in chunks[:len(chunks) // 2]:
        gating(c0)
    k = proj(D_ATT)
    k_ref[...] = k
    for src, dst in zip(cast_in, cast_out):
        dst[...] = src[...].astype(BF16)
    for c0 in chunks[len(chunks) // 2:]:
        gating(c0)
    v = proj(2 * D_ATT)
    v_ref[...] = v
    if emit_t:
        kt_ref, vt_ref = rest[-2:]
        kt_ref[0] = k.T
        vt_ref[0] = v.T


def _proj(x, g1, w_in, ln_g, ln_b, mix, mixb, *, tm, emit_vn, emit_t, seq_len=None, cast=()):
    t = x.shape[0]
    n_tiles = t // tm
    row = lambda shape: pl.BlockSpec(shape, lambda i: (i, 0))
    out_shape = [jax.ShapeDtypeStruct((t, D_ATT), F32)] * 3 + [jax.ShapeDtypeStruct((t, D_SGU), BF16)]
    out_specs = [row((tm, D_ATT))] * 3 + [row((tm, D_SGU))]
    if emit_vn:
        out_shape.append(jax.ShapeDtypeStruct((t, D_SGU), F32))
        out_specs.append(row((tm, D_SGU)))
    if emit_t:
        tiles_per_seq = seq_len // tm
        win = min(WINDOW, seq_len)
        first = tiles_per_seq - win // tm

        def t_map(i):
            return (i // tiles_per_seq, 0, jnp.maximum(i % tiles_per_seq - first, 0))

        for _ in range(2):
            out_shape.append(jax.ShapeDtypeStruct((t // seq_len, D_ATT, win), F32))
            out_specs.append(pl.BlockSpec((1, D_ATT, tm), t_map))
    cast_specs = []
    for w in cast:
        steps = max(n for n in range(1, n_tiles + 1)
                    if w.shape[0] % n == 0 and (w.shape[0] // n) % BF16_ROWS == 0)
        spec = pl.BlockSpec((w.shape[0] // steps, w.shape[1]),
                            lambda i, steps=steps: (jnp.minimum(i, steps - 1), 0))
        cast_specs.append(spec)
        out_shape.append(jax.ShapeDtypeStruct(w.shape, BF16))
        out_specs.append(spec)
    kern = functools.partial(_proj_kernel, tm=tm, emit_vn=emit_vn, emit_t=emit_t, n_cast=len(cast))
    return pl.pallas_call(
        kern,
        grid=(n_tiles,),
        in_specs=[row((tm, D_MODEL)), _const_spec(g1.shape), _const_spec(w_in.shape),
                  _const_spec(ln_g.shape), _const_spec(ln_b.shape), _const_spec(mix.shape),
                  _const_spec(mixb.shape)] + cast_specs,
        out_specs=out_specs,
        out_shape=out_shape,
        compiler_params=pltpu.CompilerParams(dimension_semantics=("arbitrary",),
                                             vmem_limit_bytes=VMEM_LIMIT_PROJ),
        name="proj_cast" if cast else "proj",
    )(x, g1, w_in, ln_g, ln_b, mix, mixb, *cast)


def _div_pow2(x, n):
    assert n & (n - 1) == 0
    return lax.shift_right_logical(x, n.bit_length() - 1)


def _mod_pow2(x, n):
    assert n & (n - 1) == 0
    return lax.bitwise_and(x, n - 1)


def _attn_kernel(q_ref, k_ref, v_ref, r0_ref, o_ref, tab_ref, acc_ref, m_ref, l_ref,
                 q0_ref, q1_ref, kd_ref, v0_ref, v1_ref, s_ref, mx_ref, *, seq_len):
    hp = pl.program_id(1)
    lane = lax.broadcasted_iota(jnp.int32, (1, LANES), 1)
    low = lane < HEAD_DIM
    contract_last = (((1,), (1,)), ((), ()))
    neg = NEG_INF * LOG2E
    quarter = seq_len // 4
    lo_f = jnp.where(low, 1.0, 0.0).astype(F32)
    hi_f = 1.0 - lo_f
    q_scale = HEAD_DIM ** -0.5 * LOG2E

    col = lax.broadcasted_iota(jnp.int32, (Q_BLOCK, 2 * Q_BLOCK), 1)
    for br in range(len(BRANCHES)):
        for h in range(2):
            base = jnp.broadcast_to(r0_ref[br, pl.ds(2 * hp + h, 1), :], (Q_BLOCK, 2 * Q_BLOCK))
            t = pltpu.roll(base, 0, 1, stride=1, stride_axis=0)
            tab_ref[br, 0, h] = t
            tab_ref[br, 1, h] = jnp.where(col >= Q_BLOCK, t, neg)
    zeros = jnp.zeros((Q_BLOCK, LANES), BF16)
    for slot in range(2):
        kd_ref[slot, 0:Q_BLOCK, :] = zeros
        v0_ref[slot, 0:Q_BLOCK, :] = zeros
        v1_ref[slot, 0:Q_BLOCK, :] = zeros

    def strided(start, n, d):
        return pl.ds(start, n) if d == 1 else pl.ds(start, n, stride=d)

    dilation = [d for _, d in BRANCHES]
    n_groups = seq_len // Q_BLOCK // ATTN_GROUP

    def prep(br):
        d, slot = dilation[br], br % 2
        sub_len = seq_len // d

        def chunk(c, carry):
            de0 = c * PREP_ROWS
            dst = pl.ds(pl.multiple_of(de0, PREP_ROWS), PREP_ROWS)
            dstp = pl.ds(pl.multiple_of(de0 + Q_BLOCK, Q_BLOCK), PREP_ROWS)
            if d <= 4:
                src = strided(_div_pow2(de0, sub_len) + _mod_pow2(de0, sub_len) * d, PREP_ROWS, d)
                qq, kk, vv = q_ref[0, src, :], k_ref[0, src, :], v_ref[0, src, :]
            else:
                res, m0 = _div_pow2(de0, sub_len), _mod_pow2(de0, sub_len)
                src = pl.ds(_mod_pow2(res, 4) * quarter + m0 * (d // 4) + _div_pow2(res, 4),
                            PREP_ROWS, stride=d // 4)
                qq, kk, vv = acc_ref[2, src, :], m_ref[2, src, :], l_ref[2, src, :]
            if d == 4:
                acc_ref[2, dst, :] = qq
                m_ref[2, dst, :] = kk
                l_ref[2, dst, :] = vv
            q0_ref[slot, dst, :] = (qq * (lo_f * q_scale)).astype(BF16)
            q1_ref[slot, dst, :] = (qq * (hi_f * q_scale)).astype(BF16)
            kd_ref[slot, dstp, :] = kk.astype(BF16)
            v0_ref[slot, dstp, :] = (vv * lo_f + hi_f).astype(BF16)
            v1_ref[slot, dstp, :] = (vv * hi_f + lo_f).astype(BF16)
            return carry

        lax.fori_loop(0, seq_len // PREP_ROWS, chunk, 0)

    def score(br, i):
        slot, nb = br % 2, seq_len // dilation[br] // Q_BLOCK
        for u in range(ATTN_GROUP):
            g = i * ATTN_GROUP + u
            rq = pl.ds(pl.multiple_of(g * Q_BLOCK, Q_BLOCK), Q_BLOCK)
            rk = pl.ds(pl.multiple_of(g * Q_BLOCK, Q_BLOCK), 2 * Q_BLOCK)
            first = jnp.where(_mod_pow2(g, nb) == 0, 1, 0)
            kb = kd_ref[slot, rk, :]
            for h, qh_ref in enumerate((q0_ref, q1_ref)):
                s = lax.dot_general(qh_ref[slot, rq, :], kb, contract_last,
                                    preferred_element_type=F32)
                s = s + tab_ref[br, first, h]
                s_ref[u, h] = s
                mx_ref[u, h] = jnp.broadcast_to(jnp.max(s, axis=-1, keepdims=True),
                                                (Q_BLOCK, LANES))

    def finish(br, i):
        d, slot = dilation[br], br % 2
        nb = seq_len // d // Q_BLOCK
        for u in range(ATTN_GROUP):
            g = i * ATTN_GROUP + u
            rk = pl.ds(pl.multiple_of(g * Q_BLOCK, Q_BLOCK), 2 * Q_BLOCK)
            outs = []
            for h, vh_ref in enumerate((v0_ref, v1_ref)):
                m = mx_ref[u, h]
                p = jnp.concatenate([jnp.exp2(s_ref[u, h, :, :LANES] - m),
                                     jnp.exp2(s_ref[u, h, :, LANES:] - m)], axis=1)
                outs.append((jnp.dot(p.astype(BF16), vh_ref[slot, rk, :],
                                     preferred_element_type=F32), m))
            (o0, m0), (o1, m1) = outs
            if d <= 4:
                idx = pl.ds(pl.multiple_of(g * Q_BLOCK, Q_BLOCK), Q_BLOCK)
            else:
                res, blk = _div_pow2(g, nb), _mod_pow2(g, nb)
                idx = pl.ds(_mod_pow2(res, 4) * quarter + blk * (Q_BLOCK * d // 4)
                            + _div_pow2(res, 4), Q_BLOCK, stride=d // 4)
            acc_ref[br, idx, :] = jnp.where(low, o0, o1)
            l_ref[br, idx, :] = pltpu.roll(jnp.where(low, o1, o0), HEAD_DIM, 1)
            m_ref[br, idx, :] = jnp.where(low, m0, m1)

    prep(0)
    score(0, 0)
    for br in range(len(BRANCHES)):
        def step(i, carry, br=br):
            finish(br, i - 1)
            score(br, i)
            return carry

        lax.fori_loop(1, n_groups, step, 0)
        if br + 1 < len(BRANCHES):
            prep(br + 1)
            finish(br, n_groups - 1)
            score(br + 1, 0)
        else:
            finish(br, n_groups - 1)

    def merge(c, carry):
        de0 = c * PREP_ROWS
        dil = pl.ds(pl.multiple_of(de0, PREP_ROWS), PREP_ROWS)
        nat = pl.ds(_div_pow2(de0, quarter) + _mod_pow2(de0, quarter) * 4, PREP_ROWS, stride=4)
        ms = (m_ref[0, nat, :], m_ref[1, dil, :], m_ref[2, dil, :])
        m_all = jnp.maximum(jnp.maximum(ms[0], ms[1]), ms[2])
        ws = [jnp.exp2(m - m_all) for m in ms]
        num = (acc_ref[0, nat, :] * ws[0] + acc_ref[1, dil, :] * ws[1] + acc_ref[2, dil, :] * ws[2])
        den = l_ref[0, nat, :] * ws[0] + l_ref[1, dil, :] * ws[1] + l_ref[2, dil, :] * ws[2]
        acc_ref[0, nat, :] = num / den
        return carry

    lax.fori_loop(0, seq_len // PREP_ROWS, merge, 0)

    rows_per = 512

    def fin(i, c):
        sl = pl.ds(pl.multiple_of(i * rows_per, rows_per), rows_per)
        o_ref[0, sl, :] = acc_ref[0, sl, :].astype(o_ref.dtype)
        return c

    lax.fori_loop(0, seq_len // rows_per, fin, 0)


def _attn(q, k, v, r0):
    b, s, _ = q.shape
    n_br = len(BRANCHES)
    blk = pl.BlockSpec((1, s, LANES), lambda i, j: (i, 0, j))
    state = pltpu.VMEM((n_br, s, LANES), F32)
    qd = pltpu.VMEM((2, s, LANES), BF16)
    kd = pltpu.VMEM((2, s + Q_BLOCK, LANES), BF16)
    return pl.pallas_call(
        functools.partial(_attn_kernel, seq_len=s),
        grid=(b, D_ATT // LANES),
        in_specs=[blk, blk, blk, _const_spec(r0.shape)],
        out_specs=blk,
        out_shape=jax.ShapeDtypeStruct((b, s, D_ATT), BF16),
        scratch_shapes=[pltpu.VMEM((n_br, 2, 2, Q_BLOCK, 2 * Q_BLOCK), F32),
                        state, state, state, qd, qd, kd, kd, kd,
                        pltpu.VMEM((ATTN_GROUP, 2, Q_BLOCK, 2 * Q_BLOCK), F32),
                        pltpu.VMEM((ATTN_GROUP, 2, Q_BLOCK, LANES), F32)],
        compiler_params=pltpu.CompilerParams(dimension_semantics=("arbitrary", "arbitrary"),
                                             vmem_limit_bytes=VMEM_LIMIT_ATTN),
        name="attn",
    )(q, k, v, r0)


def _sample_side_steps(b, q_ref, ktn_ref, vtn_ref, logc_ref, kin_ref, vin_ref, kout_ref, vout_ref,
                       att_ref, *, wb, t_new):
    rows = t_new * N_HEADS
    contract_last = (((1,), (1,)), ((), ()))
    lane = lax.broadcasted_iota(jnp.int32, (1, LANES), 1)
    keep = lane < LANES - t_new
    n_tiles = wb // LANES
    st = {}

    def scores():
        shift = lax.bitwise_and(LANES - t_new * b, LANES - 1)
        st["ktn"] = pltpu.roll(ktn_ref[0], shift, 1)
        st["vtn"] = pltpu.roll(vtn_ref[0], shift, 1)
        q_tile = q_ref[0] * (HEAD_DIM ** -0.5)
        per_tile = 8 // t_new
        q = q_tile[0:t_new]
        for i in range(1, per_tile):
            q = jnp.where(lax.rem(b, per_tile) == i, q_tile[i * t_new:(i + 1) * t_new], q)
        qrep = jnp.broadcast_to(q[:, None, :], (t_new, N_HEADS, D_ATT)).reshape(rows, D_ATT)
        row_h = lax.broadcasted_iota(jnp.int32, (rows, D_ATT), 0) % N_HEADS
        col_h = lax.broadcasted_iota(jnp.int32, (rows, D_ATT), 1) // HEAD_DIM
        st["own"] = row_h == col_h
        qbd = jnp.where(st["own"], qrep, 0.0).astype(BF16)
        st["s"] = (jnp.dot(qbd, kin_ref[...].astype(BF16), preferred_element_type=F32)
                   + logc_ref[:, :wb])
        st["sn"] = (jnp.dot(qbd, st["ktn"].astype(BF16), preferred_element_type=F32)
                    + logc_ref[:, wb:])

    def softmax():
        s, sn = st["s"], st["sn"]
        m = jnp.maximum(jnp.max(s, axis=-1, keepdims=True), jnp.max(sn, axis=-1, keepdims=True))
        p = jnp.exp(s - m)
        pn = jnp.exp(sn - m)
        st["l"] = jnp.sum(p, axis=-1, keepdims=True) + jnp.sum(pn, axis=-1, keepdims=True)
        st["p"], st["pn"] = p.astype(BF16), pn.astype(BF16)

    def values():
        o = (lax.dot_general(st["p"], vin_ref[...].astype(BF16), contract_last,
                             preferred_element_type=F32)
             + lax.dot_general(st["pn"], st["vtn"].astype(BF16), contract_last,
                               preferred_element_type=F32))
        o = jnp.where(st["own"], o, 0.0) / st["l"]
        att_ref[0] = jnp.sum(o.reshape(t_new, N_HEADS, D_ATT), axis=1)

    def shift_tiles(src_ref, new_key, dst_ref, j0, j1):
        def run():
            cur = st.get(("cur", new_key))
            if cur is None:
                cur = pltpu.roll(src_ref[:, 0:LANES], LANES - t_new, 1)
            for j in range(j0, j1):
                if j + 1 < n_tiles:
                    nxt = pltpu.roll(src_ref[:, (j + 1) * LANES:(j + 2) * LANES], LANES - t_new, 1)
                else:
                    nxt = pltpu.roll(st[new_key], LANES - t_new, 1)
                dst_ref[:, j * LANES:(j + 1) * LANES] = jnp.where(keep, cur, nxt)
                cur = nxt
            st[("cur", new_key)] = cur
        return run

    per = SIDE_TILES_PER_STEP
    k_steps = [scores] + [shift_tiles(kin_ref, "ktn", kout_ref, j0, min(j0 + per, n_tiles))
                          for j0 in range(0, n_tiles, per)]
    v_steps = [softmax, values] + [shift_tiles(vin_ref, "vtn", vout_ref, j0, min(j0 + per, n_tiles))
                                   for j0 in range(0, n_tiles, per)]
    return k_steps, v_steps


def _ffn_body(x_ref, att_ref, sg_ref, wo_ref, g2_ref, wg_ref, wu_ref, wd_ref, gf_ref,
              y_ref, a_ref, side_steps=()):
    side = list(side_steps)
    assert len(side) < N_FF_CHUNKS
    tm = x_ref.shape[0]
    halves = [slice(0, tm // 2), slice(tm // 2, tm)] if tm >= 2 * FFN_MIN_HALF else [slice(0, tm)]

    def gate_up(rows, c, h):
        cols = slice(c * FF_CHUNK, (c + 1) * FF_CHUNK)
        g = jnp.dot(h, wg_ref[:, cols], preferred_element_type=F32)
        u = jnp.dot(h, wu_ref[:, cols], preferred_element_type=F32)
        a_ref[rows, cols] = (g * jax.nn.sigmoid(g) * u).astype(BF16)

    x1s, hs = [], []
    for rows in halves:
        x1 = (x_ref[rows, :]
              + jnp.dot(att_ref[rows, :].astype(BF16), wo_ref[:D_ATT, :], preferred_element_type=F32)
              + jnp.dot(sg_ref[rows, :], wo_ref[D_ATT:, :], preferred_element_type=F32))
        ms = jnp.mean(x1 * x1, axis=-1, keepdims=True)
        x1s.append(x1)
        hs.append((x1 * lax.rsqrt(ms + EPS) * g2_ref[...]).astype(BF16))
    for rows, h in zip(halves, hs):
        gate_up(rows, 0, h)
    h = hs[0] if len(hs) == 1 else jnp.concatenate(hs, axis=0)
    for c in range(1, N_FF_CHUNKS):
        if side:
            side.pop(0)()
        gate_up(slice(0, tm), c, h)
    for rows, x1 in zip(halves, x1s):
        x2 = x1 + jnp.dot(a_ref[rows, :], wd_ref[...], preferred_element_type=F32)
        ms2 = jnp.mean(x2 * x2, axis=-1, keepdims=True)
        y_ref[rows, :] = x2 * lax.rsqrt(ms2 + EPS) * gf_ref[...]


def _ffn_kernel(*refs):
    _ffn_body(*refs)


def _ffn_side_kernel(x_ref, att_ref, sg_ref, wo_ref, g2_ref, wg_ref, wu_ref, wd_ref, gf_ref,
                     q_ref, ktn_ref, vtn_ref, logc_ref, kt_hbm, vt_hbm,
                     y_ref, atts_ref, kto_hbm, vto_hbm,
                     a_ref, kin_ref, vin_ref, kout_ref, vout_ref, sems, *, wb, t_new):
    i = pl.program_id(0)
    last = pl.num_programs(0) - 1

    def fetch(j):
        return (pltpu.make_async_copy(kt_hbm.at[j], kin_ref, sems.at[0]),
                pltpu.make_async_copy(vt_hbm.at[j], vin_ref, sems.at[1]))

    def flush(j):
        return (pltpu.make_async_copy(kout_ref, kto_hbm.at[j], sems.at[2]),
                pltpu.make_async_copy(vout_ref, vto_hbm.at[j], sems.at[3]))

    @pl.when(i == 0)
    def _():
        for cp in fetch(0):
            cp.start()

    for cp in fetch(i):
        cp.wait()

    @pl.when(i > 0)
    def _():
        for cp in flush(i - 1):
            cp.wait()

    def then_swap(step, which):
        def run():
            step()
            flush(i)[which].start()
            fetch(jnp.minimum(i + 1, last))[which].start()
        return run

    k_steps, v_steps = _sample_side_steps(i, q_ref, ktn_ref, vtn_ref, logc_ref, kin_ref, vin_ref,
                                          kout_ref, vout_ref, atts_ref, wb=wb, t_new=t_new)
    k_steps[-1] = then_swap(k_steps[-1], 0)
    v_steps[-1] = then_swap(v_steps[-1], 1)
    steps = k_steps[:1] + v_steps[:2] + k_steps[1:] + v_steps[2:]
    _ffn_body(x_ref, att_ref, sg_ref, wo_ref, g2_ref, wg_ref, wu_ref, wd_ref, gf_ref,
              y_ref, a_ref, side_steps=steps)

    @pl.when(i == last)
    def _():
        for cp in fetch(i) + flush(i):
            cp.wait()


def _ffn_specs(tm, weights):
    row = lambda shape: pl.BlockSpec(shape, lambda i: (i, 0))
    return ([row((tm, D_MODEL)), row((tm, D_ATT)), row((tm, D_SGU))]
            + [_const_spec(w.shape) for w in weights]), row((tm, D_MODEL))


def _ffn(x, att, sg, weights, *, tm):
    t = x.shape[0]
    in_specs, out_spec = _ffn_specs(tm, weights)
    return pl.pallas_call(
        _ffn_kernel,
        grid=(t // tm,),
        in_specs=in_specs,
        out_specs=out_spec,
        out_shape=jax.ShapeDtypeStruct((t, D_MODEL), F32),
        scratch_shapes=[pltpu.VMEM((tm, D_FF), BF16)],
        compiler_params=pltpu.CompilerParams(dimension_semantics=("arbitrary",),
                                             vmem_limit_bytes=VMEM_LIMIT),
        name="ffn",
    )(x, att, sg, *weights)


def _ffn_with_sample_side(x, att, sg, weights, q_s, ktn, vtn, logc, kt, vt, *, tm, t_new):
    t = x.shape[0]
    bd, _, wb = kt.shape
    assert t // tm == bd
    in_specs, out_spec = _ffn_specs(tm, weights)
    any_spec = pl.BlockSpec(memory_space=pl.ANY)
    window = pltpu.VMEM((D_ATT, wb), F32)
    return pl.pallas_call(
        functools.partial(_ffn_side_kernel, wb=wb, t_new=t_new),
        grid=(bd,),
        in_specs=in_specs + [pl.BlockSpec((1, 8, D_ATT), lambda i: (i // (8 // t_new), 0, 0)),
                             _const_spec(ktn.shape), _const_spec(vtn.shape),
                             _const_spec(logc.shape), any_spec, any_spec],
        out_specs=[out_spec, pl.BlockSpec((1, t_new, D_ATT), lambda i: (i, 0, 0)),
                   any_spec, any_spec],
        out_shape=[jax.ShapeDtypeStruct((t, D_MODEL), F32),
                   jax.ShapeDtypeStruct((bd, t_new, D_ATT), F32),
                   jax.ShapeDtypeStruct(kt.shape, F32), jax.ShapeDtypeStruct(vt.shape, F32)],
        scratch_shapes=[pltpu.VMEM((tm, D_FF), BF16), window, window, window, window,
                        pltpu.SemaphoreType.DMA((4,))],
        compiler_params=pltpu.CompilerParams(dimension_semantics=("arbitrary",),
                                             vmem_limit_bytes=VMEM_LIMIT_FUSED),
        name="ffn_side",
    )(x, att, sg, *weights, q_s, ktn, vtn, logc, kt, vt)


def _rel_bucket(dist):
    max_exact = N_BUCKETS // 2
    df = jnp.maximum(dist, 1).astype(F32)
    large = max_exact + (jnp.log(df / max_exact) / math.log(MAX_DISTANCE / max_exact)
                         * (N_BUCKETS - max_exact)).astype(jnp.int32)
    large = jnp.minimum(large, N_BUCKETS - 1)
    return jnp.where(dist < max_exact, dist, large)


def _branch_bias_reversed(rel_bias, w, d):
    nj = w // d + 1
    dist = (nj - 1 - jnp.arange(nj, dtype=jnp.int32)) * d
    return rel_bias[_rel_bucket(dist)].T.astype(F32)


def _prompt_bias_rows(rev):
    rows = [jnp.pad(r, ((0, 0), (0, 2 * Q_BLOCK - r.shape[1])), constant_values=NEG_INF)
            for r in rev]
    return jnp.stack(rows) * LOG2E


def _sample_bias_table(rev, wb, t_new):
    width = wb + LANES
    per_branch = []
    for (w, d), r in zip(BRANCHES, rev):
        nj = r.shape[1]
        if d > 1:
            fill = jnp.full((N_HEADS, nj, d - 1), NEG_INF, F32)
            r = jnp.concatenate([r[:, :, None], fill], axis=2).reshape(N_HEADS, nj * d)
        rows = []
        for t in range(t_new):
            base = wb + t - (nj - 1) * d
            rows.append(jnp.pad(r, ((0, 0), (base, width - base - nj * d)),
                                constant_values=NEG_INF))
        per_branch.append(jnp.stack(rows))
    x = jnp.stack(per_branch)
    mx = jnp.max(x, axis=0)
    logc = mx + jnp.log(jnp.sum(jnp.exp(x - mx), axis=0))
    return logc.reshape(t_new * N_HEADS, width)


def kernel(x_prompt, x_sample, cache_k_win, cache_v_win, norm1_g, w_in, sgu_ln_g, sgu_ln_b,
           sgu_w, sgu_b, w_out, norm2_g, w_gate, w_up, w_down, rel_bias, final_g):
    depth = w_in.shape[0]
    assert depth == 1
    b, s, _ = x_prompt.shape
    bd, t_new, _ = x_sample.shape
    wb = cache_k_win.shape[2]
    assert bd * t_new == CHUNK and s % (Q_BLOCK * 16) == 0 and wb == WINDOW
    assert (s // Q_BLOCK) % ATTN_GROUP == 0 and s % PREP_ROWS == 0

    l = 0
    w_in_b = w_in[l].astype(BF16)
    g1 = norm1_g[l][None]
    g2 = norm2_g[l][None]
    gf = final_g[None]
    ln_g = sgu_ln_g[l][None]
    ln_b = sgu_ln_b[l][None]

    causal = jnp.tril(jnp.ones((CHUNK, CHUNK), F32))
    wm = sgu_w[l] * causal
    mix_p = wm.astype(BF16)
    mixb_p = jnp.repeat(sgu_b[l].T, HEAD_DIM, axis=1)
    rep = jnp.tile(jnp.eye(t_new, dtype=F32), (bd, 1))
    same_batch = jnp.kron(jnp.eye(bd, dtype=F32), jnp.ones((t_new, t_new), F32))
    mix_s = (jnp.einsum('it,gts,js->gij', rep, wm[:, :t_new, :t_new], rep)
             * same_batch).astype(BF16)
    mixb_s = jnp.tile(mixb_p[:t_new], (bd, 1))

    rev = [_branch_bias_reversed(rel_bias, w, d) for w, d in BRANCHES]

    xp = x_prompt.reshape(b * s, D_MODEL)
    q, k, v, sg, kt_p, vt_p, wo, wg, wu, wd = _proj(
        xp, g1, w_in_b, ln_g, ln_b, mix_p, mixb_p, tm=PROJ_TM, emit_vn=False, emit_t=True, seq_len=s,
        cast=(w_out[l], w_gate[l], w_up[l], w_down[l]))
    att = _attn(q.reshape(b, s, D_ATT), k.reshape(b, s, D_ATT), v.reshape(b, s, D_ATT),
                _prompt_bias_rows(rev))
    nw = min(WINDOW, s)
    new_k_p = kt_p.reshape(1, b, N_HEADS, HEAD_DIM, nw).transpose(0, 1, 4, 2, 3)
    new_v_p = vt_p.reshape(1, b, N_HEADS, HEAD_DIM, nw).transpose(0, 1, 4, 2, 3)

    xs = x_sample.reshape(bd * t_new, D_MODEL)
    qs, _, _, sgs, vn_s, kt_n, vt_n = _proj(xs, g1, w_in_b, ln_g, ln_b, mix_s, mixb_s,
                                            tm=CHUNK, emit_vn=True, emit_t=True, seq_len=CHUNK)
    kt_c = cache_k_win[l].transpose(0, 2, 3, 1).reshape(bd, D_ATT, wb)
    vt_c = cache_v_win[l].transpose(0, 2, 3, 1).reshape(bd, D_ATT, wb)

    logc = _sample_bias_table(rev, wb, t_new)
    weights = (wo, g2, wg, wu, wd, gf)
    y_prompt, att_s, kt_o, vt_o = _ffn_with_sample_side(
        xp, att.reshape(b * s, D_ATT), sg, weights, qs.reshape(bd * t_new // 8, 8, D_ATT),
        kt_n, vt_n, logc, kt_c, vt_c, tm=(b * s) // bd, t_new=t_new)
    y_prompt = y_prompt.reshape(b, s, D_MODEL)
    y_sample = _ffn(xs, att_s.reshape(bd * t_new, D_ATT), sgs, weights,
                    tm=CHUNK).reshape(bd, t_new, D_MODEL)
    new_k_s = kt_o.reshape(1, bd, N_HEADS, HEAD_DIM, wb).transpose(0, 1, 4, 2, 3)
    new_v_s = vt_o.reshape(1, bd, N_HEADS, HEAD_DIM, wb).transpose(0, 1, 4, 2, 3)
    sgu_v = vn_s.reshape(1, bd, t_new, D_SGU)

    return (y_prompt, y_sample, new_k_p, new_v_p, new_k_s, new_v_s, sgu_v)
```

```python
import functools
import math

import jax
import jax.numpy as jnp
from jax import lax
from jax.experimental import pallas as pl
from jax.experimental.pallas import tpu as pltpu

D_MODEL = 1024
N_HEADS = 8
HEAD_DIM = 64
D_ATT = N_HEADS * HEAD_DIM
N_GROUPS = 8
D_SGU = 512
CHUNK = 128
BRANCHES = ((128, 1), (512, 4), (2048, 16))
WINDOW = 2048
Q_BLOCK = 128
N_BUCKETS = 32
MAX_DISTANCE = WINDOW
D_FF = 2816
EPS = 1e-6
NEG_INF = -1e30
LOG2E = math.log2(math.e)

LANES = 128
FF_CHUNK = 256
N_FF_CHUNKS = D_FF // FF_CHUNK
VMEM_LIMIT = 48 * 1024 * 1024
VMEM_LIMIT_FUSED = 60 * 1024 * 1024
VMEM_LIMIT_ATTN = 56 * 1024 * 1024
VMEM_LIMIT_PROJ = 56 * 1024 * 1024
PROJ_TM = 1024
SIDE_TILES_PER_STEP = 8
ATTN_GROUP = 8
PREP_ROWS = 256
BF16_ROWS = 16
FFN_MIN_HALF = 128

F32 = jnp.float32
BF16 = jnp.bfloat16


def _const_spec(shape):
    nd = len(shape)
    return pl.BlockSpec(shape, lambda *_: (0,) * nd, pipeline_mode=pl.Buffered(1))


def _proj_kernel(x_ref, g1_ref, w_ref, lng_ref, lnb_ref, mix_ref, mixb_ref, *refs,
                 tm, emit_vn, emit_t, n_cast):
    cast_in, outs = refs[:n_cast], refs[n_cast:]
    cast_out = outs[len(outs) - n_cast:]
    q_ref, k_ref, v_ref, sg_ref = outs[:4]
    rest = outs[4:len(outs) - n_cast]
    lane = lax.broadcasted_iota(jnp.int32, (1, LANES), 1)
    low = lane < HEAD_DIM

    x = x_ref[...]
    xg = (x * g1_ref[...]).astype(BF16)
    r = lax.rsqrt(jnp.mean(x * x, axis=-1, keepdims=True) + EPS)

    def proj(c0):
        return jnp.dot(xg, w_ref[:, c0:c0 + D_ATT], preferred_element_type=F32) * r

    vg = proj(3 * D_ATT + D_SGU)
    u = proj(3 * D_ATT)
    mu = jnp.mean(vg, axis=-1, keepdims=True)
    dv = vg - mu
    var = jnp.mean(dv * dv, axis=-1, keepdims=True)
    vn = dv * lax.rsqrt(var + EPS) * lng_ref[...] + lnb_ref[...]
    if emit_vn:
        rest[0][...] = vn
    q_ref[...] = proj(0)

    def gating(c0):
        for s in range(D_SGU // LANES):
            cols = slice(s * LANES, (s + 1) * LANES)
            slab = vn[c0:c0 + CHUNK, cols]
            lo = jnp.where(low, slab, 0.0).astype(BF16)
            hi = jnp.where(low, 0.0, slab).astype(BF16)
            gate = (jnp.dot(mix_ref[2 * s], lo, preferred_element_type=F32)
                    + jnp.dot(mix_ref[2 * s + 1], hi, preferred_element_type=F32)
                    + mixb_ref[:, cols])
            sg_ref[c0:c0 + CHUNK, cols] = (u[c0:c0 + CHUNK, cols] * gate).astype(BF16)

    chunks = list(range(0, tm, CHUNK))
    for c0 in chunks[:len(chunks) // 2]:
        gating(c0)
    k = proj(D_ATT)
    k_ref[...] = k
    for src, dst in zip(cast_in, cast_out):
        dst[...] = src[...].astype(BF16)
    for c0 in chunks[len(chunks) // 2:]:
        gating(c0)
    v = proj(2 * D_ATT)
    v_ref[...] = v
    if emit_t:
        kt_ref, vt_ref = rest[-2:]
        kt_ref[0] = k.T
        vt_ref[0] = v.T


def _proj(x, g1, w_in, ln_g, ln_b, mix, mixb, *, tm, emit_vn, emit_t, seq_len=None, cast=()):
    t = x.shape[0]
    n_tiles = t // tm
    row = lambda shape: pl.BlockSpec(shape, lambda i: (i, 0))
    out_shape = [jax.ShapeDtypeStruct((t, D_ATT), F32)] * 3 + [jax.ShapeDtypeStruct((t, D_SGU), BF16)]
    out_specs = [row((tm, D_ATT))] * 3 + [row((tm, D_SGU))]
    if emit_vn:
        out_shape.append(jax.ShapeDtypeStruct((t, D_SGU), F32))
        out_specs.append(row((tm, D_SGU)))
    if emit_t:
        tiles_per_seq = seq_len // tm
        win = min(WINDOW, seq_len)
        first = tiles_per_seq - win // tm

        def t_map(i):
            return (i // tiles_per_seq, 0, jnp.maximum(i % tiles_per_seq - first, 0))

        for _ in range(2):
            out_shape.append(jax.ShapeDtypeStruct((t // seq_len, D_ATT, win), F32))
            out_specs.append(pl.BlockSpec((1, D_ATT, tm), t_map))
    cast_specs = []
    for w in cast:
        steps = max(n for n in range(1, n_tiles + 1)
                    if w.shape[0] % n == 0 and (w.shape[0] // n) % BF16_ROWS == 0)
        spec = pl.BlockSpec((w.shape[0] // steps, w.shape[1]),
                            lambda i, steps=steps: (jnp.minimum(i, steps - 1), 0))
        cast_specs.append(spec)
        out_shape.append(jax.ShapeDtypeStruct(w.shape, BF16))
        out_specs.append(spec)
    kern = functools.partial(_proj_kernel, tm=tm, emit_vn=emit_vn, emit_t=emit_t, n_cast=len(cast))
    return pl.pallas_call(
        kern,
        grid=(n_tiles,),
        in_specs=[row((tm, D_MODEL)), _const_spec(g1.shape), _const_spec(w_in.shape),
                  _const_spec(ln_g.shape), _const_spec(ln_b.shape), _const_spec(mix.shape),
                  _const_spec(mixb.shape)] + cast_specs,
        out_specs=out_specs,
        out_shape=out_shape,
        compiler_params=pltpu.CompilerParams(dimension_semantics=("arbitrary",),
                                             vmem_limit_bytes=VMEM_LIMIT_PROJ),
        name="proj_cast" if cast else "proj",
    )(x, g1, w_in, ln_g, ln_b, mix, mixb, *cast)


def _div_pow2(x, n):
    assert n & (n - 1) == 0
    return lax.shift_right_logical(x, n.bit_length() - 1)


def _mod_pow2(x, n):
    assert n & (n - 1) == 0
    return lax.bitwise_and(x, n - 1)


def _attn_kernel(q_ref, k_ref, v_ref, r0_ref, o_ref, tab_ref, acc_ref, m_ref, l_ref,
                 q0_ref, q1_ref, kd_ref, v0_ref, v1_ref, s_ref, mx_ref, *, seq_len):
    hp = pl.program_id(1)
    lane = lax.broadcasted_iota(jnp.int32, (1, LANES), 1)
    low = lane < HEAD_DIM
    contract_last = (((1,), (1,)), ((), ()))
    neg = NEG_INF * LOG2E
    quarter = seq_len // 4
    lo_f = jnp.where(low, 1.0, 0.0).astype(F32)
    hi_f = 1.0 - lo_f
    q_scale = HEAD_DIM ** -0.5 * LOG2E

    col = lax.broadcasted_iota(jnp.int32, (Q_BLOCK, 2 * Q_BLOCK), 1)
    for br in range(len(BRANCHES)):
        for h in range(2):
            base = jnp.broadcast_to(r0_ref[br, pl.ds(2 * hp + h, 1), :], (Q_BLOCK, 2 * Q_BLOCK))
            t = pltpu.roll(base, 0, 1, stride=1, stride_axis=0)
            tab_ref[br, 0, h] = t
            tab_ref[br, 1, h] = jnp.where(col >= Q_BLOCK, t, neg)
    zeros = jnp.zeros((Q_BLOCK, LANES), BF16)
    for slot in range(2):
        kd_ref[slot, 0:Q_BLOCK, :] = zeros
        v0_ref[slot, 0:Q_BLOCK, :] = zeros
        v1_ref[slot, 0:Q_BLOCK, :] = zeros

    def strided(start, n, d):
        return pl.ds(start, n) if d == 1 else pl.ds(start, n, stride=d)

    dilation = [d for _, d in BRANCHES]
    n_groups = seq_len // Q_BLOCK // ATTN_GROUP

    def prep(br):
        d, slot = dilation[br], br % 2
        sub_len = seq_len // d

        def chunk(c, carry):
            de0 = c * PREP_ROWS
            dst = pl.ds(pl.multiple_of(de0, PREP_ROWS), PREP_ROWS)
            dstp = pl.ds(pl.multiple_of(de0 + Q_BLOCK, Q_BLOCK), PREP_ROWS)
            if d <= 4:
                src = strided(_div_pow2(de0, sub_len) + _mod_pow2(de0, sub_len) * d, PREP_ROWS, d)
                qq, kk, vv = q_ref[0, src, :], k_ref[0, src, :], v_ref[0, src, :]
            else:
                res, m0 = _div_pow2(de0, sub_len), _mod_pow2(de0, sub_len)
                src = pl.ds(_mod_pow2(res, 4) * quarter + m0 * (d // 4) + _div_pow2(res, 4),
                            PREP_ROWS, stride=d // 4)
                qq, kk, vv = acc_ref[2, src, :], m_ref[2, src, :], l_ref[2, src, :]
            if d == 4:
                acc_ref[2, dst, :] = qq
                m_ref[2, dst, :] = kk
                l_ref[2, dst, :] = vv
            q0_ref[slot, dst, :] = (qq * (lo_f * q_scale)).astype(BF16)
            q1_ref[slot, dst, :] = (qq * (hi_f * q_scale)).astype(BF16)
            kd_ref[slot, dstp, :] = kk.astype(BF16)
            v0_ref[slot, dstp, :] = (vv * lo_f + hi_f).astype(BF16)
            v1_ref[slot, dstp, :] = (vv * hi_f + lo_f).astype(BF16)
            return carry

        lax.fori_loop(0, seq_len // PREP_ROWS, chunk, 0)

    def score(br, i):
        slot, nb = br % 2, seq_len // dilation[br] // Q_BLOCK
        for u in range(ATTN_GROUP):
            g = i * ATTN_GROUP + u
            rq = pl.ds(pl.multiple_of(g * Q_BLOCK, Q_BLOCK), Q_BLOCK)
            rk = pl.ds(pl.multiple_of(g * Q_BLOCK, Q_BLOCK), 2 * Q_BLOCK)
            first = jnp.where(_mod_pow2(g, nb) == 0, 1, 0)
            kb = kd_ref[slot, rk, :]
            for h, qh_ref in enumerate((q0_ref, q1_ref)):
                s = lax.dot_general(qh_ref[slot, rq, :], kb, contract_last,
                                    preferred_element_type=F32)
                s = s + tab_ref[br, first, h]
                s_ref[u, h] = s
                mx_ref[u, h] = jnp.broadcast_to(jnp.max(s, axis=-1, keepdims=True),
                                                (Q_BLOCK, LANES))

    def finish(br, i):
        d, slot = dilation[br], br % 2
        nb = seq_len // d // Q_BLOCK
        for u in range(ATTN_GROUP):
            g = i * ATTN_GROUP + u
            rk = pl.ds(pl.multiple_of(g * Q_BLOCK, Q_BLOCK), 2 * Q_BLOCK)
            outs = []
            for h, vh_ref in enumerate((v0_ref, v1_ref)):
                m = mx_ref[u, h]
                p = jnp.concatenate([jnp.exp2(s_ref[u, h, :, :LANES] - m),
                                     jnp.exp2(s_ref[u, h, :, LANES:] - m)], axis=1)
                outs.append((jnp.dot(p.astype(BF16), vh_ref[slot, rk, :],
                                     preferred_element_type=F32), m))
            (o0, m0), (o1, m1) = outs
            if d <= 4:
                idx = pl.ds(pl.multiple_of(g * Q_BLOCK, Q_BLOCK), Q_BLOCK)
            else:
                res, blk = _div_pow2(g, nb), _mod_pow2(g, nb)
                idx = pl.ds(_mod_pow2(res, 4) * quarter + blk * (Q_BLOCK * d // 4)
                            + _div_pow2(res, 4), Q_BLOCK, stride=d // 4)
            acc_ref[br, idx, :] = jnp.where(low, o0, o1)
            l_ref[br, idx, :] = pltpu.roll(jnp.where(low, o1, o0), HEAD_DIM, 1)
            m_ref[br, idx, :] = jnp.where(low, m0, m1)

    prep(0)
    score(0, 0)
    for br in range(len(BRANCHES)):
        for i in range(1, n_groups):
            finish(br, i - 1)
            score(br, i)
        if br + 1 < len(BRANCHES):
            prep(br + 1)
            finish(br, n_groups - 1)
            score(br + 1, 0)
        else:
            finish(br, n_groups - 1)

    def merge(c, carry):
        de0 = c * PREP_ROWS
        dil = pl.ds(pl.multiple_of(de0, PREP_ROWS), PREP_ROWS)
        nat = pl.ds(_div_pow2(de0, quarter) + _mod_pow2(de0, quarter) * 4, PREP_ROWS, stride=4)
        ms = (m_ref[0, nat, :], m_ref[1, dil, :], m_ref[2, dil, :])
        m_all = jnp.maximum(jnp.maximum(ms[0], ms[1]), ms[2])
        ws = [jnp.exp2(m - m_all) for m in ms]
        num = (acc_ref[0, nat, :] * ws[0] + acc_ref[1, dil, :] * ws[1] + acc_ref[2, dil, :] * ws[2])
        den = l_ref[0, nat, :] * ws[0] + l_ref[1, dil, :] * ws[1] + l_ref[2, dil, :] * ws[2]
        acc_ref[0, nat, :] = num / den
        return carry

    lax.fori_loop(0, seq_len // PREP_ROWS, merge, 0)

    rows_per = 512

    def fin(i, c):
        sl = pl.ds(pl.multiple_of(i * rows_per, rows_per), rows_per)
        o_ref[0, sl, :] = acc_ref[0, sl, :].astype(o_ref.dtype)
        return c

    lax.fori_loop(0, seq_len // rows_per, fin, 0)


def _attn(q, k, v, r0):
    b, s, _ = q.shape
    n_br = len(BRANCHES)
    blk = pl.BlockSpec((1, s, LANES), lambda i, j: (i, 0, j))
    state = pltpu.VMEM((n_br, s, LANES), F32)
    qd = pltpu.VMEM((2, s, LANES), BF16)
    kd = pltpu.VMEM((2, s + Q_BLOCK, LANES), BF16)
    return pl.pallas_call(
        functools.partial(_attn_kernel, seq_len=s),
        grid=(b, D_ATT // LANES),
        in_specs=[blk, blk, blk, _const_spec(r0.shape)],
        out_specs=blk,
        out_shape=jax.ShapeDtypeStruct((b, s, D_ATT), BF16),
        scratch_shapes=[pltpu.VMEM((n_br, 2, 2, Q_BLOCK, 2 * Q_BLOCK), F32),
                        state, state, state, qd, qd, kd, kd, kd,
                        pltpu.VMEM((ATTN_GROUP, 2, Q_BLOCK, 2 * Q_BLOCK), F32),
                        pltpu.VMEM((ATTN_GROUP, 2, Q_BLOCK, LANES), F32)],
        compiler_params=pltpu.CompilerParams(dimension_semantics=("arbitrary", "arbitrary"),
                                             vmem_limit_bytes=VMEM_LIMIT_ATTN),
        name="attn",
    )(q, k, v, r0)


def _sample_side_steps(b, q_ref, ktn_ref, vtn_ref, logc_ref, kin_ref, vin_ref, kout_ref, vout_ref,
                       att_ref, *, wb, t_new):
    rows = t_new * N_HEADS
    contract_last = (((1,), (1,)), ((), ()))
    lane = lax.broadcasted_iota(jnp.int32, (1, LANES), 1)
    keep = lane < LANES - t_new
    n_tiles = wb // LANES
    st = {}

    def scores():
        shift = lax.bitwise_and(LANES - t_new * b, LANES - 1)
        st["ktn"] = pltpu.roll(ktn_ref[0], shift, 1)
        st["vtn"] = pltpu.roll(vtn_ref[0], shift, 1)
        q_tile = q_ref[0] * (HEAD_DIM ** -0.5)
        per_tile = 8 // t_new
        q = q_tile[0:t_new]
        for i in range(1, per_tile):
            q = jnp.where(lax.rem(b, per_tile) == i, q_tile[i * t_new:(i + 1) * t_new], q)
        qrep = jnp.broadcast_to(q[:, None, :], (t_new, N_HEADS, D_ATT)).reshape(rows, D_ATT)
        row_h = lax.broadcasted_iota(jnp.int32, (rows, D_ATT), 0) % N_HEADS
        col_h = lax.broadcasted_iota(jnp.int32, (rows, D_ATT), 1) // HEAD_DIM
        st["own"] = row_h == col_h
        qbd = jnp.where(st["own"], qrep, 0.0).astype(BF16)
        st["s"] = (jnp.dot(qbd, kin_ref[...].astype(BF16), preferred_element_type=F32)
                   + logc_ref[:, :wb])
        st["sn"] = (jnp.dot(qbd, st["ktn"].astype(BF16), preferred_element_type=F32)
                    + logc_ref[:, wb:])

    def softmax():
        s, sn = st["s"], st["sn"]
        m = jnp.maximum(jnp.max(s, axis=-1, keepdims=True), jnp.max(sn, axis=-1, keepdims=True))
        p = jnp.exp(s - m)
        pn = jnp.exp(sn - m)
        st["l"] = jnp.sum(p, axis=-1, keepdims=True) + jnp.sum(pn, axis=-1, keepdims=True)
        st["p"], st["pn"] = p.astype(BF16), pn.astype(BF16)

    def values():
        o = (lax.dot_general(st["p"], vin_ref[...].astype(BF16), contract_last,
                             preferred_element_type=F32)
             + lax.dot_general(st["pn"], st["vtn"].astype(BF16), contract_last,
                               preferred_element_type=F32))
        o = jnp.where(st["own"], o, 0.0) / st["l"]
        att_ref[0] = jnp.sum(o.reshape(t_new, N_HEADS, D_ATT), axis=1)

    def shift_tiles(src_ref, new_key, dst_ref, j0, j1):
        def run():
            cur = st.get(("cur", new_key))
            if cur is None:
                cur = pltpu.roll(src_ref[:, 0:LANES], LANES - t_new, 1)
            for j in range(j0, j1):
                if j + 1 < n_tiles:
                    nxt = pltpu.roll(src_ref[:, (j + 1) * LANES:(j + 2) * LANES], LANES - t_new, 1)
                else:
                    nxt = pltpu.roll(st[new_key], LANES - t_new, 1)
                dst_ref[:, j * LANES:(j + 1) * LANES] = jnp.where(keep, cur, nxt)
                cur = nxt
            st[("cur", new_key)] = cur
        return run

    per = SIDE_TILES_PER_STEP
    k_steps = [scores] + [shift_tiles(kin_ref, "ktn", kout_ref, j0, min(j0 + per, n_tiles))
                          for j0 in range(0, n_tiles, per)]
    v_steps = [softmax, values] + [shift_tiles(vin_ref, "vtn", vout_ref, j0, min(j0 + per, n_tiles))
                                   for j0 in range(0, n_tiles, per)]
    return k_steps, v_steps


def _ffn_body(x_ref, att_ref, sg_ref, wo_ref, g2_ref, wg_ref, wu_ref, wd_ref, gf_ref,
              y_ref, a_ref, side_steps=()):
    side = list(side_steps)
    assert len(side) < N_FF_CHUNKS
    tm = x_ref.shape[0]
    halves = [slice(0, tm // 2), slice(tm // 2, tm)] if tm >= 2 * FFN_MIN_HALF else [slice(0, tm)]

    def gate_up(rows, c, h):
        cols = slice(c * FF_CHUNK, (c + 1) * FF_CHUNK)
        g = jnp.dot(h, wg_ref[:, cols], preferred_element_type=F32)
        u = jnp.dot(h, wu_ref[:, cols], preferred_element_type=F32)
        a_ref[rows, cols] = (g * jax.nn.sigmoid(g) * u).astype(BF16)

    x1s, hs = [], []
    for rows in halves:
        x1 = (x_ref[rows, :]
              + jnp.dot(att_ref[rows, :].astype(BF16), wo_ref[:D_ATT, :], preferred_element_type=F32)
              + jnp.dot(sg_ref[rows, :], wo_ref[D_ATT:, :], preferred_element_type=F32))
        ms = jnp.mean(x1 * x1, axis=-1, keepdims=True)
        x1s.append(x1)
        hs.append((x1 * lax.rsqrt(ms + EPS) * g2_ref[...]).astype(BF16))
    for rows, h in zip(halves, hs):
        gate_up(rows, 0, h)
    h = hs[0] if len(hs) == 1 else jnp.concatenate(hs, axis=0)
    for c in range(1, N_FF_CHUNKS):
        if side:
            side.pop(0)()
        gate_up(slice(0, tm), c, h)
    for rows, x1 in zip(halves, x1s):
        x2 = x1 + jnp.dot(a_ref[rows, :], wd_ref[...], preferred_element_type=F32)
        ms2 = jnp.mean(x2 * x2, axis=-1, keepdims=True)
        y_ref[rows, :] = x2 * lax.rsqrt(ms2 + EPS) * gf_ref[...]


def _ffn_kernel(*refs):
    _ffn_body(*refs)


def _ffn_side_kernel(x_ref, att_ref, sg_ref, wo_ref, g2_ref, wg_ref, wu_ref, wd_ref, gf_ref,
                     q_ref, ktn_ref, vtn_ref, logc_ref, kt_hbm, vt_hbm,
                     y_ref, atts_ref, kto_hbm, vto_hbm,
                     a_ref, kin_ref, vin_ref, kout_ref, vout_ref, sems, *, wb, t_new):
    i = pl.program_id(0)
    last = pl.num_programs(0) - 1

    def fetch(j):
        return (pltpu.make_async_copy(kt_hbm.at[j], kin_ref, sems.at[0]),
                pltpu.make_async_copy(vt_hbm.at[j], vin_ref, sems.at[1]))

    def flush(j):
        return (pltpu.make_async_copy(kout_ref, kto_hbm.at[j], sems.at[2]),
                pltpu.make_async_copy(vout_ref, vto_hbm.at[j], sems.at[3]))

    @pl.when(i == 0)
    def _():
        for cp in fetch(0):
            cp.start()

    for cp in fetch(i):
        cp.wait()

    @pl.when(i > 0)
    def _():
        for cp in flush(i - 1):
            cp.wait()

    def then_swap(step, which):
        def run():
            step()
            flush(i)[which].start()
            fetch(jnp.minimum(i + 1, last))[which].start()
        return run

    k_steps, v_steps = _sample_side_steps(i, q_ref, ktn_ref, vtn_ref, logc_ref, kin_ref, vin_ref,
                                          kout_ref, vout_ref, atts_ref, wb=wb, t_new=t_new)
    k_steps[-1] = then_swap(k_steps[-1], 0)
    v_steps[-1] = then_swap(v_steps[-1], 1)
    steps = k_steps[:1] + v_steps[:2] + k_steps[1:] + v_steps[2:]
    _ffn_body(x_ref, att_ref, sg_ref, wo_ref, g2_ref, wg_ref, wu_ref, wd_ref, gf_ref,
              y_ref, a_ref, side_steps=steps)

    @pl.when(i == last)
    def _():
        for cp in fetch(i) + flush(i):
            cp.wait()


def _ffn_specs(tm, weights):
    row = lambda shape: pl.BlockSpec(shape, lambda i: (i, 0))
    return ([row((tm, D_MODEL)), row((tm, D_ATT)), row((tm, D_SGU))]
            + [_const_spec(w.shape) for w in weights]), row((tm, D_MODEL))


def _ffn(x, att, sg, weights, *, tm):
    t = x.shape[0]
    in_specs, out_spec = _ffn_specs(tm, weights)
    return pl.pallas_call(
        _ffn_kernel,
        grid=(t // tm,),
        in_specs=in_specs,
        out_specs=out_spec,
        out_shape=jax.ShapeDtypeStruct((t, D_MODEL), F32),
        scratch_shapes=[pltpu.VMEM((tm, D_FF), BF16)],
        compiler_params=pltpu.CompilerParams(dimension_semantics=("arbitrary",),
                                             vmem_limit_bytes=VMEM_LIMIT),
        name="ffn",
    )(x, att, sg, *weights)


def _ffn_with_sample_side(x, att, sg, weights, q_s, ktn, vtn, logc, kt, vt, *, tm, t_new):
    t = x.shape[0]
    bd, _, wb = kt.shape
    assert t // tm == bd
    in_specs, out_spec = _ffn_specs(tm, weights)
    any_spec = pl.BlockSpec(memory_space=pl.ANY)
    window = pltpu.VMEM((D_ATT, wb), F32)
    return pl.pallas_call(
        functools.partial(_ffn_side_kernel, wb=wb, t_new=t_new),
        grid=(bd,),
        in_specs=in_specs + [pl.BlockSpec((1, 8, D_ATT), lambda i: (i // (8 // t_new), 0, 0)),
                             _const_spec(ktn.shape), _const_spec(vtn.shape),
                             _const_spec(logc.shape), any_spec, any_spec],
        out_specs=[out_spec, pl.BlockSpec((1, t_new, D_ATT), lambda i: (i, 0, 0)),
                   any_spec, any_spec],
        out_shape=[jax.ShapeDtypeStruct((t, D_MODEL), F32),
                   jax.ShapeDtypeStruct((bd, t_new, D_ATT), F32),
                   jax.ShapeDtypeStruct(kt.shape, F32), jax.ShapeDtypeStruct(vt.shape, F32)],
        scratch_shapes=[pltpu.VMEM((tm, D_FF), BF16), window, window, window, window,
                        pltpu.SemaphoreType.DMA((4,))],
        compiler_params=pltpu.CompilerParams(dimension_semantics=("arbitrary",),
                                             vmem_limit_bytes=VMEM_LIMIT_FUSED),
        name="ffn_side",
    )(x, att, sg, *weights, q_s, ktn, vtn, logc, kt, vt)


def _rel_bucket(dist):
    max_exact = N_BUCKETS // 2
    df = jnp.maximum(dist, 1).astype(F32)
    large = max_exact + (jnp.log(df / max_exact) / math.log(MAX_DISTANCE / max_exact)
                         * (N_BUCKETS - max_exact)).astype(jnp.int32)
    large = jnp.minimum(large, N_BUCKETS - 1)
    return jnp.where(dist < max_exact, dist, large)


def _branch_bias_reversed(rel_bias, w, d):
    nj = w // d + 1
    dist = (nj - 1 - jnp.arange(nj, dtype=jnp.int32)) * d
    return rel_bias[_rel_bucket(dist)].T.astype(F32)


def _prompt_bias_rows(rev):
    rows = [jnp.pad(r, ((0, 0), (0, 2 * Q_BLOCK - r.shape[1])), constant_values=NEG_INF)
            for r in rev]
    return jnp.stack(rows) * LOG2E


def _sample_bias_table(rev, wb, t_new):
    width = wb + LANES
    per_branch = []
    for (w, d), r in zip(BRANCHES, rev):
        nj = r.shape[1]
        if d > 1:
            fill = jnp.full((N_HEADS, nj, d - 1), NEG_INF, F32)
            r = jnp.concatenate([r[:, :, None], fill], axis=2).reshape(N_HEADS, nj * d)
        rows = []
        for t in range(t_new):
            base = wb + t - (nj - 1) * d
            rows.append(jnp.pad(r, ((0, 0), (base, width - base - nj * d)),
                                constant_values=NEG_INF))
        per_branch.append(jnp.stack(rows))
    x = jnp.stack(per_branch)
    mx = jnp.max(x, axis=0)
    logc = mx + jnp.log(jnp.sum(jnp.exp(x - mx), axis=0))
    return logc.reshape(t_new * N_HEADS, width)


def kernel(x_prompt, x_sample, cache_k_win, cache_v_win, norm1_g, w_in, sgu_ln_g, sgu_ln_b,
           sgu_w, sgu_b, w_out, norm2_g, w_gate, w_up, w_down, rel_bias, final_g):
    depth = w_in.shape[0]
    assert depth == 1
    b, s, _ = x_prompt.shape
    bd, t_new, _ = x_sample.shape
    wb = cache_k_win.shape[2]
    assert bd * t_new == CHUNK and s % (Q_BLOCK * 16) == 0 and wb == WINDOW
    assert (s // Q_BLOCK) % ATTN_GROUP == 0 and s % PREP_ROWS == 0

    l = 0
    w_in_b = w_in[l].astype(BF16)
    g1 = norm1_g[l][None]
    g2 = norm2_g[l][None]
    gf = final_g[None]
    ln_g = sgu_ln_g[l][None]
    ln_b = sgu_ln_b[l][None]

    causal = jnp.tril(jnp.ones((CHUNK, CHUNK), F32))
    wm = sgu_w[l] * causal
    mix_p = wm.astype(BF16)
    mixb_p = jnp.repeat(sgu_b[l].T, HEAD_DIM, axis=1)
    rep = jnp.tile(jnp.eye(t_new, dtype=F32), (bd, 1))
    same_batch = jnp.kron(jnp.eye(bd, dtype=F32), jnp.ones((t_new, t_new), F32))
    mix_s = (jnp.einsum('it,gts,js->gij', rep, wm[:, :t_new, :t_new], rep)
             * same_batch).astype(BF16)
    mixb_s = jnp.tile(mixb_p[:t_new], (bd, 1))

    rev = [_branch_bias_reversed(rel_bias, w, d) for w, d in BRANCHES]

    xp = x_prompt.reshape(b * s, D_MODEL)
    q, k, v, sg, kt_p, vt_p, wo, wg, wu, wd = _proj(
        xp, g1, w_in_b, ln_g, ln_b, mix_p, mixb_p, tm=PROJ_TM, emit_vn=False, emit_t=True, seq_len=s,
        cast=(w_out[l], w_gate[l], w_up[l], w_down[l]))
    att = _attn(q.reshape(b, s, D_ATT), k.reshape(b, s, D_ATT), v.reshape(b, s, D_ATT),
                _prompt_bias_rows(rev))
    nw = min(WINDOW, s)
    new_k_p = kt_p.reshape(1, b, N_HEADS, HEAD_DIM, nw).transpose(0, 1, 4, 2, 3)
    new_v_p = vt_p.reshape(1, b, N_HEADS, HEAD_DIM, nw).transpose(0, 1, 4, 2, 3)

    xs = x_sample.reshape(bd * t_new, D_MODEL)
    qs, _, _, sgs, vn_s, kt_n, vt_n = _proj(xs, g1, w_in_b, ln_g, ln_b, mix_s, mixb_s,
                                            tm=CHUNK, emit_vn=True, emit_t=True, seq_len=CHUNK)
    kt_c = cache_k_win[l].transpose(0, 2, 3, 1).reshape(bd, D_ATT, wb)
    vt_c = cache_v_win[l].transpose(0, 2, 3, 1).reshape(bd, D_ATT, wb)

    logc = _sample_bias_table(rev, wb, t_new)
    weights = (wo, g2, wg, wu, wd, gf)
    y_prompt, att_s, kt_o, vt_o = _ffn_with_sample_side(
        xp, att.reshape(b * s, D_ATT), sg, weights, qs.reshape(bd * t_new // 8, 8, D_ATT),
        kt_n, vt_n, logc, kt_c, vt_c, tm=(b * s) // bd, t_new=t_new)
    y_prompt = y_prompt.reshape(b, s, D_MODEL)
    y_sample = _ffn(xs, att_s.reshape(bd * t_new, D_ATT), sgs, weights,
                    tm=CHUNK).reshape(bd, t_new, D_MODEL)
    new_k_s = kt_o.reshape(1, bd, N_HEADS, HEAD_DIM, wb).transpose(0, 1, 4, 2, 3)
    new_v_s = vt_o.reshape(1, bd, N_HEADS, HEAD_DIM, wb).transpose(0, 1, 4, 2, 3)
    sgu_v = vn_s.reshape(1, bd, t_new, D_SGU)

    return (y_prompt, y_sample, new_k_p, new_v_p, new_k_s, new_v_s, sgu_v)
```

```python
import functools
import math

import jax
import jax.numpy as jnp
from jax import lax
from jax.experimental import pallas as pl
from jax.experimental.pallas import tpu as pltpu

D_MODEL = 1024
N_HEADS = 8
HEAD_DIM = 64
D_ATT = N_HEADS * HEAD_DIM
N_GROUPS = 8
D_SGU = 512
CHUNK = 128
BRANCHES = ((128, 1), (512, 4), (2048, 16))
WINDOW = 2048
Q_BLOCK = 128
N_BUCKETS = 32
MAX_DISTANCE = WINDOW
D_FF = 2816
EPS = 1e-6
NEG_INF = -1e30
LOG2E = math.log2(math.e)

LANES = 128
FF_CHUNK = 256
N_FF_CHUNKS = D_FF // FF_CHUNK
MIB = 1024 * 1024
V7X_VMEM_BYTES = 64 * MIB
VMEM_LIMIT = 48 * MIB
VMEM_LIMIT_FUSED = 60 * MIB
VMEM_LIMIT_ATTN = 56 * MIB
VMEM_LIMIT_PROJ = 56 * MIB
assert max(VMEM_LIMIT, VMEM_LIMIT_FUSED, VMEM_LIMIT_ATTN, VMEM_LIMIT_PROJ) < V7X_VMEM_BYTES
PROJ_TM = 1024
SIDE_TILES_PER_STEP = 8
ATTN_GROUP = 8
PREP_ROWS = 256
BF16_ROWS = 16
FFN_MIN_HALF = 128

F32 = jnp.float32
BF16 = jnp.bfloat16


def _const_spec(shape):
    nd = len(shape)
    return pl.BlockSpec(shape, lambda *_: (0,) * nd, pipeline_mode=pl.Buffered(1))


def _proj_kernel(x_ref, g1_ref, w_ref, lng_ref, lnb_ref, mix_ref, mixb_ref, *refs,
                 tm, emit_vn, emit_t, n_cast):
    cast_in, outs = refs[:n_cast], refs[n_cast:]
    cast_out = outs[len(outs) - n_cast:]
    q_ref, k_ref, v_ref, sg_ref = outs[:4]
    rest = outs[4:len(outs) - n_cast]
    lane = lax.broadcasted_iota(jnp.int32, (1, LANES), 1)
    low = lane < HEAD_DIM

    x = x_ref[...]
    xg = (x * g1_ref[...]).astype(BF16)
    r = lax.rsqrt(jnp.mean(x * x, axis=-1, keepdims=True) + EPS)

    def proj(c0):
        return jnp.dot(xg, w_ref[:, c0:c0 + D_ATT], preferred_element_type=F32) * r

    vg = proj(3 * D_ATT + D_SGU)
    u = proj(3 * D_ATT)
    mu = jnp.mean(vg, axis=-1, keepdims=True)
    dv = vg - mu
    var = jnp.mean(dv * dv, axis=-1, keepdims=True)
    vn = dv * lax.rsqrt(var + EPS) * lng_ref[...] + lnb_ref[...]
    if emit_vn:
        rest[0][...] = vn
    q_ref[...] = proj(0)

    def gating(c0):
        for s in range(D_SGU // LANES):
            cols = slice(s * LANES, (s + 1) * LANES)
            slab = vn[c0:c0 + CHUNK, cols]
            lo = jnp.where(low, slab, 0.0).astype(BF16)
            hi = jnp.where(low, 0.0, slab).astype(BF16)
            gate = (jnp.dot(mix_ref[2 * s], lo, preferred_element_type=F32)
                    + jnp.dot(mix_ref[2 * s + 1], hi, preferred_element_type=F32)
                    + mixb_ref[:, cols])
            sg_ref[c0:c0 + CHUNK, cols] = (u[c0:c0 + CHUNK, cols] * gate).astype(BF16)

    chunks = list(range(0, tm, CHUNK))
    for c0 in chunks[:len(chunks) // 2]:
        gating(c0)
    k = proj(D_ATT)
    k_ref[...] = k
    for src, dst in zip(cast_in, cast_out):
        dst[...] = src[...].astype(BF16)
    for c0 in chunks[len(chunks) // 2:]:
        gating(c0)
    v = proj(2 * D_ATT)
    v_ref[...] = v
    if emit_t:
        kt_ref, vt_ref = rest[-2:]
        kt_ref[0] = k.T
        vt_ref[0] = v.T


def _proj(x, g1, w_in, ln_g, ln_b, mix, mixb, *, tm, emit_vn, emit_t, seq_len=None, cast=()):
    t = x.shape[0]
    n_tiles = t // tm
    row = lambda shape: pl.BlockSpec(shape, lambda i: (i, 0))
    out_shape = [jax.ShapeDtypeStruct((t, D_ATT), F32)] * 3 + [jax.ShapeDtypeStruct((t, D_SGU), BF16)]
    out_specs = [row((tm, D_ATT))] * 3 + [row((tm, D_SGU))]
    if emit_vn:
        out_shape.append(jax.ShapeDtypeStruct((t, D_SGU), F32))
        out_specs.append(row((tm, D_SGU)))
    if emit_t:
        tiles_per_seq = seq_len // tm
        win = min(WINDOW, seq_len)
        first = tiles_per_seq - win // tm

        def t_map(i):
            return (i // tiles_per_seq, 0, jnp.maximum(i % tiles_per_seq - first, 0))

        for _ in range(2):
            out_shape.append(jax.ShapeDtypeStruct((t // seq_len, D_ATT, win), F32))
            out_specs.append(pl.BlockSpec((1, D_ATT, tm), t_map))
    cast_specs = []
    for w in cast:
        steps = max(n for n in range(1, n_tiles + 1)
                    if w.shape[0] % n == 0 and (w.shape[0] // n) % BF16_ROWS == 0)
        spec = pl.BlockSpec((w.shape[0] // steps, w.shape[1]),
                            lambda i, steps=steps: (jnp.minimum(i, steps - 1), 0))
        cast_specs.append(spec)
        out_shape.append(jax.ShapeDtypeStruct(w.shape, BF16))
        out_specs.append(spec)
    kern = functools.partial(_proj_kernel, tm=tm, emit_vn=emit_vn, emit_t=emit_t, n_cast=len(cast))
    return pl.pallas_call(
        kern,
        grid=(n_tiles,),
        in_specs=[row((tm, D_MODEL)), _const_spec(g1.shape), _const_spec(w_in.shape),
                  _const_spec(ln_g.shape), _const_spec(ln_b.shape), _const_spec(mix.shape),
                  _const_spec(mixb.shape)] + cast_specs,
        out_specs=out_specs,
        out_shape=out_shape,
        compiler_params=pltpu.CompilerParams(dimension_semantics=("arbitrary",),
                                             vmem_limit_bytes=VMEM_LIMIT_PROJ),
        name="proj_cast" if cast else "proj",
    )(x, g1, w_in, ln_g, ln_b, mix, mixb, *cast)


def _div_pow2(x, n):
    assert n & (n - 1) == 0
    return lax.shift_right_logical(x, n.bit_length() - 1)


def _mod_pow2(x, n):
    assert n & (n - 1) == 0
    return lax.bitwise_and(x, n - 1)


def _attn_kernel(q_ref, k_ref, v_ref, r0_ref, o_ref, tab_ref, acc_ref, m_ref, l_ref,
                 q0_ref, q1_ref, kd_ref, v0_ref, v1_ref, s_ref, mx_ref, *, seq_len):
    hp = pl.program_id(1)
    lane = lax.broadcasted_iota(jnp.int32, (1, LANES), 1)
    low = lane < HEAD_DIM
    contract_last = (((1,), (1,)), ((), ()))
    neg = NEG_INF * LOG2E
    quarter = seq_len // 4
    lo_f = jnp.where(low, 1.0, 0.0).astype(F32)
    hi_f = 1.0 - lo_f
    q_scale = HEAD_DIM ** -0.5 * LOG2E

    col = lax.broadcasted_iota(jnp.int32, (Q_BLOCK, 2 * Q_BLOCK), 1)
    for br in range(len(BRANCHES)):
        for h in range(2):
            base = jnp.broadcast_to(r0_ref[br, pl.ds(2 * hp + h, 1), :], (Q_BLOCK, 2 * Q_BLOCK))
            t = pltpu.roll(base, 0, 1, stride=1, stride_axis=0)
            tab_ref[br, 0, h] = t
            tab_ref[br, 1, h] = jnp.where(col >= Q_BLOCK, t, neg)
    zeros = jnp.zeros((Q_BLOCK, LANES), BF16)
    for slot in range(2):
        kd_ref[slot, 0:Q_BLOCK, :] = zeros
        v0_ref[slot, 0:Q_BLOCK, :] = zeros
        v1_ref[slot, 0:Q_BLOCK, :] = zeros

    def strided(start, n, d):
        return pl.ds(start, n) if d == 1 else pl.ds(start, n, stride=d)

    dilation = [d for _, d in BRANCHES]
    n_groups = seq_len // Q_BLOCK // ATTN_GROUP

    def prep(br):
        d, slot = dilation[br], br % 2
        sub_len = seq_len // d

        def chunk(c, carry):
            de0 = c * PREP_ROWS
            dst = pl.ds(pl.multiple_of(de0, PREP_ROWS), PREP_ROWS)
            dstp = pl.ds(pl.multiple_of(de0 + Q_BLOCK, Q_BLOCK), PREP_ROWS)
            if d <= 4:
                src = strided(_div_pow2(de0, sub_len) + _mod_pow2(de0, sub_len) * d, PREP_ROWS, d)
                qq, kk, vv = q_ref[0, src, :], k_ref[0, src, :], v_ref[0, src, :]
            else:
                res, m0 = _div_pow2(de0, sub_len), _mod_pow2(de0, sub_len)
                src = pl.ds(_mod_pow2(res, 4) * quarter + m0 * (d // 4) + _div_pow2(res, 4),
                            PREP_ROWS, stride=d // 4)
                qq, kk, vv = acc_ref[2, src, :], m_ref[2, src, :], l_ref[2, src, :]
            if d == 4:
                acc_ref[2, dst, :] = qq
                m_ref[2, dst, :] = kk
                l_ref[2, dst, :] = vv
            q0_ref[slot, dst, :] = (qq * (lo_f * q_scale)).astype(BF16)
            q1_ref[slot, dst, :] = (qq * (hi_f * q_scale)).astype(BF16)
            kd_ref[slot, dstp, :] = kk.astype(BF16)
            v0_ref[slot, dstp, :] = (vv * lo_f + hi_f).astype(BF16)
            v1_ref[slot, dstp, :] = (vv * hi_f + lo_f).astype(BF16)
            return carry

        lax.fori_loop(0, seq_len // PREP_ROWS, chunk, 0, unroll=4)

    def score(br, i):
        slot, nb = br % 2, seq_len // dilation[br] // Q_BLOCK
        for u in range(ATTN_GROUP):
            g = i * ATTN_GROUP + u
            rq = pl.ds(pl.multiple_of(g * Q_BLOCK, Q_BLOCK), Q_BLOCK)
            rk = pl.ds(pl.multiple_of(g * Q_BLOCK, Q_BLOCK), 2 * Q_BLOCK)
            first = jnp.where(_mod_pow2(g, nb) == 0, 1, 0)
            kb = kd_ref[slot, rk, :]
            for h, qh_ref in enumerate((q0_ref, q1_ref)):
                s = lax.dot_general(qh_ref[slot, rq, :], kb, contract_last,
                                    preferred_element_type=F32)
                s = s + tab_ref[br, first, h]
                s_ref[u, h] = s
                mx_ref[u, h] = jnp.broadcast_to(jnp.max(s, axis=-1, keepdims=True),
                                                (Q_BLOCK, LANES))

    def finish(br, i):
        d, slot = dilation[br], br % 2
        nb = seq_len // d // Q_BLOCK
        for u in range(ATTN_GROUP):
            g = i * ATTN_GROUP + u
            rk = pl.ds(pl.multiple_of(g * Q_BLOCK, Q_BLOCK), 2 * Q_BLOCK)
            outs = []
            for h, vh_ref in enumerate((v0_ref, v1_ref)):
                m = mx_ref[u, h]
                p = jnp.concatenate([jnp.exp2(s_ref[u, h, :, :LANES] - m),
                                     jnp.exp2(s_ref[u, h, :, LANES:] - m)], axis=1)
                outs.append((jnp.dot(p.astype(BF16), vh_ref[slot, rk, :],
                                     preferred_element_type=F32), m))
            (o0, m0), (o1, m1) = outs
            if d <= 4:
                idx = pl.ds(pl.multiple_of(g * Q_BLOCK, Q_BLOCK), Q_BLOCK)
            else:
                res, blk = _div_pow2(g, nb), _mod_pow2(g, nb)
                idx = pl.ds(_mod_pow2(res, 4) * quarter + blk * (Q_BLOCK * d // 4)
                            + _div_pow2(res, 4), Q_BLOCK, stride=d // 4)
            acc_ref[br, idx, :] = jnp.where(low, o0, o1)
            l_ref[br, idx, :] = pltpu.roll(jnp.where(low, o1, o0), HEAD_DIM, 1)
            m_ref[br, idx, :] = jnp.where(low, m0, m1)

    prep(0)
    score(0, 0)
    for br in range(len(BRANCHES)):
        for i in range(1, n_groups):
            finish(br, i - 1)
            score(br, i)
        if br + 1 < len(BRANCHES):
            prep(br + 1)
            finish(br, n_groups - 1)
            score(br + 1, 0)
        else:
            finish(br, n_groups - 1)

    def merge(c, carry):
        de0 = c * PREP_ROWS
        dil = pl.ds(pl.multiple_of(de0, PREP_ROWS), PREP_ROWS)
        nat = pl.ds(_div_pow2(de0, quarter) + _mod_pow2(de0, quarter) * 4, PREP_ROWS, stride=4)
        ms = (m_ref[0, nat, :], m_ref[1, dil, :], m_ref[2, dil, :])
        m_all = jnp.maximum(jnp.maximum(ms[0], ms[1]), ms[2])
        ws = [jnp.exp2(m - m_all) for m in ms]
        num = (acc_ref[0, nat, :] * ws[0] + acc_ref[1, dil, :] * ws[1] + acc_ref[2, dil, :] * ws[2])
        den = l_ref[0, nat, :] * ws[0] + l_ref[1, dil, :] * ws[1] + l_ref[2, dil, :] * ws[2]
        acc_ref[0, nat, :] = num / den
        return carry

    lax.fori_loop(0, seq_len // PREP_ROWS, merge, 0)

    rows_per = 512

    def fin(i, c):
        sl = pl.ds(pl.multiple_of(i * rows_per, rows_per), rows_per)
        o_ref[0, sl, :] = acc_ref[0, sl, :].astype(o_ref.dtype)
        return c

    lax.fori_loop(0, seq_len // rows_per, fin, 0)


def _attn(q, k, v, r0):
    b, s, _ = q.shape
    n_br = len(BRANCHES)
    blk = pl.BlockSpec((1, s, LANES), lambda i, j: (i, 0, j))
    state = pltpu.VMEM((n_br, s, LANES), F32)
    qd = pltpu.VMEM((2, s, LANES), BF16)
    kd = pltpu.VMEM((2, s + Q_BLOCK, LANES), BF16)
    return pl.pallas_call(
        functools.partial(_attn_kernel, seq_len=s),
        grid=(b, D_ATT // LANES),
        in_specs=[blk, blk, blk, _const_spec(r0.shape)],
        out_specs=blk,
        out_shape=jax.ShapeDtypeStruct((b, s, D_ATT), BF16),
        scratch_shapes=[pltpu.VMEM((n_br, 2, 2, Q_BLOCK, 2 * Q_BLOCK), F32),
                        state, state, state, qd, qd, kd, kd, kd,
                        pltpu.VMEM((ATTN_GROUP, 2, Q_BLOCK, 2 * Q_BLOCK), F32),
                        pltpu.VMEM((ATTN_GROUP, 2, Q_BLOCK, LANES), F32)],
        compiler_params=pltpu.CompilerParams(dimension_semantics=("arbitrary", "arbitrary"),
                                             vmem_limit_bytes=VMEM_LIMIT_ATTN),
        name="attn",
    )(q, k, v, r0)


def _sample_side_steps(b, q_ref, ktn_ref, vtn_ref, logc_ref, kin_ref, vin_ref, kout_ref, vout_ref,
                       att_ref, *, wb, t_new):
    rows = t_new * N_HEADS
    contract_last = (((1,), (1,)), ((), ()))
    lane = lax.broadcasted_iota(jnp.int32, (1, LANES), 1)
    keep = lane < LANES - t_new
    n_tiles = wb // LANES
    st = {}

    def scores():
        shift = lax.bitwise_and(LANES - t_new * b, LANES - 1)
        st["ktn"] = pltpu.roll(ktn_ref[0], shift, 1)
        st["vtn"] = pltpu.roll(vtn_ref[0], shift, 1)
        q_tile = q_ref[0] * (HEAD_DIM ** -0.5)
        per_tile = 8 // t_new
        q = q_tile[0:t_new]
        for i in range(1, per_tile):
            q = jnp.where(lax.rem(b, per_tile) == i, q_tile[i * t_new:(i + 1) * t_new], q)
        qrep = jnp.broadcast_to(q[:, None, :], (t_new, N_HEADS, D_ATT)).reshape(rows, D_ATT)
        row_h = lax.broadcasted_iota(jnp.int32, (rows, D_ATT), 0) % N_HEADS
        col_h = lax.broadcasted_iota(jnp.int32, (rows, D_ATT), 1) // HEAD_DIM
        st["own"] = row_h == col_h
        qbd = jnp.where(st["own"], qrep, 0.0).astype(BF16)
        st["s"] = (jnp.dot(qbd, kin_ref[...].astype(BF16), preferred_element_type=F32)
                   + logc_ref[:, :wb])
        st["sn"] = (jnp.dot(qbd, st["ktn"].astype(BF16), preferred_element_type=F32)
                    + logc_ref[:, wb:])

    def softmax():
        s, sn = st["s"], st["sn"]
        m = jnp.maximum(jnp.max(s, axis=-1, keepdims=True), jnp.max(sn, axis=-1, keepdims=True))
        p = jnp.exp(s - m)
        pn = jnp.exp(sn - m)
        st["l"] = jnp.sum(p, axis=-1, keepdims=True) + jnp.sum(pn, axis=-1, keepdims=True)
        st["p"], st["pn"] = p.astype(BF16), pn.astype(BF16)

    def values():
        o = (lax.dot_general(st["p"], vin_ref[...].astype(BF16), contract_last,
                             preferred_element_type=F32)
             + lax.dot_general(st["pn"], st["vtn"].astype(BF16), contract_last,
                               preferred_element_type=F32))
        o = jnp.where(st["own"], o, 0.0) / st["l"]
        att_ref[0] = jnp.sum(o.reshape(t_new, N_HEADS, D_ATT), axis=1)

    def shift_tiles(src_ref, new_key, dst_ref, j0, j1):
        def run():
            cur = st.get(("cur", new_key))
            if cur is None:
                cur = pltpu.roll(src_ref[:, 0:LANES], LANES - t_new, 1)
            for j in range(j0, j1):
                if j + 1 < n_tiles:
                    nxt = pltpu.roll(src_ref[:, (j + 1) * LANES:(j + 2) * LANES], LANES - t_new, 1)
                else:
                    nxt = pltpu.roll(st[new_key], LANES - t_new, 1)
                dst_ref[:, j * LANES:(j + 1) * LANES] = jnp.where(keep, cur, nxt)
                cur = nxt
            st[("cur", new_key)] = cur
        return run

    per = SIDE_TILES_PER_STEP
    k_steps = [scores] + [shift_tiles(kin_ref, "ktn", kout_ref, j0, min(j0 + per, n_tiles))
                          for j0 in range(0, n_tiles, per)]
    v_steps = [softmax, values] + [shift_tiles(vin_ref, "vtn", vout_ref, j0, min(j0 + per, n_tiles))
                                   for j0 in range(0, n_tiles, per)]
    return k_steps, v_steps


def _ffn_body(x_ref, att_ref, sg_ref, wo_ref, g2_ref, wg_ref, wu_ref, wd_ref, gf_ref,
              y_ref, a_ref, side_steps=()):
    side = list(side_steps)
    assert len(side) < N_FF_CHUNKS
    tm = x_ref.shape[0]
    halves = [slice(0, tm // 2), slice(tm // 2, tm)] if tm >= 2 * FFN_MIN_HALF else [slice(0, tm)]

    def gate_up(rows, c, h):
        cols = slice(c * FF_CHUNK, (c + 1) * FF_CHUNK)
        g = jnp.dot(h, wg_ref[:, cols], preferred_element_type=F32)
        u = jnp.dot(h, wu_ref[:, cols], preferred_element_type=F32)
        a_ref[rows, cols] = (g * jax.nn.sigmoid(g) * u).astype(BF16)

    x1s, hs = [], []
    for rows in halves:
        x1 = (x_ref[rows, :]
              + jnp.dot(att_ref[rows, :].astype(BF16), wo_ref[:D_ATT, :], preferred_element_type=F32)
              + jnp.dot(sg_ref[rows, :], wo_ref[D_ATT:, :], preferred_element_type=F32))
        ms = jnp.mean(x1 * x1, axis=-1, keepdims=True)
        x1s.append(x1)
        hs.append((x1 * lax.rsqrt(ms + EPS) * g2_ref[...]).astype(BF16))
    for rows, h in zip(halves, hs):
        gate_up(rows, 0, h)
    h = hs[0] if len(hs) == 1 else jnp.concatenate(hs, axis=0)
    for c in range(1, N_FF_CHUNKS):
        if side:
            side.pop(0)()
        gate_up(slice(0, tm), c, h)
    for rows, x1 in zip(halves, x1s):
        x2 = x1 + jnp.dot(a_ref[rows, :], wd_ref[...], preferred_element_type=F32)
        ms2 = jnp.mean(x2 * x2, axis=-1, keepdims=True)
        y_ref[rows, :] = x2 * lax.rsqrt(ms2 + EPS) * gf_ref[...]


def _ffn_kernel(*refs):
    _ffn_body(*refs)


def _ffn_side_kernel(x_ref, att_ref, sg_ref, wo_ref, g2_ref, wg_ref, wu_ref, wd_ref, gf_ref,
                     q_ref, ktn_ref, vtn_ref, logc_ref, kt_hbm, vt_hbm,
                     y_ref, atts_ref, kto_hbm, vto_hbm,
                     a_ref, kin_ref, vin_ref, kout_ref, vout_ref, sems, *, wb, t_new):
    i = pl.program_id(0)
    last = pl.num_programs(0) - 1

    def fetch(j):
        return (pltpu.make_async_copy(kt_hbm.at[j], kin_ref, sems.at[0]),
                pltpu.make_async_copy(vt_hbm.at[j], vin_ref, sems.at[1]))

    def flush(j):
        return (pltpu.make_async_copy(kout_ref, kto_hbm.at[j], sems.at[2]),
                pltpu.make_async_copy(vout_ref, vto_hbm.at[j], sems.at[3]))

    @pl.when(i == 0)
    def _():
        for cp in fetch(0):
            cp.start()

    for cp in fetch(i):
        cp.wait()

    @pl.when(i > 0)
    def _():
        for cp in flush(i - 1):
            cp.wait()

    def then_swap(step, which):
        def run():
            step()
            flush(i)[which].start()
            fetch(jnp.minimum(i + 1, last))[which].start()
        return run

    k_steps, v_steps = _sample_side_steps(i, q_ref, ktn_ref, vtn_ref, logc_ref, kin_ref, vin_ref,
                                          kout_ref, vout_ref, atts_ref, wb=wb, t_new=t_new)
    k_steps[-1] = then_swap(k_steps[-1], 0)
    v_steps[-1] = then_swap(v_steps[-1], 1)
    steps = k_steps + v_steps
    _ffn_body(x_ref, att_ref, sg_ref, wo_ref, g2_ref, wg_ref, wu_ref, wd_ref, gf_ref,
              y_ref, a_ref, side_steps=steps)

    @pl.when(i == last)
    def _():
        for cp in fetch(i) + flush(i):
            cp.wait()


def _ffn_specs(tm, weights):
    row = lambda shape: pl.BlockSpec(shape, lambda i: (i, 0))
    return ([row((tm, D_MODEL)), row((tm, D_ATT)), row((tm, D_SGU))]
            + [_const_spec(w.shape) for w in weights]), row((tm, D_MODEL))


def _ffn(x, att, sg, weights, *, tm):
    t = x.shape[0]
    in_specs, out_spec = _ffn_specs(tm, weights)
    return pl.pallas_call(
        _ffn_kernel,
        grid=(t // tm,),
        in_specs=in_specs,
        out_specs=out_spec,
        out_shape=jax.ShapeDtypeStruct((t, D_MODEL), F32),
        scratch_shapes=[pltpu.VMEM((tm, D_FF), BF16)],
        compiler_params=pltpu.CompilerParams(dimension_semantics=("arbitrary",),
                                             vmem_limit_bytes=VMEM_LIMIT),
        name="ffn",
    )(x, att, sg, *weights)


def _ffn_with_sample_side(x, att, sg, weights, q_s, ktn, vtn, logc, kt, vt, *, tm, t_new):
    t = x.shape[0]
    bd, _, wb = kt.shape
    assert t // tm == bd
    in_specs, out_spec = _ffn_specs(tm, weights)
    any_spec = pl.BlockSpec(memory_space=pl.ANY)
    window = pltpu.VMEM((D_ATT, wb), F32)
    return pl.pallas_call(
        functools.partial(_ffn_side_kernel, wb=wb, t_new=t_new),
        grid=(bd,),
        in_specs=in_specs + [pl.BlockSpec((1, 8, D_ATT), lambda i: (i // (8 // t_new), 0, 0)),
                             _const_spec(ktn.shape), _const_spec(vtn.shape),
                             _const_spec(logc.shape), any_spec, any_spec],
        out_specs=[out_spec, pl.BlockSpec((1, t_new, D_ATT), lambda i: (i, 0, 0)),
                   any_spec, any_spec],
        out_shape=[jax.ShapeDtypeStruct((t, D_MODEL), F32),
                   jax.ShapeDtypeStruct((bd, t_new, D_ATT), F32),
                   jax.ShapeDtypeStruct(kt.shape, F32), jax.ShapeDtypeStruct(vt.shape, F32)],
        scratch_shapes=[pltpu.VMEM((tm, D_FF), BF16), window, window, window, window,
                        pltpu.SemaphoreType.DMA((4,))],
        compiler_params=pltpu.CompilerParams(dimension_semantics=("arbitrary",),
                                             vmem_limit_bytes=VMEM_LIMIT_FUSED),
        name="ffn_side",
    )(x, att, sg, *weights, q_s, ktn, vtn, logc, kt, vt)


def _rel_bucket(dist):
    max_exact = N_BUCKETS // 2
    df = jnp.maximum(dist, 1).astype(F32)
    large = max_exact + (jnp.log(df / max_exact) / math.log(MAX_DISTANCE / max_exact)
                         * (N_BUCKETS - max_exact)).astype(jnp.int32)
    large = jnp.minimum(large, N_BUCKETS - 1)
    return jnp.where(dist < max_exact, dist, large)


def _branch_bias_reversed(rel_bias, w, d):
    nj = w // d + 1
    dist = (nj - 1 - jnp.arange(nj, dtype=jnp.int32)) * d
    return rel_bias[_rel_bucket(dist)].T.astype(F32)


def _prompt_bias_rows(rev):
    rows = [jnp.pad(r, ((0, 0), (0, 2 * Q_BLOCK - r.shape[1])), constant_values=NEG_INF)
            for r in rev]
    return jnp.stack(rows) * LOG2E


def _sample_bias_table(rev, wb, t_new):
    width = wb + LANES
    per_branch = []
    for (w, d), r in zip(BRANCHES, rev):
        nj = r.shape[1]
        if d > 1:
            fill = jnp.full((N_HEADS, nj, d - 1), NEG_INF, F32)
            r = jnp.concatenate([r[:, :, None], fill], axis=2).reshape(N_HEADS, nj * d)
        rows = []
        for t in range(t_new):
            base = wb + t - (nj - 1) * d
            rows.append(jnp.pad(r, ((0, 0), (base, width - base - nj * d)),
                                constant_values=NEG_INF))
        per_branch.append(jnp.stack(rows))
    x = jnp.stack(per_branch)
    mx = jnp.max(x, axis=0)
    logc = mx + jnp.log(jnp.sum(jnp.exp(x - mx), axis=0))
    return logc.reshape(t_new * N_HEADS, width)


def kernel(x_prompt, x_sample, cache_k_win, cache_v_win, norm1_g, w_in, sgu_ln_g, sgu_ln_b,
           sgu_w, sgu_b, w_out, norm2_g, w_gate, w_up, w_down, rel_bias, final_g):
    depth = w_in.shape[0]
    assert depth == 1
    b, s, _ = x_prompt.shape
    bd, t_new, _ = x_sample.shape
    wb = cache_k_win.shape[2]
    assert bd * t_new == CHUNK and s % (Q_BLOCK * 16) == 0 and wb == WINDOW
    assert (s // Q_BLOCK) % ATTN_GROUP == 0 and s % PREP_ROWS == 0

    l = 0
    w_in_b = w_in[l].astype(BF16)
    g1 = norm1_g[l][None]
    g2 = norm2_g[l][None]
    gf = final_g[None]
    ln_g = sgu_ln_g[l][None]
    ln_b = sgu_ln_b[l][None]

    causal = jnp.tril(jnp.ones((CHUNK, CHUNK), F32))
    wm = sgu_w[l] * causal
    mix_p = wm.astype(BF16)
    mixb_p = jnp.repeat(sgu_b[l].T, HEAD_DIM, axis=1)
    rep = jnp.tile(jnp.eye(t_new, dtype=F32), (bd, 1))
    same_batch = jnp.kron(jnp.eye(bd, dtype=F32), jnp.ones((t_new, t_new), F32))
    mix_s = (jnp.einsum('it,gts,js->gij', rep, wm[:, :t_new, :t_new], rep)
             * same_batch).astype(BF16)
    mixb_s = jnp.tile(mixb_p[:t_new], (bd, 1))

    rev = [_branch_bias_reversed(rel_bias, w, d) for w, d in BRANCHES]

    xp = x_prompt.reshape(b * s, D_MODEL)
    q, k, v, sg, kt_p, vt_p, wo, wg, wu, wd = _proj(
        xp, g1, w_in_b, ln_g, ln_b, mix_p, mixb_p, tm=PROJ_TM, emit_vn=False, emit_t=True, seq_len=s,
        cast=(w_out[l], w_gate[l], w_up[l], w_down[l]))
    att = _attn(q.reshape(b, s, D_ATT), k.reshape(b, s, D_ATT), v.reshape(b, s, D_ATT),
                _prompt_bias_rows(rev))
    nw = min(WINDOW, s)
    new_k_p = kt_p.reshape(1, b, N_HEADS, HEAD_DIM, nw).transpose(0, 1, 4, 2, 3)
    new_v_p = vt_p.reshape(1, b, N_HEADS, HEAD_DIM, nw).transpose(0, 1, 4, 2, 3)

    xs = x_sample.reshape(bd * t_new, D_MODEL)
    qs, _, _, sgs, vn_s, kt_n, vt_n = _proj(xs, g1, w_in_b, ln_g, ln_b, mix_s, mixb_s,
                                            tm=CHUNK, emit_vn=True, emit_t=True, seq_len=CHUNK)
    kt_c = cache_k_win[l].transpose(0, 2, 3, 1).reshape(bd, D_ATT, wb)
    vt_c = cache_v_win[l].transpose(0, 2, 3, 1).reshape(bd, D_ATT, wb)

    logc = _sample_bias_table(rev, wb, t_new)
    weights = (wo, g2, wg, wu, wd, gf)
    y_prompt, att_s, kt_o, vt_o = _ffn_with_sample_side(
        xp, att.reshape(b * s, D_ATT), sg, weights, qs.reshape(bd * t_new // 8, 8, D_ATT),
        kt_n, vt_n, logc, kt_c, vt_c, tm=(b * s) // bd, t_new=t_new)
    y_prompt = y_prompt.reshape(b, s, D_MODEL)
    y_sample = _ffn(xs, att_s.reshape(bd * t_new, D_ATT), sgs, weights,
                    tm=CHUNK).reshape(bd, t_new, D_MODEL)
    new_k_s = kt_o.reshape(1, bd, N_HEADS, HEAD_DIM, wb).transpose(0, 1, 4, 2, 3)
    new_v_s = vt_o.reshape(1, bd, N_HEADS, HEAD_DIM, wb).transpose(0, 1, 4, 2, 3)
    sgu_v = vn_s.reshape(1, bd, t_new, D_SGU)

    return (y_prompt, y_sample, new_k_p, new_v_p, new_k_s, new_v_s, sgu_v)
```

```python
import functools
import math

import jax
import jax.numpy as jnp
from jax import lax
from jax.experimental import pallas as pl
from jax.experimental.pallas import tpu as pltpu

D_MODEL = 1024
N_HEADS = 8
HEAD_DIM = 64
D_ATT = N_HEADS * HEAD_DIM
N_GROUPS = 8
D_SGU = 512
CHUNK = 128
BRANCHES = ((128, 1), (512, 4), (2048, 16))
WINDOW = 2048
Q_BLOCK = 128
N_BUCKETS = 32
MAX_DISTANCE = WINDOW
D_FF = 2816
EPS = 1e-6
NEG_INF = -1e30
LOG2E = math.log2(math.e)

LANES = 128
FF_CHUNK = 256
N_FF_CHUNKS = D_FF // FF_CHUNK
MIB = 1024 * 1024
V7X_VMEM_BYTES = 64 * MIB
VMEM_LIMIT = 48 * MIB
VMEM_LIMIT_FUSED = 60 * MIB
VMEM_LIMIT_ATTN = 56 * MIB
VMEM_LIMIT_PROJ = 56 * MIB
assert max(VMEM_LIMIT, VMEM_LIMIT_FUSED, VMEM_LIMIT_ATTN, VMEM_LIMIT_PROJ) < V7X_VMEM_BYTES
PROJ_TM = 1024
SIDE_TILES_PER_STEP = 8
ATTN_GROUP = 8
PREP_ROWS = 256
BF16_ROWS = 16
FFN_MIN_HALF = 128

F32 = jnp.float32
BF16 = jnp.bfloat16


def _const_spec(shape):
    nd = len(shape)
    return pl.BlockSpec(shape, lambda *_: (0,) * nd, pipeline_mode=pl.Buffered(1))


def _proj_kernel(x_ref, g1_ref, w_ref, lng_ref, lnb_ref, mix_ref, mixb_ref, *refs,
                 tm, emit_vn, emit_t, n_cast):
    cast_in, outs = refs[:n_cast], refs[n_cast:]
    cast_out = outs[len(outs) - n_cast:]
    q_ref, k_ref, v_ref, sg_ref = outs[:4]
    rest = outs[4:len(outs) - n_cast]
    lane = lax.broadcasted_iota(jnp.int32, (1, LANES), 1)
    low = lane < HEAD_DIM

    x = x_ref[...]
    xg = (x * g1_ref[...]).astype(BF16)
    r = lax.rsqrt(jnp.mean(x * x, axis=-1, keepdims=True) + EPS)

    def proj(c0):
        return jnp.dot(xg, w_ref[:, c0:c0 + D_ATT], preferred_element_type=F32) * r

    vg = proj(3 * D_ATT + D_SGU)
    u = proj(3 * D_ATT)
    mu = jnp.mean(vg, axis=-1, keepdims=True)
    dv = vg - mu
    var = jnp.mean(dv * dv, axis=-1, keepdims=True)
    vn = dv * lax.rsqrt(var + EPS) * lng_ref[...] + lnb_ref[...]
    if emit_vn:
        rest[0][...] = vn
    q_ref[...] = proj(0)

    def gating(c0):
        for s in range(D_SGU // LANES):
            cols = slice(s * LANES, (s + 1) * LANES)
            slab = vn[c0:c0 + CHUNK, cols]
            lo = jnp.where(low, slab, 0.0).astype(BF16)
            hi = jnp.where(low, 0.0, slab).astype(BF16)
            gate = (jnp.dot(mix_ref[2 * s], lo, preferred_element_type=F32)
                    + jnp.dot(mix_ref[2 * s + 1], hi, preferred_element_type=F32)
                    + mixb_ref[:, cols])
            sg_ref[c0:c0 + CHUNK, cols] = (u[c0:c0 + CHUNK, cols] * gate).astype(BF16)

    chunks = list(range(0, tm, CHUNK))
    for c0 in chunks[:len(chunks) // 2]:
        gating(c0)
    k = proj(D_ATT)
    k_ref[...] = k
    for src, dst in zip(cast_in, cast_out):
        dst[...] = src[...].astype(BF16)
    for c0 in chunks[len(chunks) // 2:]:
        gating(c0)
    v = proj(2 * D_ATT)
    v_ref[...] = v
    if emit_t:
        kt_ref, vt_ref = rest[-2:]
        kt_ref[0] = k.T
        vt_ref[0] = v.T


def _proj(x, g1, w_in, ln_g, ln_b, mix, mixb, *, tm, emit_vn, emit_t, seq_len=None, cast=()):
    t = x.shape[0]
    n_tiles = t // tm
    row = lambda shape: pl.BlockSpec(shape, lambda i: (i, 0))
    out_shape = [jax.ShapeDtypeStruct((t, D_ATT), F32)] * 3 + [jax.ShapeDtypeStruct((t, D_SGU), BF16)]
    out_specs = [row((tm, D_ATT))] * 3 + [row((tm, D_SGU))]
    if emit_vn:
        out_shape.append(jax.ShapeDtypeStruct((t, D_SGU), F32))
        out_specs.append(row((tm, D_SGU)))
    if emit_t:
        tiles_per_seq = seq_len // tm
        win = min(WINDOW, seq_len)
        first = tiles_per_seq - win // tm

        def t_map(i):
            return (i // tiles_per_seq, 0, jnp.maximum(i % tiles_per_seq - first, 0))

        for _ in range(2):
            out_shape.append(jax.ShapeDtypeStruct((t // seq_len, D_ATT, win), F32))
            out_specs.append(pl.BlockSpec((1, D_ATT, tm), t_map))
    cast_specs = []
    for w in cast:
        steps = max(n for n in range(1, n_tiles + 1)
                    if w.shape[0] % n == 0 and (w.shape[0] // n) % BF16_ROWS == 0)
        spec = pl.BlockSpec((w.shape[0] // steps, w.shape[1]),
                            lambda i, steps=steps: (jnp.minimum(i, steps - 1), 0))
        cast_specs.append(spec)
        out_shape.append(jax.ShapeDtypeStruct(w.shape, BF16))
        out_specs.append(spec)
    kern = functools.partial(_proj_kernel, tm=tm, emit_vn=emit_vn, emit_t=emit_t, n_cast=len(cast))
    return pl.pallas_call(
        kern,
        grid=(n_tiles,),
        in_specs=[row((tm, D_MODEL)), _const_spec(g1.shape), _const_spec(w_in.shape),
                  _const_spec(ln_g.shape), _const_spec(ln_b.shape), _const_spec(mix.shape),
                  _const_spec(mixb.shape)] + cast_specs,
        out_specs=out_specs,
        out_shape=out_shape,
        compiler_params=pltpu.CompilerParams(dimension_semantics=("arbitrary",),
                                             vmem_limit_bytes=VMEM_LIMIT_PROJ),
        name="proj_cast" if cast else "proj",
    )(x, g1, w_in, ln_g, ln_b, mix, mixb, *cast)


def _div_pow2(x, n):
    assert n & (n - 1) == 0
    return lax.shift_right_logical(x, n.bit_length() - 1)


def _mod_pow2(x, n):
    assert n & (n - 1) == 0
    return lax.bitwise_and(x, n - 1)


def _attn_kernel(q_ref, k_ref, v_ref, r0_ref, o_ref, tab_ref, acc_ref, m_ref, l_ref,
                 q0_ref, q1_ref, kd_ref, v0_ref, v1_ref, s_ref, mx_ref, *, seq_len):
    hp = pl.program_id(1)
    lane = lax.broadcasted_iota(jnp.int32, (1, LANES), 1)
    low = lane < HEAD_DIM
    contract_last = (((1,), (1,)), ((), ()))
    neg = NEG_INF * LOG2E
    quarter = seq_len // 4
    lo_f = jnp.where(low, 1.0, 0.0).astype(F32)
    hi_f = 1.0 - lo_f
    q_scale = HEAD_DIM ** -0.5 * LOG2E

    col = lax.broadcasted_iota(jnp.int32, (Q_BLOCK, 2 * Q_BLOCK), 1)
    for br in range(len(BRANCHES)):
        for h in range(2):
            base = jnp.broadcast_to(r0_ref[br, pl.ds(2 * hp + h, 1), :], (Q_BLOCK, 2 * Q_BLOCK))
            t = pltpu.roll(base, 0, 1, stride=1, stride_axis=0)
            tab_ref[br, 0, h] = t
            tab_ref[br, 1, h] = jnp.where(col >= Q_BLOCK, t, neg)
    zeros = jnp.zeros((Q_BLOCK, LANES), BF16)
    for slot in range(2):
        kd_ref[slot, 0:Q_BLOCK, :] = zeros
        v0_ref[slot, 0:Q_BLOCK, :] = zeros
        v1_ref[slot, 0:Q_BLOCK, :] = zeros

    def strided(start, n, d):
        return pl.ds(start, n) if d == 1 else pl.ds(start, n, stride=d)

    dilation = [d for _, d in BRANCHES]
    n_groups = seq_len // Q_BLOCK // ATTN_GROUP

    def prep(br):
        d, slot = dilation[br], br % 2
        sub_len = seq_len // d

        def chunk(c, carry):
            de0 = c * PREP_ROWS
            dst = pl.ds(pl.multiple_of(de0, PREP_ROWS), PREP_ROWS)
            dstp = pl.ds(pl.multiple_of(de0 + Q_BLOCK, Q_BLOCK), PREP_ROWS)
            if d <= 4:
                src = strided(_div_pow2(de0, sub_len) + _mod_pow2(de0, sub_len) * d, PREP_ROWS, d)
                qq, kk, vv = q_ref[0, src, :], k_ref[0, src, :], v_ref[0, src, :]
            else:
                res, m0 = _div_pow2(de0, sub_len), _mod_pow2(de0, sub_len)
                src = pl.ds(_mod_pow2(res, 4) * quarter + m0 * (d // 4) + _div_pow2(res, 4),
                            PREP_ROWS, stride=d // 4)
                qq, kk, vv = acc_ref[2, src, :], m_ref[2, src, :], l_ref[2, src, :]
            if d == 4:
                acc_ref[2, dst, :] = qq
                m_ref[2, dst, :] = kk
                l_ref[2, dst, :] = vv
            q0_ref[slot, dst, :] = (qq * (lo_f * q_scale)).astype(BF16)
            q1_ref[slot, dst, :] = (qq * (hi_f * q_scale)).astype(BF16)
            kd_ref[slot, dstp, :] = kk.astype(BF16)
            v0_ref[slot, dstp, :] = (vv * lo_f + hi_f).astype(BF16)
            v1_ref[slot, dstp, :] = (vv * hi_f + lo_f).astype(BF16)
            return carry

        lax.fori_loop(0, seq_len // PREP_ROWS, chunk, 0, unroll=4)

    def score(br, i):
        slot, nb = br % 2, seq_len // dilation[br] // Q_BLOCK
        for u in range(ATTN_GROUP):
            g = i * ATTN_GROUP + u
            rq = pl.ds(pl.multiple_of(g * Q_BLOCK, Q_BLOCK), Q_BLOCK)
            rk = pl.ds(pl.multiple_of(g * Q_BLOCK, Q_BLOCK), 2 * Q_BLOCK)
            first = jnp.where(_mod_pow2(g, nb) == 0, 1, 0)
            kb = kd_ref[slot, rk, :]
            for h, qh_ref in enumerate((q0_ref, q1_ref)):
                s = lax.dot_general(qh_ref[slot, rq, :], kb, contract_last,
                                    preferred_element_type=F32)
                s = s + tab_ref[br, first, h]
                s_ref[u, h] = s
                mx_ref[u, h] = jnp.broadcast_to(jnp.max(s, axis=-1, keepdims=True),
                                                (Q_BLOCK, LANES))

    def finish(br, i):
        d, slot = dilation[br], br % 2
        nb = seq_len // d // Q_BLOCK
        for u in range(ATTN_GROUP):
            g = i * ATTN_GROUP + u
            rk = pl.ds(pl.multiple_of(g * Q_BLOCK, Q_BLOCK), 2 * Q_BLOCK)
            outs = []
            for h, vh_ref in enumerate((v0_ref, v1_ref)):
                m = mx_ref[u, h]
                p = jnp.concatenate([jnp.exp2(s_ref[u, h, :, :LANES] - m),
                                     jnp.exp2(s_ref[u, h, :, LANES:] - m)], axis=1)
                outs.append((jnp.dot(p.astype(BF16), vh_ref[slot, rk, :],
                                     preferred_element_type=F32), m))
            (o0, m0), (o1, m1) = outs
            if d <= 4:
                idx = pl.ds(pl.multiple_of(g * Q_BLOCK, Q_BLOCK), Q_BLOCK)
            else:
                res, blk = _div_pow2(g, nb), _mod_pow2(g, nb)
                idx = pl.ds(_mod_pow2(res, 4) * quarter + blk * (Q_BLOCK * d // 4)
                            + _div_pow2(res, 4), Q_BLOCK, stride=d // 4)
            acc_ref[br, idx, :] = jnp.where(low, o0, o1)
            l_ref[br, idx, :] = pltpu.roll(jnp.where(low, o1, o0), HEAD_DIM, 1)
            m_ref[br, idx, :] = jnp.where(low, m0, m1)

    prep(0)
    score(0, 0)
    for br in range(len(BRANCHES)):
        for i in range(1, n_groups):
            finish(br, i - 1)
            score(br, i)
        if br + 1 < len(BRANCHES):
            prep(br + 1)
            finish(br, n_groups - 1)
            score(br + 1, 0)
        else:
            finish(br, n_groups - 1)

    def merge(c, carry):
        de0 = c * PREP_ROWS
        dil = pl.ds(pl.multiple_of(de0, PREP_ROWS), PREP_ROWS)
        nat = pl.ds(_div_pow2(de0, quarter) + _mod_pow2(de0, quarter) * 4, PREP_ROWS, stride=4)
        ms = (m_ref[0, nat, :], m_ref[1, dil, :], m_ref[2, dil, :])
        m_all = jnp.maximum(jnp.maximum(ms[0], ms[1]), ms[2])
        ws = [jnp.exp2(m - m_all) for m in ms]
        num = (acc_ref[0, nat, :] * ws[0] + acc_ref[1, dil, :] * ws[1] + acc_ref[2, dil, :] * ws[2])
        den = l_ref[0, nat, :] * ws[0] + l_ref[1, dil, :] * ws[1] + l_ref[2, dil, :] * ws[2]
        acc_ref[0, nat, :] = num / den
        return carry

    lax.fori_loop(0, seq_len // PREP_ROWS, merge, 0)

    rows_per = 512

    def fin(i, c):
        sl = pl.ds(pl.multiple_of(i * rows_per, rows_per), rows_per)
        o_ref[0, sl, :] = acc_ref[0, sl, :].astype(o_ref.dtype)
        return c

    lax.fori_loop(0, seq_len // rows_per, fin, 0)


def _attn(q, k, v, r0):
    b, s, _ = q.shape
    n_br = len(BRANCHES)
    blk = pl.BlockSpec((1, s, LANES), lambda i, j: (i, 0, j))
    state = pltpu.VMEM((n_br, s, LANES), F32)
    qd = pltpu.VMEM((2, s, LANES), BF16)
    kd = pltpu.VMEM((2, s + Q_BLOCK, LANES), BF16)
    return pl.pallas_call(
        functools.partial(_attn_kernel, seq_len=s),
        grid=(b, D_ATT // LANES),
        in_specs=[blk, blk, blk, _const_spec(r0.shape)],
        out_specs=blk,
        out_shape=jax.ShapeDtypeStruct((b, s, D_ATT), BF16),
        scratch_shapes=[pltpu.VMEM((n_br, 2, 2, Q_BLOCK, 2 * Q_BLOCK), F32),
                        state, state, state, qd, qd, kd, kd, kd,
                        pltpu.VMEM((ATTN_GROUP, 2, Q_BLOCK, 2 * Q_BLOCK), F32),
                        pltpu.VMEM((ATTN_GROUP, 2, Q_BLOCK, LANES), F32)],
        compiler_params=pltpu.CompilerParams(dimension_semantics=("arbitrary", "arbitrary"),
                                             vmem_limit_bytes=VMEM_LIMIT_ATTN),
        name="attn",
    )(q, k, v, r0)


def _sample_side_steps(b, q_ref, ktn_ref, vtn_ref, logc_ref, kin_ref, vin_ref, kout_ref, vout_ref,
                       att_ref, *, wb, t_new):
    rows = t_new * N_HEADS
    contract_last = (((1,), (1,)), ((), ()))
    lane = lax.broadcasted_iota(jnp.int32, (1, LANES), 1)
    keep = lane < LANES - t_new
    n_tiles = wb // LANES
    st = {}

    def scores():
        shift = lax.bitwise_and(LANES - t_new * b, LANES - 1)
        st["ktn"] = pltpu.roll(ktn_ref[0], shift, 1)
        st["vtn"] = pltpu.roll(vtn_ref[0], shift, 1)
        q_tile = q_ref[0] * (HEAD_DIM ** -0.5)
        per_tile = 8 // t_new
        q = q_tile[0:t_new]
        for i in range(1, per_tile):
            q = jnp.where(lax.rem(b, per_tile) == i, q_tile[i * t_new:(i + 1) * t_new], q)
        qrep = jnp.broadcast_to(q[:, None, :], (t_new, N_HEADS, D_ATT)).reshape(rows, D_ATT)
        row_h = lax.broadcasted_iota(jnp.int32, (rows, D_ATT), 0) % N_HEADS
        col_h = lax.broadcasted_iota(jnp.int32, (rows, D_ATT), 1) // HEAD_DIM
        st["own"] = row_h == col_h
        qbd = jnp.where(st["own"], qrep, 0.0).astype(BF16)
        st["s"] = (jnp.dot(qbd, kin_ref[...].astype(BF16), preferred_element_type=F32)
                   + logc_ref[:, :wb])
        st["sn"] = (jnp.dot(qbd, st["ktn"].astype(BF16), preferred_element_type=F32)
                    + logc_ref[:, wb:])

    def softmax():
        s, sn = st["s"], st["sn"]
        m = jnp.maximum(jnp.max(s, axis=-1, keepdims=True), jnp.max(sn, axis=-1, keepdims=True))
        p = jnp.exp(s - m)
        pn = jnp.exp(sn - m)
        st["l"] = jnp.sum(p, axis=-1, keepdims=True) + jnp.sum(pn, axis=-1, keepdims=True)
        st["p"], st["pn"] = p.astype(BF16), pn.astype(BF16)

    def values():
        o = (lax.dot_general(st["p"], vin_ref[...].astype(BF16), contract_last,
                             preferred_element_type=F32)
             + lax.dot_general(st["pn"], st["vtn"].astype(BF16), contract_last,
                               preferred_element_type=F32))
        o = jnp.where(st["own"], o, 0.0) / st["l"]
        att_ref[0] = jnp.sum(o.reshape(t_new, N_HEADS, D_ATT), axis=1)

    def shift_tiles(src_ref, new_key, dst_ref, j0, j1):
        def run():
            cur = st.get(("cur", new_key))
            if cur is None:
                cur = pltpu.roll(src_ref[:, 0:LANES], LANES - t_new, 1)
            for j in range(j0, j1):
                if j + 1 < n_tiles:
                    nxt = pltpu.roll(src_ref[:, (j + 1) * LANES:(j + 2) * LANES], LANES - t_new, 1)
                else:
                    nxt = pltpu.roll(st[new_key], LANES - t_new, 1)
                dst_ref[:, j * LANES:(j + 1) * LANES] = jnp.where(keep, cur, nxt)
                cur = nxt
            st[("cur", new_key)] = cur
        return run

    per = SIDE_TILES_PER_STEP
    k_steps = [scores] + [shift_tiles(kin_ref, "ktn", kout_ref, j0, min(j0 + per, n_tiles))
                          for j0 in range(0, n_tiles, per)]
    v_steps = [softmax, values] + [shift_tiles(vin_ref, "vtn", vout_ref, j0, min(j0 + per, n_tiles))
                                   for j0 in range(0, n_tiles, per)]
    return k_steps, v_steps


def _ffn_body(x_ref, att_ref, sg_ref, wo_ref, g2_ref, wg_ref, wu_ref, wd_ref, gf_ref,
              y_ref, a_ref, side_steps=()):
    side = list(side_steps)
    assert len(side) < N_FF_CHUNKS
    tm = x_ref.shape[0]
    halves = [slice(0, tm // 2), slice(tm // 2, tm)] if tm >= 2 * FFN_MIN_HALF else [slice(0, tm)]

    def gate_up(rows, c, h):
        cols = slice(c * FF_CHUNK, (c + 1) * FF_CHUNK)
        g = jnp.dot(h, wg_ref[:, cols], preferred_element_type=F32)
        u = jnp.dot(h, wu_ref[:, cols], preferred_element_type=F32)
        a_ref[rows, cols] = (g * jax.nn.sigmoid(g) * u).astype(BF16)

    x1s, hs = [], []
    for rows in halves:
        x1 = (x_ref[rows, :]
              + jnp.dot(att_ref[rows, :].astype(BF16), wo_ref[:D_ATT, :], preferred_element_type=F32)
              + jnp.dot(sg_ref[rows, :], wo_ref[D_ATT:, :], preferred_element_type=F32))
        ms = jnp.mean(x1 * x1, axis=-1, keepdims=True)
        x1s.append(x1)
        hs.append((x1 * lax.rsqrt(ms + EPS) * g2_ref[...]).astype(BF16))
    for rows, h in zip(halves, hs):
        gate_up(rows, 0, h)
    h = hs[0] if len(hs) == 1 else jnp.concatenate(hs, axis=0)
    for c in range(1, N_FF_CHUNKS):
        if side:
            side.pop(0)()
        gate_up(slice(0, tm), c, h)
    for rows, x1 in zip(halves, x1s):
        x2 = x1 + jnp.dot(a_ref[rows, :], wd_ref[...], preferred_element_type=F32)
        ms2 = jnp.mean(x2 * x2, axis=-1, keepdims=True)
        y_ref[rows, :] = x2 * lax.rsqrt(ms2 + EPS) * gf_ref[...]


def _ffn_kernel(*refs):
    _ffn_body(*refs)


def _ffn_side_kernel(x_ref, att_ref, sg_ref, wo_ref, g2_ref, wg_ref, wu_ref, wd_ref, gf_ref,
                     q_ref, ktn_ref, vtn_ref, logc_ref, kt_hbm, vt_hbm,
                     y_ref, atts_ref, kto_hbm, vto_hbm,
                     a_ref, kin_ref, vin_ref, kout_ref, vout_ref, sems, *, wb, t_new):
    i = pl.program_id(0)
    last = pl.num_programs(0) - 1

    def fetch(j):
        return (pltpu.make_async_copy(kt_hbm.at[j], kin_ref, sems.at[0]),
                pltpu.make_async_copy(vt_hbm.at[j], vin_ref, sems.at[1]))

    def flush(j):
        return (pltpu.make_async_copy(kout_ref, kto_hbm.at[j], sems.at[2]),
                pltpu.make_async_copy(vout_ref, vto_hbm.at[j], sems.at[3]))

    @pl.when(i == 0)
    def _():
        for cp in fetch(0):
            cp.start()

    for cp in fetch(i):
        cp.wait()

    @pl.when(i > 0)
    def _():
        for cp in flush(i - 1):
            cp.wait()

    def then_swap(step, which):
        def run():
            step()
            flush(i)[which].start(priority=1)
            fetch(jnp.minimum(i + 1, last))[which].start(priority=1)
        return run

    k_steps, v_steps = _sample_side_steps(i, q_ref, ktn_ref, vtn_ref, logc_ref, kin_ref, vin_ref,
                                          kout_ref, vout_ref, atts_ref, wb=wb, t_new=t_new)
    k_steps[-1] = then_swap(k_steps[-1], 0)
    v_steps[-1] = then_swap(v_steps[-1], 1)
    steps = k_steps + v_steps
    _ffn_body(x_ref, att_ref, sg_ref, wo_ref, g2_ref, wg_ref, wu_ref, wd_ref, gf_ref,
              y_ref, a_ref, side_steps=steps)

    @pl.when(i == last)
    def _():
        for cp in fetch(i) + flush(i):
            cp.wait()


def _ffn_specs(tm, weights):
    row = lambda shape: pl.BlockSpec(shape, lambda i: (i, 0))
    return ([row((tm, D_MODEL)), row((tm, D_ATT)), row((tm, D_SGU))]
            + [_const_spec(w.shape) for w in weights]), row((tm, D_MODEL))


def _ffn(x, att, sg, weights, *, tm):
    t = x.shape[0]
    in_specs, out_spec = _ffn_specs(tm, weights)
    return pl.pallas_call(
        _ffn_kernel,
        grid=(t // tm,),
        in_specs=in_specs,
        out_specs=out_spec,
        out_shape=jax.ShapeDtypeStruct((t, D_MODEL), F32),
        scratch_shapes=[pltpu.VMEM((tm, D_FF), BF16)],
        compiler_params=pltpu.CompilerParams(dimension_semantics=("arbitrary",),
                                             vmem_limit_bytes=VMEM_LIMIT),
        name="ffn",
    )(x, att, sg, *weights)


def _ffn_with_sample_side(x, att, sg, weights, q_s, ktn, vtn, logc, kt, vt, *, tm, t_new):
    t = x.shape[0]
    bd, _, wb = kt.shape
    assert t // tm == bd
    in_specs, out_spec = _ffn_specs(tm, weights)
    any_spec = pl.BlockSpec(memory_space=pl.ANY)
    window = pltpu.VMEM((D_ATT, wb), F32)
    return pl.pallas_call(
        functools.partial(_ffn_side_kernel, wb=wb, t_new=t_new),
        grid=(bd,),
        in_specs=in_specs + [pl.BlockSpec((1, 8, D_ATT), lambda i: (i // (8 // t_new), 0, 0)),
                             _const_spec(ktn.shape), _const_spec(vtn.shape),
                             _const_spec(logc.shape), any_spec, any_spec],
        out_specs=[out_spec, pl.BlockSpec((1, t_new, D_ATT), lambda i: (i, 0, 0)),
                   any_spec, any_spec],
        out_shape=[jax.ShapeDtypeStruct((t, D_MODEL), F32),
                   jax.ShapeDtypeStruct((bd, t_new, D_ATT), F32),
                   jax.ShapeDtypeStruct(kt.shape, F32), jax.ShapeDtypeStruct(vt.shape, F32)],
        scratch_shapes=[pltpu.VMEM((tm, D_FF), BF16), window, window, window, window,
                        pltpu.SemaphoreType.DMA((4,))],
        compiler_params=pltpu.CompilerParams(dimension_semantics=("arbitrary",),
                                             vmem_limit_bytes=VMEM_LIMIT_FUSED),
        name="ffn_side",
    )(x, att, sg, *weights, q_s, ktn, vtn, logc, kt, vt)


def _rel_bucket(dist):
    max_exact = N_BUCKETS // 2
    df = jnp.maximum(dist, 1).astype(F32)
    large = max_exact + (jnp.log(df / max_exact) / math.log(MAX_DISTANCE / max_exact)
                         * (N_BUCKETS - max_exact)).astype(jnp.int32)
    large = jnp.minimum(large, N_BUCKETS - 1)
    return jnp.where(dist < max_exact, dist, large)


def _branch_bias_reversed(rel_bias, w, d):
    nj = w // d + 1
    dist = (nj - 1 - jnp.arange(nj, dtype=jnp.int32)) * d
    return rel_bias[_rel_bucket(dist)].T.astype(F32)


def _prompt_bias_rows(rev):
    rows = [jnp.pad(r, ((0, 0), (0, 2 * Q_BLOCK - r.shape[1])), constant_values=NEG_INF)
            for r in rev]
    return jnp.stack(rows) * LOG2E


def _sample_bias_table(rev, wb, t_new):
    width = wb + LANES
    per_branch = []
    for (w, d), r in zip(BRANCHES, rev):
        nj = r.shape[1]
        if d > 1:
            fill = jnp.full((N_HEADS, nj, d - 1), NEG_INF, F32)
            r = jnp.concatenate([r[:, :, None], fill], axis=2).reshape(N_HEADS, nj * d)
        rows = []
        for t in range(t_new):
            base = wb + t - (nj - 1) * d
            rows.append(jnp.pad(r, ((0, 0), (base, width - base - nj * d)),
                                constant_values=NEG_INF))
        per_branch.append(jnp.stack(rows))
    x = jnp.stack(per_branch)
    mx = jnp.max(x, axis=0)
    logc = mx + jnp.log(jnp.sum(jnp.exp(x - mx), axis=0))
    return logc.reshape(t_new * N_HEADS, width)


def kernel(x_prompt, x_sample, cache_k_win, cache_v_win, norm1_g, w_in, sgu_ln_g, sgu_ln_b,
           sgu_w, sgu_b, w_out, norm2_g, w_gate, w_up, w_down, rel_bias, final_g):
    depth = w_in.shape[0]
    assert depth == 1
    b, s, _ = x_prompt.shape
    bd, t_new, _ = x_sample.shape
    wb = cache_k_win.shape[2]
    assert bd * t_new == CHUNK and s % (Q_BLOCK * 16) == 0 and wb == WINDOW
    assert (s // Q_BLOCK) % ATTN_GROUP == 0 and s % PREP_ROWS == 0

    l = 0
    w_in_b = w_in[l].astype(BF16)
    g1 = norm1_g[l][None]
    g2 = norm2_g[l][None]
    gf = final_g[None]
    ln_g = sgu_ln_g[l][None]
    ln_b = sgu_ln_b[l][None]

    causal = jnp.tril(jnp.ones((CHUNK, CHUNK), F32))
    wm = sgu_w[l] * causal
    mix_p = wm.astype(BF16)
    mixb_p = jnp.repeat(sgu_b[l].T, HEAD_DIM, axis=1)
    rep = jnp.tile(jnp.eye(t_new, dtype=F32), (bd, 1))
    same_batch = jnp.kron(jnp.eye(bd, dtype=F32), jnp.ones((t_new, t_new), F32))
    mix_s = (jnp.einsum('it,gts,js->gij', rep, wm[:, :t_new, :t_new], rep)
             * same_batch).astype(BF16)
    mixb_s = jnp.tile(mixb_p[:t_new], (bd, 1))

    rev = [_branch_bias_reversed(rel_bias, w, d) for w, d in BRANCHES]

    xp = x_prompt.reshape(b * s, D_MODEL)
    q, k, v, sg, kt_p, vt_p, wo, wg, wu, wd = _proj(
        xp, g1, w_in_b, ln_g, ln_b, mix_p, mixb_p, tm=PROJ_TM, emit_vn=False, emit_t=True, seq_len=s,
        cast=(w_out[l], w_gate[l], w_up[l], w_down[l]))
    att = _attn(q.reshape(b, s, D_ATT), k.reshape(b, s, D_ATT), v.reshape(b, s, D_ATT),
                _prompt_bias_rows(rev))
    nw = min(WINDOW, s)
    new_k_p = kt_p.reshape(1, b, N_HEADS, HEAD_DIM, nw).transpose(0, 1, 4, 2, 3)
    new_v_p = vt_p.reshape(1, b, N_HEADS, HEAD_DIM, nw).transpose(0, 1, 4, 2, 3)

    xs = x_sample.reshape(bd * t_new, D_MODEL)
    qs, _, _, sgs, vn_s, kt_n, vt_n = _proj(xs, g1, w_in_b, ln_g, ln_b, mix_s, mixb_s,
                                            tm=CHUNK, emit_vn=True, emit_t=True, seq_len=CHUNK)
    kt_c = cache_k_win[l].transpose(0, 2, 3, 1).reshape(bd, D_ATT, wb)
    vt_c = cache_v_win[l].transpose(0, 2, 3, 1).reshape(bd, D_ATT, wb)

    logc = _sample_bias_table(rev, wb, t_new)
    weights = (wo, g2, wg, wu, wd, gf)
    y_prompt, att_s, kt_o, vt_o = _ffn_with_sample_side(
        xp, att.reshape(b * s, D_ATT), sg, weights, qs.reshape(bd * t_new // 8, 8, D_ATT),
        kt_n, vt_n, logc, kt_c, vt_c, tm=(b * s) // bd, t_new=t_new)
    y_prompt = y_prompt.reshape(b, s, D_MODEL)
    y_sample = _ffn(xs, att_s.reshape(bd * t_new, D_ATT), sgs, weights,
                    tm=CHUNK).reshape(bd, t_new, D_MODEL)
    new_k_s = kt_o.reshape(1, bd, N_HEADS, HEAD_DIM, wb).transpose(0, 1, 4, 2, 3)
    new_v_s = vt_o.reshape(1, bd, N_HEADS, HEAD_DIM, wb).transpose(0, 1, 4, 2, 3)
    sgu_v = vn_s.reshape(1, bd, t_new, D_SGU)

    return (y_prompt, y_sample, new_k_p, new_v_p, new_k_s, new_v_s, sgu_v)
```

```python
import functools
import math

import jax
import jax.numpy as jnp
from jax import lax
from jax.experimental import pallas as pl
from jax.experimental.pallas import tpu as pltpu

D_MODEL = 1024
N_HEADS = 8
HEAD_DIM = 64
D_ATT = N_HEADS * HEAD_DIM
N_GROUPS = 8
D_SGU = 512
CHUNK = 128
BRANCHES = ((128, 1), (512, 4), (2048, 16))
WINDOW = 2048
Q_BLOCK = 128
N_BUCKETS = 32
MAX_DISTANCE = WINDOW
D_FF = 2816
EPS = 1e-6
NEG_INF = -1e30
LOG2E = math.log2(math.e)

LANES = 128
FF_CHUNK = 256
N_FF_CHUNKS = D_FF // FF_CHUNK
MIB = 1024 * 1024
V7X_VMEM_BYTES = 64 * MIB
VMEM_LIMIT = 48 * MIB
VMEM_LIMIT_FUSED = 60 * MIB
VMEM_LIMIT_ATTN = 56 * MIB
VMEM_LIMIT_PROJ = 56 * MIB
assert max(VMEM_LIMIT, VMEM_LIMIT_FUSED, VMEM_LIMIT_ATTN, VMEM_LIMIT_PROJ) < V7X_VMEM_BYTES
PROJ_TM = 1024
SIDE_TILES_PER_STEP = 8
ATTN_GROUP = 8
PREP_ROWS = 256
BF16_ROWS = 16
FFN_MIN_HALF = 128

F32 = jnp.float32
BF16 = jnp.bfloat16


def _const_spec(shape):
    nd = len(shape)
    return pl.BlockSpec(shape, lambda *_: (0,) * nd, pipeline_mode=pl.Buffered(1))


def _proj_kernel(x_ref, g1_ref, w_ref, lng_ref, lnb_ref, mix_ref, mixb_ref, *refs,
                 tm, emit_vn, emit_t, n_cast):
    cast_in, outs = refs[:n_cast], refs[n_cast:]
    cast_out = outs[len(outs) - n_cast:]
    q_ref, k_ref, v_ref, sg_ref = outs[:4]
    rest = outs[4:len(outs) - n_cast]
    lane = lax.broadcasted_iota(jnp.int32, (1, LANES), 1)
    low = lane < HEAD_DIM

    x = x_ref[...]
    xg = (x * g1_ref[...]).astype(BF16)
    r = lax.rsqrt(jnp.mean(x * x, axis=-1, keepdims=True) + EPS)

    def proj(c0):
        return jnp.dot(xg, w_ref[:, c0:c0 + D_ATT], preferred_element_type=F32) * r

    vg = proj(3 * D_ATT + D_SGU)
    u = proj(3 * D_ATT)
    mu = jnp.mean(vg, axis=-1, keepdims=True)
    dv = vg - mu
    var = jnp.mean(dv * dv, axis=-1, keepdims=True)
    vn = dv * lax.rsqrt(var + EPS) * lng_ref[...] + lnb_ref[...]
    if emit_vn:
        rest[0][...] = vn
    q_ref[...] = proj(0)

    def gating(c0):
        for s in range(D_SGU // LANES):
            cols = slice(s * LANES, (s + 1) * LANES)
            slab = vn[c0:c0 + CHUNK, cols]
            lo = jnp.where(low, slab, 0.0).astype(BF16)
            hi = jnp.where(low, 0.0, slab).astype(BF16)
            gate = (jnp.dot(mix_ref[2 * s], lo, preferred_element_type=F32)
                    + jnp.dot(mix_ref[2 * s + 1], hi, preferred_element_type=F32)
                    + mixb_ref[:, cols])
            sg_ref[c0:c0 + CHUNK, cols] = (u[c0:c0 + CHUNK, cols] * gate).astype(BF16)

    chunks = list(range(0, tm, CHUNK))
    for c0 in chunks[:len(chunks) // 2]:
        gating(c0)
    k = proj(D_ATT)
    k_ref[...] = k
    for src, dst in zip(cast_in, cast_out):
        dst[...] = src[...].astype(BF16)
    for c0 in chunks[len(chunks) // 2:]:
        gating(c0)
    v = proj(2 * D_ATT)
    v_ref[...] = v
    if emit_t:
        kt_ref, vt_ref = rest[-2:]
        kt_ref[0] = k.T
        vt_ref[0] = v.T


def _proj(x, g1, w_in, ln_g, ln_b, mix, mixb, *, tm, emit_vn, emit_t, seq_len=None, cast=()):
    t = x.shape[0]
    n_tiles = t // tm
    row = lambda shape: pl.BlockSpec(shape, lambda i: (i, 0))
    out_shape = [jax.ShapeDtypeStruct((t, D_ATT), F32)] * 3 + [jax.ShapeDtypeStruct((t, D_SGU), BF16)]
    out_specs = [row((tm, D_ATT))] * 3 + [row((tm, D_SGU))]
    if emit_vn:
        out_shape.append(jax.ShapeDtypeStruct((t, D_SGU), F32))
        out_specs.append(row((tm, D_SGU)))
    if emit_t:
        tiles_per_seq = seq_len // tm
        win = min(WINDOW, seq_len)
        first = tiles_per_seq - win // tm

        def t_map(i):
            return (i // tiles_per_seq, 0, jnp.maximum(i % tiles_per_seq - first, 0))

        for _ in range(2):
            out_shape.append(jax.ShapeDtypeStruct((t // seq_len, D_ATT, win), F32))
            out_specs.append(pl.BlockSpec((1, D_ATT, tm), t_map))
    cast_specs = []
    for w in cast:
        steps = max(n for n in range(1, n_tiles + 1)
                    if w.shape[0] % n == 0 and (w.shape[0] // n) % BF16_ROWS == 0)
        spec = pl.BlockSpec((w.shape[0] // steps, w.shape[1]),
                            lambda i, steps=steps: (jnp.minimum(i, steps - 1), 0))
        cast_specs.append(spec)
        out_shape.append(jax.ShapeDtypeStruct(w.shape, BF16))
        out_specs.append(spec)
    kern = functools.partial(_proj_kernel, tm=tm, emit_vn=emit_vn, emit_t=emit_t, n_cast=len(cast))
    return pl.pallas_call(
        kern,
        grid=(n_tiles,),
        in_specs=[row((tm, D_MODEL)), _const_spec(g1.shape), _const_spec(w_in.shape),
                  _const_spec(ln_g.shape), _const_spec(ln_b.shape), _const_spec(mix.shape),
                  _const_spec(mixb.shape)] + cast_specs,
        out_specs=out_specs,
        out_shape=out_shape,
        compiler_params=pltpu.CompilerParams(dimension_semantics=("arbitrary",),
                                             vmem_limit_bytes=VMEM_LIMIT_PROJ),
        name="proj_cast" if cast else "proj",
    )(x, g1, w_in, ln_g, ln_b, mix, mixb, *cast)


def _div_pow2(x, n):
    assert n & (n - 1) == 0
    return lax.shift_right_logical(x, n.bit_length() - 1)


def _mod_pow2(x, n):
    assert n & (n - 1) == 0
    return lax.bitwise_and(x, n - 1)


def _attn_kernel(q_ref, k_ref, v_ref, r0_ref, o_ref, tab_ref, acc_ref, m_ref, l_ref,
                 q0_ref, q1_ref, kd_ref, v0_ref, v1_ref, s_ref, mx_ref, *, seq_len):
    hp = pl.program_id(0)
    lane = lax.broadcasted_iota(jnp.int32, (1, LANES), 1)
    low = lane < HEAD_DIM
    contract_last = (((1,), (1,)), ((), ()))
    neg = NEG_INF * LOG2E
    quarter = seq_len // 4
    lo_f = jnp.where(low, 1.0, 0.0).astype(F32)
    hi_f = 1.0 - lo_f
    q_scale = HEAD_DIM ** -0.5 * LOG2E

    @pl.when(pl.program_id(1) == 0)
    def _():
        col = lax.broadcasted_iota(jnp.int32, (Q_BLOCK, 2 * Q_BLOCK), 1)
        for br in range(len(BRANCHES)):
            for h in range(2):
                base = jnp.broadcast_to(r0_ref[br, pl.ds(2 * hp + h, 1), :],
                                        (Q_BLOCK, 2 * Q_BLOCK))
                t = pltpu.roll(base, 0, 1, stride=1, stride_axis=0)
                tab_ref[br, 0, h] = t
                tab_ref[br, 1, h] = jnp.where(col >= Q_BLOCK, t, neg)
        zeros = jnp.zeros((Q_BLOCK, LANES), BF16)
        for slot in range(2):
            kd_ref[slot, 0:Q_BLOCK, :] = zeros
            v0_ref[slot, 0:Q_BLOCK, :] = zeros
            v1_ref[slot, 0:Q_BLOCK, :] = zeros

    def strided(start, n, d):
        return pl.ds(start, n) if d == 1 else pl.ds(start, n, stride=d)

    dilation = [d for _, d in BRANCHES]
    n_groups = seq_len // Q_BLOCK // ATTN_GROUP

    def prep(br):
        d, slot = dilation[br], br % 2
        sub_len = seq_len // d

        def chunk(c, carry):
            de0 = c * PREP_ROWS
            dst = pl.ds(pl.multiple_of(de0, PREP_ROWS), PREP_ROWS)
            dstp = pl.ds(pl.multiple_of(de0 + Q_BLOCK, Q_BLOCK), PREP_ROWS)
            if d <= 4:
                src = strided(_div_pow2(de0, sub_len) + _mod_pow2(de0, sub_len) * d, PREP_ROWS, d)
                qq, kk, vv = q_ref[0, src, :], k_ref[0, src, :], v_ref[0, src, :]
            else:
                res, m0 = _div_pow2(de0, sub_len), _mod_pow2(de0, sub_len)
                src = pl.ds(_mod_pow2(res, 4) * quarter + m0 * (d // 4) + _div_pow2(res, 4),
                            PREP_ROWS, stride=d // 4)
                qq, kk, vv = acc_ref[2, src, :], m_ref[2, src, :], l_ref[2, src, :]
            if d == 4:
                acc_ref[2, dst, :] = qq
                m_ref[2, dst, :] = kk
                l_ref[2, dst, :] = vv
            q0_ref[slot, dst, :] = (qq * (lo_f * q_scale)).astype(BF16)
            q1_ref[slot, dst, :] = (qq * (hi_f * q_scale)).astype(BF16)
            kd_ref[slot, dstp, :] = kk.astype(BF16)
            v0_ref[slot, dstp, :] = (vv * lo_f + hi_f).astype(BF16)
            v1_ref[slot, dstp, :] = (vv * hi_f + lo_f).astype(BF16)
            return carry

        lax.fori_loop(0, seq_len // PREP_ROWS, chunk, 0, unroll=4)

    def score(br, i):
        slot, nb = br % 2, seq_len // dilation[br] // Q_BLOCK
        for u in range(ATTN_GROUP):
            g = i * ATTN_GROUP + u
            rq = pl.ds(pl.multiple_of(g * Q_BLOCK, Q_BLOCK), Q_BLOCK)
            rk = pl.ds(pl.multiple_of(g * Q_BLOCK, Q_BLOCK), 2 * Q_BLOCK)
            first = jnp.where(_mod_pow2(g, nb) == 0, 1, 0)
            kb = kd_ref[slot, rk, :]
            for h, qh_ref in enumerate((q0_ref, q1_ref)):
                s = lax.dot_general(qh_ref[slot, rq, :], kb, contract_last,
                                    preferred_element_type=F32)
                s = s + tab_ref[br, first, h]
                s_ref[u, h] = s
                mx_ref[u, h] = jnp.broadcast_to(jnp.max(s, axis=-1, keepdims=True),
                                                (Q_BLOCK, LANES))

    def finish(br, i):
        d, slot = dilation[br], br % 2
        nb = seq_len // d // Q_BLOCK
        for u in range(ATTN_GROUP):
            g = i * ATTN_GROUP + u
            rk = pl.ds(pl.multiple_of(g * Q_BLOCK, Q_BLOCK), 2 * Q_BLOCK)
            outs = []
            for h, vh_ref in enumerate((v0_ref, v1_ref)):
                m = mx_ref[u, h]
                p = jnp.concatenate([jnp.exp2(s_ref[u, h, :, :LANES] - m),
                                     jnp.exp2(s_ref[u, h, :, LANES:] - m)], axis=1)
                outs.append((jnp.dot(p.astype(BF16), vh_ref[slot, rk, :],
                                     preferred_element_type=F32), m))
            (o0, m0), (o1, m1) = outs
            if d <= 4:
                idx = pl.ds(pl.multiple_of(g * Q_BLOCK, Q_BLOCK), Q_BLOCK)
            else:
                res, blk = _div_pow2(g, nb), _mod_pow2(g, nb)
                idx = pl.ds(_mod_pow2(res, 4) * quarter + blk * (Q_BLOCK * d // 4)
                            + _div_pow2(res, 4), Q_BLOCK, stride=d // 4)
            acc_ref[br, idx, :] = jnp.where(low, o0, o1)
            l_ref[br, idx, :] = pltpu.roll(jnp.where(low, o1, o0), HEAD_DIM, 1)
            m_ref[br, idx, :] = jnp.where(low, m0, m1)

    prep(0)
    score(0, 0)
    for br in range(len(BRANCHES)):
        for i in range(1, n_groups):
            finish(br, i - 1)
            score(br, i)
        if br + 1 < len(BRANCHES):
            prep(br + 1)
            finish(br, n_groups - 1)
            score(br + 1, 0)
        else:
            finish(br, n_groups - 1)

    def merge(c, carry):
        de0 = c * PREP_ROWS
        dil = pl.ds(pl.multiple_of(de0, PREP_ROWS), PREP_ROWS)
        nat = pl.ds(_div_pow2(de0, quarter) + _mod_pow2(de0, quarter) * 4, PREP_ROWS, stride=4)
        ms = (m_ref[0, nat, :], m_ref[1, dil, :], m_ref[2, dil, :])
        m_all = jnp.maximum(jnp.maximum(ms[0], ms[1]), ms[2])
        ws = [jnp.exp2(m - m_all) for m in ms]
        num = (acc_ref[0, nat, :] * ws[0] + acc_ref[1, dil, :] * ws[1] + acc_ref[2, dil, :] * ws[2])
        den = l_ref[0, nat, :] * ws[0] + l_ref[1, dil, :] * ws[1] + l_ref[2, dil, :] * ws[2]
        acc_ref[0, nat, :] = num / den
        return carry

    lax.fori_loop(0, seq_len // PREP_ROWS, merge, 0)

    rows_per = 512

    def fin(i, c):
        sl = pl.ds(pl.multiple_of(i * rows_per, rows_per), rows_per)
        o_ref[0, sl, :] = acc_ref[0, sl, :].astype(o_ref.dtype)
        return c

    lax.fori_loop(0, seq_len // rows_per, fin, 0)


def _attn(q, k, v, r0):
    b, s, _ = q.shape
    n_br = len(BRANCHES)
    blk = pl.BlockSpec((1, s, LANES), lambda hp, i: (i, 0, hp))
    state = pltpu.VMEM((n_br, s, LANES), F32)
    qd = pltpu.VMEM((2, s, LANES), BF16)
    kd = pltpu.VMEM((2, s + Q_BLOCK, LANES), BF16)
    return pl.pallas_call(
        functools.partial(_attn_kernel, seq_len=s),
        grid=(D_ATT // LANES, b),
        in_specs=[blk, blk, blk, _const_spec(r0.shape)],
        out_specs=blk,
        out_shape=jax.ShapeDtypeStruct((b, s, D_ATT), BF16),
        scratch_shapes=[pltpu.VMEM((n_br, 2, 2, Q_BLOCK, 2 * Q_BLOCK), F32),
                        state, state, state, qd, qd, kd, kd, kd,
                        pltpu.VMEM((ATTN_GROUP, 2, Q_BLOCK, 2 * Q_BLOCK), F32),
                        pltpu.VMEM((ATTN_GROUP, 2, Q_BLOCK, LANES), F32)],
        compiler_params=pltpu.CompilerParams(dimension_semantics=("arbitrary", "arbitrary"),
                                             vmem_limit_bytes=VMEM_LIMIT_ATTN),
        name="attn",
    )(q, k, v, r0)


def _sample_side_steps(b, q_ref, ktn_ref, vtn_ref, logc_ref, kin_ref, vin_ref, kout_ref, vout_ref,
                       att_ref, *, wb, t_new):
    rows = t_new * N_HEADS
    contract_last = (((1,), (1,)), ((), ()))
    lane = lax.broadcasted_iota(jnp.int32, (1, LANES), 1)
    keep = lane < LANES - t_new
    n_tiles = wb // LANES
    st = {}

    def scores():
        shift = lax.bitwise_and(LANES - t_new * b, LANES - 1)
        st["ktn"] = pltpu.roll(ktn_ref[0], shift, 1)
        st["vtn"] = pltpu.roll(vtn_ref[0], shift, 1)
        q_tile = q_ref[0] * (HEAD_DIM ** -0.5)
        per_tile = 8 // t_new
        q = q_tile[0:t_new]
        for i in range(1, per_tile):
            q = jnp.where(lax.rem(b, per_tile) == i, q_tile[i * t_new:(i + 1) * t_new], q)
        qrep = jnp.broadcast_to(q[:, None, :], (t_new, N_HEADS, D_ATT)).reshape(rows, D_ATT)
        row_h = lax.broadcasted_iota(jnp.int32, (rows, D_ATT), 0) % N_HEADS
        col_h = lax.broadcasted_iota(jnp.int32, (rows, D_ATT), 1) // HEAD_DIM
        st["own"] = row_h == col_h
        qbd = jnp.where(st["own"], qrep, 0.0).astype(BF16)
        st["s"] = (jnp.dot(qbd, kin_ref[...].astype(BF16), preferred_element_type=F32)
                   + logc_ref[:, :wb])
        st["sn"] = (jnp.dot(qbd, st["ktn"].astype(BF16), preferred_element_type=F32)
                    + logc_ref[:, wb:])

    def softmax():
        s, sn = st["s"], st["sn"]
        m = jnp.maximum(jnp.max(s, axis=-1, keepdims=True), jnp.max(sn, axis=-1, keepdims=True))
        p = jnp.exp(s - m)
        pn = jnp.exp(sn - m)
        st["l"] = jnp.sum(p, axis=-1, keepdims=True) + jnp.sum(pn, axis=-1, keepdims=True)
        st["p"], st["pn"] = p.astype(BF16), pn.astype(BF16)

    def values():
        o = (lax.dot_general(st["p"], vin_ref[...].astype(BF16), contract_last,
                             preferred_element_type=F32)
             + lax.dot_general(st["pn"], st["vtn"].astype(BF16), contract_last,
                               preferred_element_type=F32))
        o = jnp.where(st["own"], o, 0.0) / st["l"]
        att_ref[0] = jnp.sum(o.reshape(t_new, N_HEADS, D_ATT), axis=1)

    def shift_tiles(src_ref, new_key, dst_ref, j0, j1):
        def run():
            cur = st.get(("cur", new_key))
            if cur is None:
                cur = pltpu.roll(src_ref[:, 0:LANES], LANES - t_new, 1)
            for j in range(j0, j1):
                if j + 1 < n_tiles:
                    nxt = pltpu.roll(src_ref[:, (j + 1) * LANES:(j + 2) * LANES], LANES - t_new, 1)
                else:
                    nxt = pltpu.roll(st[new_key], LANES - t_new, 1)
                dst_ref[:, j * LANES:(j + 1) * LANES] = jnp.where(keep, cur, nxt)
                cur = nxt
            st[("cur", new_key)] = cur
        return run

    per = SIDE_TILES_PER_STEP
    k_steps = [scores] + [shift_tiles(kin_ref, "ktn", kout_ref, j0, min(j0 + per, n_tiles))
                          for j0 in range(0, n_tiles, per)]
    v_steps = [softmax, values] + [shift_tiles(vin_ref, "vtn", vout_ref, j0, min(j0 + per, n_tiles))
                                   for j0 in range(0, n_tiles, per)]
    return k_steps, v_steps


def _ffn_body(x_ref, att_ref, sg_ref, wo_ref, g2_ref, wg_ref, wu_ref, wd_ref, gf_ref,
              y_ref, a_ref, side_steps=()):
    side = list(side_steps)
    assert len(side) < N_FF_CHUNKS
    tm = x_ref.shape[0]
    halves = [slice(0, tm // 2), slice(tm // 2, tm)] if tm >= 2 * FFN_MIN_HALF else [slice(0, tm)]

    def gate_up(rows, c, h):
        cols = slice(c * FF_CHUNK, (c + 1) * FF_CHUNK)
        g = jnp.dot(h, wg_ref[:, cols], preferred_element_type=F32)
        u = jnp.dot(h, wu_ref[:, cols], preferred_element_type=F32)
        a_ref[rows, cols] = (g * jax.nn.sigmoid(g) * u).astype(BF16)

    x1s, hs = [], []
    for rows in halves:
        x1 = (x_ref[rows, :]
              + jnp.dot(att_ref[rows, :].astype(BF16), wo_ref[:D_ATT, :], preferred_element_type=F32)
              + jnp.dot(sg_ref[rows, :], wo_ref[D_ATT:, :], preferred_element_type=F32))
        ms = jnp.mean(x1 * x1, axis=-1, keepdims=True)
        x1s.append(x1)
        hs.append((x1 * lax.rsqrt(ms + EPS) * g2_ref[...]).astype(BF16))
    for rows, h in zip(halves, hs):
        gate_up(rows, 0, h)
    h = hs[0] if len(hs) == 1 else jnp.concatenate(hs, axis=0)
    for c in range(1, N_FF_CHUNKS):
        if side:
            side.pop(0)()
        gate_up(slice(0, tm), c, h)
    for rows, x1 in zip(halves, x1s):
        x2 = x1 + jnp.dot(a_ref[rows, :], wd_ref[...], preferred_element_type=F32)
        ms2 = jnp.mean(x2 * x2, axis=-1, keepdims=True)
        y_ref[rows, :] = x2 * lax.rsqrt(ms2 + EPS) * gf_ref[...]


def _ffn_kernel(*refs):
    _ffn_body(*refs)


def _ffn_side_kernel(x_ref, att_ref, sg_ref, wo_ref, g2_ref, wg_ref, wu_ref, wd_ref, gf_ref,
                     q_ref, ktn_ref, vtn_ref, logc_ref, kt_hbm, vt_hbm,
                     y_ref, atts_ref, kto_hbm, vto_hbm,
                     a_ref, kin_ref, vin_ref, kout_ref, vout_ref, sems, *, wb, t_new):
    i = pl.program_id(0)
    last = pl.num_programs(0) - 1

    def fetch(j):
        return (pltpu.make_async_copy(kt_hbm.at[j], kin_ref, sems.at[0]),
                pltpu.make_async_copy(vt_hbm.at[j], vin_ref, sems.at[1]))

    def flush(j):
        return (pltpu.make_async_copy(kout_ref, kto_hbm.at[j], sems.at[2]),
                pltpu.make_async_copy(vout_ref, vto_hbm.at[j], sems.at[3]))

    @pl.when(i == 0)
    def _():
        for cp in fetch(0):
            cp.start()

    for cp in fetch(i):
        cp.wait()

    @pl.when(i > 0)
    def _():
        for cp in flush(i - 1):
            cp.wait()

    def then_swap(step, which):
        def run():
            step()
            flush(i)[which].start()
            fetch(jnp.minimum(i + 1, last))[which].start()
        return run

    k_steps, v_steps = _sample_side_steps(i, q_ref, ktn_ref, vtn_ref, logc_ref, kin_ref, vin_ref,
                                          kout_ref, vout_ref, atts_ref, wb=wb, t_new=t_new)
    k_steps[-1] = then_swap(k_steps[-1], 0)
    v_steps[-1] = then_swap(v_steps[-1], 1)
    steps = k_steps + v_steps
    _ffn_body(x_ref, att_ref, sg_ref, wo_ref, g2_ref, wg_ref, wu_ref, wd_ref, gf_ref,
              y_ref, a_ref, side_steps=steps)

    @pl.when(i == last)
    def _():
        for cp in fetch(i) + flush(i):
            cp.wait()


def _ffn_specs(tm, weights):
    row = lambda shape: pl.BlockSpec(shape, lambda i: (i, 0))
    return ([row((tm, D_MODEL)), row((tm, D_ATT)), row((tm, D_SGU))]
            + [_const_spec(w.shape) for w in weights]), row((tm, D_MODEL))


def _ffn(x, att, sg, weights, *, tm):
    t = x.shape[0]
    in_specs, out_spec = _ffn_specs(tm, weights)
    return pl.pallas_call(
        _ffn_kernel,
        grid=(t // tm,),
        in_specs=in_specs,
        out_specs=out_spec,
        out_shape=jax.ShapeDtypeStruct((t, D_MODEL), F32),
        scratch_shapes=[pltpu.VMEM((tm, D_FF), BF16)],
        compiler_params=pltpu.CompilerParams(dimension_semantics=("arbitrary",),
                                             vmem_limit_bytes=VMEM_LIMIT),
        name="ffn",
    )(x, att, sg, *weights)


def _ffn_with_sample_side(x, att, sg, weights, q_s, ktn, vtn, logc, kt, vt, *, tm, t_new):
    t = x.shape[0]
    bd, _, wb = kt.shape
    assert t // tm == bd
    in_specs, out_spec = _ffn_specs(tm, weights)
    any_spec = pl.BlockSpec(memory_space=pl.ANY)
    window = pltpu.VMEM((D_ATT, wb), F32)
    return pl.pallas_call(
        functools.partial(_ffn_side_kernel, wb=wb, t_new=t_new),
        grid=(bd,),
        in_specs=in_specs + [pl.BlockSpec((1, 8, D_ATT), lambda i: (i // (8 // t_new), 0, 0)),
                             _const_spec(ktn.shape), _const_spec(vtn.shape),
                             _const_spec(logc.shape), any_spec, any_spec],
        out_specs=[out_spec, pl.BlockSpec((1, t_new, D_ATT), lambda i: (i, 0, 0)),
                   any_spec, any_spec],
        out_shape=[jax.ShapeDtypeStruct((t, D_MODEL), F32),
                   jax.ShapeDtypeStruct((bd, t_new, D_ATT), F32),
                   jax.ShapeDtypeStruct(kt.shape, F32), jax.ShapeDtypeStruct(vt.shape, F32)],
        scratch_shapes=[pltpu.VMEM((tm, D_FF), BF16), window, window, window, window,
                        pltpu.SemaphoreType.DMA((4,))],
        compiler_params=pltpu.CompilerParams(dimension_semantics=("arbitrary",),
                                             vmem_limit_bytes=VMEM_LIMIT_FUSED),
        name="ffn_side",
    )(x, att, sg, *weights, q_s, ktn, vtn, logc, kt, vt)


def _rel_bucket(dist):
    max_exact = N_BUCKETS // 2
    df = jnp.maximum(dist, 1).astype(F32)
    large = max_exact + (jnp.log(df / max_exact) / math.log(MAX_DISTANCE / max_exact)
                         * (N_BUCKETS - max_exact)).astype(jnp.int32)
    large = jnp.minimum(large, N_BUCKETS - 1)
    return jnp.where(dist < max_exact, dist, large)


def _branch_bias_reversed(rel_bias, w, d):
    nj = w // d + 1
    dist = (nj - 1 - jnp.arange(nj, dtype=jnp.int32)) * d
    return rel_bias[_rel_bucket(dist)].T.astype(F32)


def _prompt_bias_rows(rev):
    rows = [jnp.pad(r, ((0, 0), (0, 2 * Q_BLOCK - r.shape[1])), constant_values=NEG_INF)
            for r in rev]
    return jnp.stack(rows) * LOG2E


def _sample_bias_table(rev, wb, t_new):
    width = wb + LANES
    per_branch = []
    for (w, d), r in zip(BRANCHES, rev):
        nj = r.shape[1]
        if d > 1:
            fill = jnp.full((N_HEADS, nj, d - 1), NEG_INF, F32)
            r = jnp.concatenate([r[:, :, None], fill], axis=2).reshape(N_HEADS, nj * d)
        rows = []
        for t in range(t_new):
            base = wb + t - (nj - 1) * d
            rows.append(jnp.pad(r, ((0, 0), (base, width - base - nj * d)),
                                constant_values=NEG_INF))
        per_branch.append(jnp.stack(rows))
    x = jnp.stack(per_branch)
    mx = jnp.max(x, axis=0)
    logc = mx + jnp.log(jnp.sum(jnp.exp(x - mx), axis=0))
    return logc.reshape(t_new * N_HEADS, width)


def kernel(x_prompt, x_sample, cache_k_win, cache_v_win, norm1_g, w_in, sgu_ln_g, sgu_ln_b,
           sgu_w, sgu_b, w_out, norm2_g, w_gate, w_up, w_down, rel_bias, final_g):
    depth = w_in.shape[0]
    assert depth == 1
    b, s, _ = x_prompt.shape
    bd, t_new, _ = x_sample.shape
    wb = cache_k_win.shape[2]
    assert bd * t_new == CHUNK and s % (Q_BLOCK * 16) == 0 and wb == WINDOW
    assert (s // Q_BLOCK) % ATTN_GROUP == 0 and s % PREP_ROWS == 0

    l = 0
    w_in_b = w_in[l].astype(BF16)
    g1 = norm1_g[l][None]
    g2 = norm2_g[l][None]
    gf = final_g[None]
    ln_g = sgu_ln_g[l][None]
    ln_b = sgu_ln_b[l][None]

    causal = jnp.tril(jnp.ones((CHUNK, CHUNK), F32))
    wm = sgu_w[l] * causal
    mix_p = wm.astype(BF16)
    mixb_p = jnp.repeat(sgu_b[l].T, HEAD_DIM, axis=1)
    rep = jnp.tile(jnp.eye(t_new, dtype=F32), (bd, 1))
    same_batch = jnp.kron(jnp.eye(bd, dtype=F32), jnp.ones((t_new, t_new), F32))
    mix_s = (jnp.einsum('it,gts,js->gij', rep, wm[:, :t_new, :t_new], rep)
             * same_batch).astype(BF16)
    mixb_s = jnp.tile(mixb_p[:t_new], (bd, 1))

    rev = [_branch_bias_reversed(rel_bias, w, d) for w, d in BRANCHES]

    xp = x_prompt.reshape(b * s, D_MODEL)
    q, k, v, sg, kt_p, vt_p, wo, wg, wu, wd = _proj(
        xp, g1, w_in_b, ln_g, ln_b, mix_p, mixb_p, tm=PROJ_TM, emit_vn=False, emit_t=True, seq_len=s,
        cast=(w_out[l], w_gate[l], w_up[l], w_down[l]))
    att = _attn(q.reshape(b, s, D_ATT), k.reshape(b, s, D_ATT), v.reshape(b, s, D_ATT),
                _prompt_bias_rows(rev))
    nw = min(WINDOW, s)
    new_k_p = kt_p.reshape(1, b, N_HEADS, HEAD_DIM, nw).transpose(0, 1, 4, 2, 3)
    new_v_p = vt_p.reshape(1, b, N_HEADS, HEAD_DIM, nw).transpose(0, 1, 4, 2, 3)

    xs = x_sample.reshape(bd * t_new, D_MODEL)
    qs, _, _, sgs, vn_s, kt_n, vt_n = _proj(xs, g1, w_in_b, ln_g, ln_b, mix_s, mixb_s,
                                            tm=CHUNK, emit_vn=True, emit_t=True, seq_len=CHUNK)
    kt_c = cache_k_win[l].transpose(0, 2, 3, 1).reshape(bd, D_ATT, wb)
    vt_c = cache_v_win[l].transpose(0, 2, 3, 1).reshape(bd, D_ATT, wb)

    logc = _sample_bias_table(rev, wb, t_new)
    weights = (wo, g2, wg, wu, wd, gf)
    y_prompt, att_s, kt_o, vt_o = _ffn_with_sample_side(
        xp, att.reshape(b * s, D_ATT), sg, weights, qs.reshape(bd * t_new // 8, 8, D_ATT),
        kt_n, vt_n, logc, kt_c, vt_c, tm=(b * s) // bd, t_new=t_new)
    y_prompt = y_prompt.reshape(b, s, D_MODEL)
    y_sample = _ffn(xs, att_s.reshape(bd * t_new, D_ATT), sgs, weights,
                    tm=CHUNK).reshape(bd, t_new, D_MODEL)
    new_k_s = kt_o.reshape(1, bd, N_HEADS, HEAD_DIM, wb).transpose(0, 1, 4, 2, 3)
    new_v_s = vt_o.reshape(1, bd, N_HEADS, HEAD_DIM, wb).transpose(0, 1, 4, 2, 3)
    sgu_v = vn_s.reshape(1, bd, t_new, D_SGU)

    return (y_prompt, y_sample, new_k_p, new_v_p, new_k_s, new_v_s, sgu_v)
```

```python
import functools
import math

import jax
import jax.numpy as jnp
from jax import lax
from jax.experimental import pallas as pl
from jax.experimental.pallas import tpu as pltpu

D_MODEL = 1024
N_HEADS = 8
HEAD_DIM = 64
D_ATT = N_HEADS * HEAD_DIM
N_GROUPS = 8
D_SGU = 512
CHUNK = 128
BRANCHES = ((128, 1), (512, 4), (2048, 16))
WINDOW = 2048
Q_BLOCK = 128
N_BUCKETS = 32
MAX_DISTANCE = WINDOW
D_FF = 2816
EPS = 1e-6
NEG_INF = -1e30
LOG2E = math.log2(math.e)

LANES = 128
FF_CHUNK = 256
N_FF_CHUNKS = D_FF // FF_CHUNK
MIB = 1024 * 1024
V7X_VMEM_BYTES = 64 * MIB
VMEM_LIMIT = 48 * MIB
VMEM_LIMIT_FUSED = 60 * MIB
VMEM_LIMIT_ATTN = 56 * MIB
VMEM_LIMIT_PROJ = 56 * MIB
assert max(VMEM_LIMIT, VMEM_LIMIT_FUSED, VMEM_LIMIT_ATTN, VMEM_LIMIT_PROJ) < V7X_VMEM_BYTES
PROJ_TM = 1024
SIDE_TILES_PER_STEP = 8
ATTN_GROUP = 8
PREP_ROWS = 256
BF16_ROWS = 16
FFN_MIN_HALF = 128

F32 = jnp.float32
BF16 = jnp.bfloat16


def _const_spec(shape):
    nd = len(shape)
    return pl.BlockSpec(shape, lambda *_: (0,) * nd, pipeline_mode=pl.Buffered(1))


def _proj_kernel(x_ref, g1_ref, w_ref, lng_ref, lnb_ref, mix_ref, mixb_ref, *refs,
                 tm, emit_vn, emit_t, n_cast):
    cast_in, outs = refs[:n_cast], refs[n_cast:]
    cast_out = outs[len(outs) - n_cast:]
    q_ref, k_ref, v_ref, sg_ref = outs[:4]
    rest = outs[4:len(outs) - n_cast]
    lane = lax.broadcasted_iota(jnp.int32, (1, LANES), 1)
    low = lane < HEAD_DIM

    x = x_ref[...]
    xg = (x * g1_ref[...]).astype(BF16)
    r = lax.rsqrt(jnp.mean(x * x, axis=-1, keepdims=True) + EPS)

    def proj(c0):
        return jnp.dot(xg, w_ref[:, c0:c0 + D_ATT], preferred_element_type=F32) * r

    vg = proj(3 * D_ATT + D_SGU)
    u = proj(3 * D_ATT)
    mu = jnp.mean(vg, axis=-1, keepdims=True)
    dv = vg - mu
    var = jnp.mean(dv * dv, axis=-1, keepdims=True)
    vn = dv * lax.rsqrt(var + EPS) * lng_ref[...] + lnb_ref[...]
    if emit_vn:
        rest[0][...] = vn
    k = proj(D_ATT)
    k_ref[...] = k
    if emit_t:
        kt_ref, vt_ref = rest[-2:]
        kt_ref[0] = k.T

    def gating(c0):
        for s in range(D_SGU // LANES):
            cols = slice(s * LANES, (s + 1) * LANES)
            slab = vn[c0:c0 + CHUNK, cols]
            lo = jnp.where(low, slab, 0.0).astype(BF16)
            hi = jnp.where(low, 0.0, slab).astype(BF16)
            gate = (jnp.dot(mix_ref[2 * s], lo, preferred_element_type=F32)
                    + jnp.dot(mix_ref[2 * s + 1], hi, preferred_element_type=F32)
                    + mixb_ref[:, cols])
            sg_ref[c0:c0 + CHUNK, cols] = (u[c0:c0 + CHUNK, cols] * gate).astype(BF16)

    chunks = list(range(0, tm, CHUNK))
    for c0 in chunks[:len(chunks) // 2]:
        gating(c0)
    v = proj(2 * D_ATT)
    v_ref[...] = v
    if emit_t:
        vt_ref[0] = v.T
    for src, dst in zip(cast_in, cast_out):
        dst[...] = src[...].astype(BF16)
    for c0 in chunks[len(chunks) // 2:]:
        gating(c0)
    q_ref[...] = proj(0)


def _proj(x, g1, w_in, ln_g, ln_b, mix, mixb, *, tm, emit_vn, emit_t, seq_len=None, cast=()):
    t = x.shape[0]
    n_tiles = t // tm
    row = lambda shape: pl.BlockSpec(shape, lambda i: (i, 0))
    out_shape = [jax.ShapeDtypeStruct((t, D_ATT), F32)] * 3 + [jax.ShapeDtypeStruct((t, D_SGU), BF16)]
    out_specs = [row((tm, D_ATT))] * 3 + [row((tm, D_SGU))]
    if emit_vn:
        out_shape.append(jax.ShapeDtypeStruct((t, D_SGU), F32))
        out_specs.append(row((tm, D_SGU)))
    if emit_t:
        tiles_per_seq = seq_len // tm
        win = min(WINDOW, seq_len)
        first = tiles_per_seq - win // tm

        def t_map(i):
            return (i // tiles_per_seq, 0, jnp.maximum(i % tiles_per_seq - first, 0))

        for _ in range(2):
            out_shape.append(jax.ShapeDtypeStruct((t // seq_len, D_ATT, win), F32))
            out_specs.append(pl.BlockSpec((1, D_ATT, tm), t_map))
    cast_specs = []
    for w in cast:
        steps = max(n for n in range(1, n_tiles + 1)
                    if w.shape[0] % n == 0 and (w.shape[0] // n) % BF16_ROWS == 0)
        spec = pl.BlockSpec((w.shape[0] // steps, w.shape[1]),
                            lambda i, steps=steps: (jnp.minimum(i, steps - 1), 0))
        cast_specs.append(spec)
        out_shape.append(jax.ShapeDtypeStruct(w.shape, BF16))
        out_specs.append(spec)
    kern = functools.partial(_proj_kernel, tm=tm, emit_vn=emit_vn, emit_t=emit_t, n_cast=len(cast))
    return pl.pallas_call(
        kern,
        grid=(n_tiles,),
        in_specs=[row((tm, D_MODEL)), _const_spec(g1.shape), _const_spec(w_in.shape),
                  _const_spec(ln_g.shape), _const_spec(ln_b.shape), _const_spec(mix.shape),
                  _const_spec(mixb.shape)] + cast_specs,
        out_specs=out_specs,
        out_shape=out_shape,
        compiler_params=pltpu.CompilerParams(dimension_semantics=("arbitrary",),
                                             vmem_limit_bytes=VMEM_LIMIT_PROJ),
        name="proj_cast" if cast else "proj",
    )(x, g1, w_in, ln_g, ln_b, mix, mixb, *cast)


def _div_pow2(x, n):
    assert n & (n - 1) == 0
    return lax.shift_right_logical(x, n.bit_length() - 1)


def _mod_pow2(x, n):
    assert n & (n - 1) == 0
    return lax.bitwise_and(x, n - 1)


def _attn_kernel(q_ref, k_ref, v_ref, r0_ref, o_ref, tab_ref, acc_ref, m_ref, l_ref,
                 q0_ref, q1_ref, kd_ref, v0_ref, v1_ref, s_ref, mx_ref, *, seq_len):
    hp = pl.program_id(0)
    lane = lax.broadcasted_iota(jnp.int32, (1, LANES), 1)
    low = lane < HEAD_DIM
    contract_last = (((1,), (1,)), ((), ()))
    neg = NEG_INF * LOG2E
    quarter = seq_len // 4
    lo_f = jnp.where(low, 1.0, 0.0).astype(F32)
    hi_f = 1.0 - lo_f
    q_scale = HEAD_DIM ** -0.5 * LOG2E

    @pl.when(pl.program_id(1) == 0)
    def _():
        col = lax.broadcasted_iota(jnp.int32, (Q_BLOCK, 2 * Q_BLOCK), 1)
        for br in range(len(BRANCHES)):
            for h in range(2):
                base = jnp.broadcast_to(r0_ref[br, pl.ds(2 * hp + h, 1), :],
                                        (Q_BLOCK, 2 * Q_BLOCK))
                t = pltpu.roll(base, 0, 1, stride=1, stride_axis=0)
                tab_ref[br, 0, h] = t
                tab_ref[br, 1, h] = jnp.where(col >= Q_BLOCK, t, neg)
        zeros = jnp.zeros((Q_BLOCK, LANES), BF16)
        for slot in range(2):
            kd_ref[slot, 0:Q_BLOCK, :] = zeros
            v0_ref[slot, 0:Q_BLOCK, :] = zeros
            v1_ref[slot, 0:Q_BLOCK, :] = zeros

    def strided(start, n, d):
        return pl.ds(start, n) if d == 1 else pl.ds(start, n, stride=d)

    dilation = [d for _, d in BRANCHES]
    n_groups = seq_len // Q_BLOCK // ATTN_GROUP

    def prep(br):
        d, slot = dilation[br], br % 2
        sub_len = seq_len // d

        def chunk(c, carry):
            de0 = c * PREP_ROWS
            dst = pl.ds(pl.multiple_of(de0, PREP_ROWS), PREP_ROWS)
            dstp = pl.ds(pl.multiple_of(de0 + Q_BLOCK, Q_BLOCK), PREP_ROWS)
            if d <= 4:
                src = strided(_div_pow2(de0, sub_len) + _mod_pow2(de0, sub_len) * d, PREP_ROWS, d)
                qq, kk, vv = q_ref[0, src, :], k_ref[0, src, :], v_ref[0, src, :]
            else:
                res, m0 = _div_pow2(de0, sub_len), _mod_pow2(de0, sub_len)
                src = pl.ds(_mod_pow2(res, 4) * quarter + m0 * (d // 4) + _div_pow2(res, 4),
                            PREP_ROWS, stride=d // 4)
                qq, kk, vv = acc_ref[2, src, :], m_ref[2, src, :], l_ref[2, src, :]
            if d == 4:
                acc_ref[2, dst, :] = qq
                m_ref[2, dst, :] = kk
                l_ref[2, dst, :] = vv
            q0_ref[slot, dst, :] = (qq * (lo_f * q_scale)).astype(BF16)
            q1_ref[slot, dst, :] = (qq * (hi_f * q_scale)).astype(BF16)
            kd_ref[slot, dstp, :] = kk.astype(BF16)
            v0_ref[slot, dstp, :] = (vv * lo_f + hi_f).astype(BF16)
            v1_ref[slot, dstp, :] = (vv * hi_f + lo_f).astype(BF16)
            return carry

        lax.fori_loop(0, seq_len // PREP_ROWS, chunk, 0, unroll=4)

    def score(br, i):
        slot, nb = br % 2, seq_len // dilation[br] // Q_BLOCK
        for u in range(ATTN_GROUP):
            g = i * ATTN_GROUP + u
            rq = pl.ds(pl.multiple_of(g * Q_BLOCK, Q_BLOCK), Q_BLOCK)
            rk = pl.ds(pl.multiple_of(g * Q_BLOCK, Q_BLOCK), 2 * Q_BLOCK)
            first = jnp.where(_mod_pow2(g, nb) == 0, 1, 0)
            kb = kd_ref[slot, rk, :]
            for h, qh_ref in enumerate((q0_ref, q1_ref)):
                s = lax.dot_general(qh_ref[slot, rq, :], kb, contract_last,
                                    preferred_element_type=F32)
                s = s + tab_ref[br, first, h]
                s_ref[u, h] = s
                mx_ref[u, h] = jnp.broadcast_to(jnp.max(s, axis=-1, keepdims=True),
                                                (Q_BLOCK, LANES))

    def finish(br, i):
        d, slot = dilation[br], br % 2
        nb = seq_len // d // Q_BLOCK
        for u in range(ATTN_GROUP):
            g = i * ATTN_GROUP + u
            rk = pl.ds(pl.multiple_of(g * Q_BLOCK, Q_BLOCK), 2 * Q_BLOCK)
            outs = []
            for h, vh_ref in enumerate((v0_ref, v1_ref)):
                m = mx_ref[u, h]
                p = jnp.concatenate([jnp.exp2(s_ref[u, h, :, :LANES] - m),
                                     jnp.exp2(s_ref[u, h, :, LANES:] - m)], axis=1)
                outs.append((jnp.dot(p.astype(BF16), vh_ref[slot, rk, :],
                                     preferred_element_type=F32), m))
            (o0, m0), (o1, m1) = outs
            if d <= 4:
                idx = pl.ds(pl.multiple_of(g * Q_BLOCK, Q_BLOCK), Q_BLOCK)
            else:
                res, blk = _div_pow2(g, nb), _mod_pow2(g, nb)
                idx = pl.ds(_mod_pow2(res, 4) * quarter + blk * (Q_BLOCK * d // 4)
                            + _div_pow2(res, 4), Q_BLOCK, stride=d // 4)
            acc_ref[br, idx, :] = jnp.where(low, o0, o1)
            l_ref[br, idx, :] = pltpu.roll(jnp.where(low, o1, o0), HEAD_DIM, 1)
            m_ref[br, idx, :] = jnp.where(low, m0, m1)

    prep(0)
    score(0, 0)
    for br in range(len(BRANCHES)):
        for i in range(1, n_groups):
            finish(br, i - 1)
            score(br, i)
        if br + 1 < len(BRANCHES):
            prep(br + 1)
            finish(br, n_groups - 1)
            score(br + 1, 0)
        else:
            finish(br, n_groups - 1)

    def merge(c, carry):
        de0 = c * PREP_ROWS
        dil = pl.ds(pl.multiple_of(de0, PREP_ROWS), PREP_ROWS)
        nat = pl.ds(_div_pow2(de0, quarter) + _mod_pow2(de0, quarter) * 4, PREP_ROWS, stride=4)
        ms = (m_ref[0, nat, :], m_ref[1, dil, :], m_ref[2, dil, :])
        m_all = jnp.maximum(jnp.maximum(ms[0], ms[1]), ms[2])
        ws = [jnp.exp2(m - m_all) for m in ms]
        num = (acc_ref[0, nat, :] * ws[0] + acc_ref[1, dil, :] * ws[1] + acc_ref[2, dil, :] * ws[2])
        den = l_ref[0, nat, :] * ws[0] + l_ref[1, dil, :] * ws[1] + l_ref[2, dil, :] * ws[2]
        acc_ref[0, nat, :] = num / den
        return carry

    lax.fori_loop(0, seq_len // PREP_ROWS, merge, 0)

    rows_per = 512

    def fin(i, c):
        sl = pl.ds(pl.multiple_of(i * rows_per, rows_per), rows_per)
        o_ref[0, sl, :] = acc_ref[0, sl, :].astype(o_ref.dtype)
        return c

    lax.fori_loop(0, seq_len // rows_per, fin, 0)


def _attn(q, k, v, r0):
    b, s, _ = q.shape
    n_br = len(BRANCHES)
    blk = pl.BlockSpec((1, s, LANES), lambda hp, i: (i, 0, hp))
    state = pltpu.VMEM((n_br, s, LANES), F32)
    qd = pltpu.VMEM((2, s, LANES), BF16)
    kd = pltpu.VMEM((2, s + Q_BLOCK, LANES), BF16)
    return pl.pallas_call(
        functools.partial(_attn_kernel, seq_len=s),
        grid=(D_ATT // LANES, b),
        in_specs=[blk, blk, blk, _const_spec(r0.shape)],
        out_specs=blk,
        out_shape=jax.ShapeDtypeStruct((b, s, D_ATT), BF16),
        scratch_shapes=[pltpu.VMEM((n_br, 2, 2, Q_BLOCK, 2 * Q_BLOCK), F32),
                        state, state, state, qd, qd, kd, kd, kd,
                        pltpu.VMEM((ATTN_GROUP, 2, Q_BLOCK, 2 * Q_BLOCK), F32),
                        pltpu.VMEM((ATTN_GROUP, 2, Q_BLOCK, LANES), F32)],
        compiler_params=pltpu.CompilerParams(dimension_semantics=("arbitrary", "arbitrary"),
                                             vmem_limit_bytes=VMEM_LIMIT_ATTN),
        name="attn",
    )(q, k, v, r0)


def _sample_side_steps(b, q_ref, ktn_ref, vtn_ref, logc_ref, kin_ref, vin_ref, kout_ref, vout_ref,
                       att_ref, *, wb, t_new):
    rows = t_new * N_HEADS
    contract_last = (((1,), (1,)), ((), ()))
    lane = lax.broadcasted_iota(jnp.int32, (1, LANES), 1)
    keep = lane < LANES - t_new
    n_tiles = wb // LANES
    st = {}

    def scores():
        shift = lax.bitwise_and(LANES - t_new * b, LANES - 1)
        st["ktn"] = pltpu.roll(ktn_ref[0], shift, 1)
        st["vtn"] = pltpu.roll(vtn_ref[0], shift, 1)
        q_tile = q_ref[0] * (HEAD_DIM ** -0.5)
        per_tile = 8 // t_new
        q = q_tile[0:t_new]
        for i in range(1, per_tile):
            q = jnp.where(lax.rem(b, per_tile) == i, q_tile[i * t_new:(i + 1) * t_new], q)
        qrep = jnp.broadcast_to(q[:, None, :], (t_new, N_HEADS, D_ATT)).reshape(rows, D_ATT)
        row_h = lax.broadcasted_iota(jnp.int32, (rows, D_ATT), 0) % N_HEADS
        col_h = lax.broadcasted_iota(jnp.int32, (rows, D_ATT), 1) // HEAD_DIM
        st["own"] = row_h == col_h
        qbd = jnp.where(st["own"], qrep, 0.0).astype(BF16)
        st["s"] = (jnp.dot(qbd, kin_ref[...].astype(BF16), preferred_element_type=F32)
                   + logc_ref[:, :wb])
        st["sn"] = (jnp.dot(qbd, st["ktn"].astype(BF16), preferred_element_type=F32)
                    + logc_ref[:, wb:])

    def softmax():
        s, sn = st["s"], st["sn"]
        m = jnp.maximum(jnp.max(s, axis=-1, keepdims=True), jnp.max(sn, axis=-1, keepdims=True))
        p = jnp.exp(s - m)
        pn = jnp.exp(sn - m)
        st["l"] = jnp.sum(p, axis=-1, keepdims=True) + jnp.sum(pn, axis=-1, keepdims=True)
        st["p"], st["pn"] = p.astype(BF16), pn.astype(BF16)

    def values():
        o = (lax.dot_general(st["p"], vin_ref[...].astype(BF16), contract_last,
                             preferred_element_type=F32)
             + lax.dot_general(st["pn"], st["vtn"].astype(BF16), contract_last,
                               preferred_element_type=F32))
        o = jnp.where(st["own"], o, 0.0) / st["l"]
        att_ref[0] = jnp.sum(o.reshape(t_new, N_HEADS, D_ATT), axis=1)

    def shift_tiles(src_ref, new_key, dst_ref, j0, j1):
        def run():
            cur = st.get(("cur", new_key))
            if cur is None:
                cur = pltpu.roll(src_ref[:, 0:LANES], LANES - t_new, 1)
            for j in range(j0, j1):
                if j + 1 < n_tiles:
                    nxt = pltpu.roll(src_ref[:, (j + 1) * LANES:(j + 2) * LANES], LANES - t_new, 1)
                else:
                    nxt = pltpu.roll(st[new_key], LANES - t_new, 1)
                dst_ref[:, j * LANES:(j + 1) * LANES] = jnp.where(keep, cur, nxt)
                cur = nxt
            st[("cur", new_key)] = cur
        return run

    per = SIDE_TILES_PER_STEP
    k_steps = [scores] + [shift_tiles(kin_ref, "ktn", kout_ref, j0, min(j0 + per, n_tiles))
                          for j0 in range(0, n_tiles, per)]
    v_steps = [softmax, values] + [shift_tiles(vin_ref, "vtn", vout_ref, j0, min(j0 + per, n_tiles))
                                   for j0 in range(0, n_tiles, per)]
    return k_steps, v_steps


def _ffn_body(x_ref, att_ref, sg_ref, wo_ref, g2_ref, wg_ref, wu_ref, wd_ref, gf_ref,
              y_ref, a_ref, side_steps=()):
    side = list(side_steps)
    assert len(side) < N_FF_CHUNKS
    tm = x_ref.shape[0]
    halves = [slice(0, tm // 2), slice(tm // 2, tm)] if tm >= 2 * FFN_MIN_HALF else [slice(0, tm)]

    def gate_up(rows, c, h):
        cols = slice(c * FF_CHUNK, (c + 1) * FF_CHUNK)
        g = jnp.dot(h, wg_ref[:, cols], preferred_element_type=F32)
        u = jnp.dot(h, wu_ref[:, cols], preferred_element_type=F32)
        a_ref[rows, cols] = (g * jax.nn.sigmoid(g) * u).astype(BF16)

    x1s, hs = [], []
    for rows in halves:
        x1 = (x_ref[rows, :]
              + jnp.dot(att_ref[rows, :].astype(BF16), wo_ref[:D_ATT, :], preferred_element_type=F32)
              + jnp.dot(sg_ref[rows, :], wo_ref[D_ATT:, :], preferred_element_type=F32))
        ms = jnp.mean(x1 * x1, axis=-1, keepdims=True)
        x1s.append(x1)
        hs.append((x1 * lax.rsqrt(ms + EPS) * g2_ref[...]).astype(BF16))
    for rows, h in zip(halves, hs):
        gate_up(rows, 0, h)
    h = hs[0] if len(hs) == 1 else jnp.concatenate(hs, axis=0)
    for c in range(1, N_FF_CHUNKS):
        if side:
            side.pop(0)()
        gate_up(slice(0, tm), c, h)
    for rows, x1 in zip(halves, x1s):
        x2 = x1 + jnp.dot(a_ref[rows, :], wd_ref[...], preferred_element_type=F32)
        ms2 = jnp.mean(x2 * x2, axis=-1, keepdims=True)
        y_ref[rows, :] = x2 * lax.rsqrt(ms2 + EPS) * gf_ref[...]


def _ffn_kernel(*refs):
    _ffn_body(*refs)


def _ffn_side_kernel(x_ref, att_ref, sg_ref, wo_ref, g2_ref, wg_ref, wu_ref, wd_ref, gf_ref,
                     q_ref, ktn_ref, vtn_ref, logc_ref, kt_hbm, vt_hbm,
                     y_ref, atts_ref, kto_hbm, vto_hbm,
                     a_ref, kin_ref, vin_ref, kout_ref, vout_ref, sems, *, wb, t_new):
    i = pl.program_id(0)
    last = pl.num_programs(0) - 1

    def fetch(j):
        return (pltpu.make_async_copy(kt_hbm.at[j], kin_ref, sems.at[0]),
                pltpu.make_async_copy(vt_hbm.at[j], vin_ref, sems.at[1]))

    def flush(j):
        return (pltpu.make_async_copy(kout_ref, kto_hbm.at[j], sems.at[2]),
                pltpu.make_async_copy(vout_ref, vto_hbm.at[j], sems.at[3]))

    @pl.when(i == 0)
    def _():
        for cp in fetch(0):
            cp.start()

    for cp in fetch(i):
        cp.wait()

    @pl.when(i > 0)
    def _():
        for cp in flush(i - 1):
            cp.wait()

    def then_swap(step, which):
        def run():
            step()
            flush(i)[which].start()
            fetch(jnp.minimum(i + 1, last))[which].start()
        return run

    k_steps, v_steps = _sample_side_steps(i, q_ref, ktn_ref, vtn_ref, logc_ref, kin_ref, vin_ref,
                                          kout_ref, vout_ref, atts_ref, wb=wb, t_new=t_new)
    k_steps[-1] = then_swap(k_steps[-1], 0)
    v_steps[-1] = then_swap(v_steps[-1], 1)
    steps = k_steps + v_steps
    _ffn_body(x_ref, att_ref, sg_ref, wo_ref, g2_ref, wg_ref, wu_ref, wd_ref, gf_ref,
              y_ref, a_ref, side_steps=steps)

    @pl.when(i == last)
    def _():
        for cp in fetch(i) + flush(i):
            cp.wait()


def _ffn_specs(tm, weights):
    row = lambda shape: pl.BlockSpec(shape, lambda i: (i, 0))
    return ([row((tm, D_MODEL)), row((tm, D_ATT)), row((tm, D_SGU))]
            + [_const_spec(w.shape) for w in weights]), row((tm, D_MODEL))


def _ffn(x, att, sg, weights, *, tm):
    t = x.shape[0]
    in_specs, out_spec = _ffn_specs(tm, weights)
    return pl.pallas_call(
        _ffn_kernel,
        grid=(t // tm,),
        in_specs=in_specs,
        out_specs=out_spec,
        out_shape=jax.ShapeDtypeStruct((t, D_MODEL), F32),
        scratch_shapes=[pltpu.VMEM((tm, D_FF), BF16)],
        compiler_params=pltpu.CompilerParams(dimension_semantics=("arbitrary",),
                                             vmem_limit_bytes=VMEM_LIMIT),
        name="ffn",
    )(x, att, sg, *weights)


def _ffn_with_sample_side(x, att, sg, weights, q_s, ktn, vtn, logc, kt, vt, *, tm, t_new):
    t = x.shape[0]
    bd, _, wb = kt.shape
    assert t // tm == bd
    in_specs, out_spec = _ffn_specs(tm, weights)
    any_spec = pl.BlockSpec(memory_space=pl.ANY)
    window = pltpu.VMEM((D_ATT, wb), F32)
    return pl.pallas_call(
        functools.partial(_ffn_side_kernel, wb=wb, t_new=t_new),
        grid=(bd,),
        in_specs=in_specs + [pl.BlockSpec((1, 8, D_ATT), lambda i: (i // (8 // t_new), 0, 0)),
                             _const_spec(ktn.shape), _const_spec(vtn.shape),
                             _const_spec(logc.shape), any_spec, any_spec],
        out_specs=[out_spec, pl.BlockSpec((1, t_new, D_ATT), lambda i: (i, 0, 0)),
                   any_spec, any_spec],
        out_shape=[jax.ShapeDtypeStruct((t, D_MODEL), F32),
                   jax.ShapeDtypeStruct((bd, t_new, D_ATT), F32),
                   jax.ShapeDtypeStruct(kt.shape, F32), jax.ShapeDtypeStruct(vt.shape, F32)],
        scratch_shapes=[pltpu.VMEM((tm, D_FF), BF16), window, window, window, window,
                        pltpu.SemaphoreType.DMA((4,))],
        compiler_params=pltpu.CompilerParams(dimension_semantics=("arbitrary",),
                                             vmem_limit_bytes=VMEM_LIMIT_FUSED),
        name="ffn_side",
    )(x, att, sg, *weights, q_s, ktn, vtn, logc, kt, vt)


def _rel_bucket(dist):
    max_exact = N_BUCKETS // 2
    df = jnp.maximum(dist, 1).astype(F32)
    large = max_exact + (jnp.log(df / max_exact) / math.log(MAX_DISTANCE / max_exact)
                         * (N_BUCKETS - max_exact)).astype(jnp.int32)
    large = jnp.minimum(large, N_BUCKETS - 1)
    return jnp.where(dist < max_exact, dist, large)


def _branch_bias_reversed(rel_bias, w, d):
    nj = w // d + 1
    dist = (nj - 1 - jnp.arange(nj, dtype=jnp.int32)) * d
    return rel_bias[_rel_bucket(dist)].T.astype(F32)


def _prompt_bias_rows(rev):
    rows = [jnp.pad(r, ((0, 0), (0, 2 * Q_BLOCK - r.shape[1])), constant_values=NEG_INF)
            for r in rev]
    return jnp.stack(rows) * LOG2E


def _sample_bias_table(rev, wb, t_new):
    width = wb + LANES
    per_branch = []
    for (w, d), r in zip(BRANCHES, rev):
        nj = r.shape[1]
        if d > 1:
            fill = jnp.full((N_HEADS, nj, d - 1), NEG_INF, F32)
            r = jnp.concatenate([r[:, :, None], fill], axis=2).reshape(N_HEADS, nj * d)
        rows = []
        for t in range(t_new):
            base = wb + t - (nj - 1) * d
            rows.append(jnp.pad(r, ((0, 0), (base, width - base - nj * d)),
                                constant_values=NEG_INF))
        per_branch.append(jnp.stack(rows))
    x = jnp.stack(per_branch)
    mx = jnp.max(x, axis=0)
    logc = mx + jnp.log(jnp.sum(jnp.exp(x - mx), axis=0))
    return logc.reshape(t_new * N_HEADS, width)


def kernel(x_prompt, x_sample, cache_k_win, cache_v_win, norm1_g, w_in, sgu_ln_g, sgu_ln_b,
           sgu_w, sgu_b, w_out, norm2_g, w_gate, w_up, w_down, rel_bias, final_g):
    depth = w_in.shape[0]
    assert depth == 1
    b, s, _ = x_prompt.shape
    bd, t_new, _ = x_sample.shape
    wb = cache_k_win.shape[2]
    assert bd * t_new == CHUNK and s % (Q_BLOCK * 16) == 0 and wb == WINDOW
    assert (s // Q_BLOCK) % ATTN_GROUP == 0 and s % PREP_ROWS == 0

    l = 0
    w_in_b = w_in[l].astype(BF16)
    g1 = norm1_g[l][None]
    g2 = norm2_g[l][None]
    gf = final_g[None]
    ln_g = sgu_ln_g[l][None]
    ln_b = sgu_ln_b[l][None]

    causal = jnp.tril(jnp.ones((CHUNK, CHUNK), F32))
    wm = sgu_w[l] * causal
    mix_p = wm.astype(BF16)
    mixb_p = jnp.repeat(sgu_b[l].T, HEAD_DIM, axis=1)
    rep = jnp.tile(jnp.eye(t_new, dtype=F32), (bd, 1))
    same_batch = jnp.kron(jnp.eye(bd, dtype=F32), jnp.ones((t_new, t_new), F32))
    mix_s = (jnp.einsum('it,gts,js->gij', rep, wm[:, :t_new, :t_new], rep)
             * same_batch).astype(BF16)
    mixb_s = jnp.tile(mixb_p[:t_new], (bd, 1))

    rev = [_branch_bias_reversed(rel_bias, w, d) for w, d in BRANCHES]

    xp = x_prompt.reshape(b * s, D_MODEL)
    q, k, v, sg, kt_p, vt_p, wo, wg, wu, wd = _proj(
        xp, g1, w_in_b, ln_g, ln_b, mix_p, mixb_p, tm=PROJ_TM, emit_vn=False, emit_t=True, seq_len=s,
        cast=(w_out[l], w_gate[l], w_up[l], w_down[l]))
    att = _attn(q.reshape(b, s, D_ATT), k.reshape(b, s, D_ATT), v.reshape(b, s, D_ATT),
                _prompt_bias_rows(rev))
    nw = min(WINDOW, s)
    new_k_p = kt_p.reshape(1, b, N_HEADS, HEAD_DIM, nw).transpose(0, 1, 4, 2, 3)
    new_v_p = vt_p.reshape(1, b, N_HEADS, HEAD_DIM, nw).transpose(0, 1, 4, 2, 3)

    xs = x_sample.reshape(bd * t_new, D_MODEL)
    qs, _, _, sgs, vn_s, kt_n, vt_n = _proj(xs, g1, w_in_b, ln_g, ln_b, mix_s, mixb_s,
                                            tm=CHUNK, emit_vn=True, emit_t=True, seq_len=CHUNK)
    kt_c = cache_k_win[l].transpose(0, 2, 3, 1).reshape(bd, D_ATT, wb)
    vt_c = cache_v_win[l].transpose(0, 2, 3, 1).reshape(bd, D_ATT, wb)

    logc = _sample_bias_table(rev, wb, t_new)
    weights = (wo, g2, wg, wu, wd, gf)
    y_prompt, att_s, kt_o, vt_o = _ffn_with_sample_side(
        xp, att.reshape(b * s, D_ATT), sg, weights, qs.reshape(bd * t_new // 8, 8, D_ATT),
        kt_n, vt_n, logc, kt_c, vt_c, tm=(b * s) // bd, t_new=t_new)
    y_prompt = y_prompt.reshape(b, s, D_MODEL)
    y_sample = _ffn(xs, att_s.reshape(bd * t_new, D_ATT), sgs, weights,
                    tm=CHUNK).reshape(bd, t_new, D_MODEL)
    new_k_s = kt_o.reshape(1, bd, N_HEADS, HEAD_DIM, wb).transpose(0, 1, 4, 2, 3)
    new_v_s = vt_o.reshape(1, bd, N_HEADS, HEAD_DIM, wb).transpose(0, 1, 4, 2, 3)
    sgu_v = vn_s.reshape(1, bd, t_new, D_SGU)

    return (y_prompt, y_sample, new_k_p, new_v_p, new_k_s, new_v_s, sgu_v)
```

```python
import functools
import math

import jax
import jax.numpy as jnp
from jax import lax
from jax.experimental import pallas as pl
from jax.experimental.pallas import tpu as pltpu

D_MODEL = 1024
N_HEADS = 8
HEAD_DIM = 64
D_ATT = N_HEADS * HEAD_DIM
N_GROUPS = 8
D_SGU = 512
CHUNK = 128
BRANCHES = ((128, 1), (512, 4), (2048, 16))
WINDOW = 2048
Q_BLOCK = 128
N_BUCKETS = 32
MAX_DISTANCE = WINDOW
D_FF = 2816
EPS = 1e-6
NEG_INF = -1e30
LOG2E = math.log2(math.e)

LANES = 128
FF_CHUNK = 256
N_FF_CHUNKS = D_FF // FF_CHUNK
MIB = 1024 * 1024
V7X_VMEM_BYTES = 64 * MIB
VMEM_LIMIT = 48 * MIB
VMEM_LIMIT_FUSED = 60 * MIB
VMEM_LIMIT_ATTN = 56 * MIB
VMEM_LIMIT_PROJ = 56 * MIB
assert max(VMEM_LIMIT, VMEM_LIMIT_FUSED, VMEM_LIMIT_ATTN, VMEM_LIMIT_PROJ) < V7X_VMEM_BYTES
PROJ_TM = 1024
SIDE_TILES_PER_STEP = 8
ATTN_GROUP = 8
PREP_ROWS = 256
BF16_ROWS = 16
FFN_MIN_HALF = 128

F32 = jnp.float32
BF16 = jnp.bfloat16


def _const_spec(shape):
    nd = len(shape)
    return pl.BlockSpec(shape, lambda *_: (0,) * nd, pipeline_mode=pl.Buffered(1))


def _proj_kernel(x_ref, g1_ref, w_ref, lng_ref, lnb_ref, mix_ref, mixb_ref, *refs,
                 tm, emit_vn, emit_t, n_cast):
    cast_in, outs = refs[:n_cast], refs[n_cast:]
    cast_out = outs[len(outs) - n_cast:]
    q_ref, k_ref, v_ref, sg_ref = outs[:4]
    rest = outs[4:len(outs) - n_cast]
    lane = lax.broadcasted_iota(jnp.int32, (1, LANES), 1)
    low = lane < HEAD_DIM

    x = x_ref[...]
    xg = (x * g1_ref[...]).astype(BF16)
    r = lax.rsqrt(jnp.mean(x * x, axis=-1, keepdims=True) + EPS)

    def proj(c0):
        return jnp.dot(xg, w_ref[:, c0:c0 + D_ATT], preferred_element_type=F32) * r

    vg = proj(3 * D_ATT + D_SGU)
    u = proj(3 * D_ATT)
    mu = jnp.mean(vg, axis=-1, keepdims=True)
    dv = vg - mu
    var = jnp.mean(dv * dv, axis=-1, keepdims=True)
    vn = dv * lax.rsqrt(var + EPS) * lng_ref[...] + lnb_ref[...]
    if emit_vn:
        rest[0][...] = vn
    k = proj(D_ATT)
    k_ref[...] = k
    if emit_t:
        kt_ref, vt_ref = rest[-2:]
        kt_ref[0] = k.T

    def gating(c0):
        for s in range(D_SGU // LANES):
            cols = slice(s * LANES, (s + 1) * LANES)
            slab = vn[c0:c0 + CHUNK, cols]
            lo = jnp.where(low, slab, 0.0).astype(BF16)
            hi = jnp.where(low, 0.0, slab).astype(BF16)
            gate = (jnp.dot(mix_ref[2 * s], lo, preferred_element_type=F32)
                    + jnp.dot(mix_ref[2 * s + 1], hi, preferred_element_type=F32)
                    + mixb_ref[:, cols])
            sg_ref[c0:c0 + CHUNK, cols] = (u[c0:c0 + CHUNK, cols] * gate).astype(BF16)

    chunks = list(range(0, tm, CHUNK))
    for c0 in chunks[:len(chunks) // 2]:
        gating(c0)
    v = proj(2 * D_ATT)
    v_ref[...] = v
    if emit_t:
        vt_ref[0] = v.T
    for src, dst in zip(cast_in, cast_out):
        dst[...] = src[...].astype(BF16)
    for c0 in chunks[len(chunks) // 2:]:
        gating(c0)
    q_ref[...] = proj(0)


def _proj(x, g1, w_in, ln_g, ln_b, mix, mixb, *, tm, emit_vn, emit_t, seq_len=None, cast=()):
    t = x.shape[0]
    n_tiles = t // tm
    row = lambda shape: pl.BlockSpec(shape, lambda i: (i, 0))
    out_shape = [jax.ShapeDtypeStruct((t, D_ATT), F32)] * 3 + [jax.ShapeDtypeStruct((t, D_SGU), BF16)]
    out_specs = [row((tm, D_ATT))] * 3 + [row((tm, D_SGU))]
    if emit_vn:
        out_shape.append(jax.ShapeDtypeStruct((t, D_SGU), F32))
        out_specs.append(row((tm, D_SGU)))
    if emit_t:
        tiles_per_seq = seq_len // tm
        win = min(WINDOW, seq_len)
        first = tiles_per_seq - win // tm

        def t_map(i):
            return (i // tiles_per_seq, 0, jnp.maximum(i % tiles_per_seq - first, 0))

        for _ in range(2):
            out_shape.append(jax.ShapeDtypeStruct((t // seq_len, D_ATT, win), F32))
            out_specs.append(pl.BlockSpec((1, D_ATT, tm), t_map))
    cast_specs = []
    for w in cast:
        steps = max(n for n in range(1, n_tiles + 1)
                    if w.shape[0] % n == 0 and (w.shape[0] // n) % BF16_ROWS == 0)
        spec = pl.BlockSpec((w.shape[0] // steps, w.shape[1]),
                            lambda i, steps=steps: (jnp.minimum(i, steps - 1), 0))
        cast_specs.append(spec)
        out_shape.append(jax.ShapeDtypeStruct(w.shape, BF16))
        out_specs.append(spec)
    kern = functools.partial(_proj_kernel, tm=tm, emit_vn=emit_vn, emit_t=emit_t, n_cast=len(cast))
    return pl.pallas_call(
        kern,
        grid=(n_tiles,),
        in_specs=[row((tm, D_MODEL)), _const_spec(g1.shape), _const_spec(w_in.shape),
                  _const_spec(ln_g.shape), _const_spec(ln_b.shape), _const_spec(mix.shape),
                  _const_spec(mixb.shape)] + cast_specs,
        out_specs=out_specs,
        out_shape=out_shape,
        compiler_params=pltpu.CompilerParams(dimension_semantics=("arbitrary",),
                                             vmem_limit_bytes=VMEM_LIMIT_PROJ),
        name="proj_cast" if cast else "proj",
    )(x, g1, w_in, ln_g, ln_b, mix, mixb, *cast)


def _div_pow2(x, n):
    assert n & (n - 1) == 0
    return lax.shift_right_logical(x, n.bit_length() - 1)


def _mod_pow2(x, n):
    assert n & (n - 1) == 0
    return lax.bitwise_and(x, n - 1)


def _attn_kernel(q_ref, k_ref, v_ref, r0_ref, o_ref, tab_ref, acc_ref, m_ref, l_ref,
                 q0_ref, q1_ref, kd_ref, v0_ref, v1_ref, s_ref, mx_ref, *, seq_len):
    hp = pl.program_id(0)
    lane = lax.broadcasted_iota(jnp.int32, (1, LANES), 1)
    low = lane < HEAD_DIM
    contract_last = (((1,), (1,)), ((), ()))
    neg = NEG_INF * LOG2E
    quarter = seq_len // 4
    lo_f = jnp.where(low, 1.0, 0.0).astype(F32)
    hi_f = 1.0 - lo_f
    q_scale = HEAD_DIM ** -0.5 * LOG2E

    @pl.when(pl.program_id(1) == 0)
    def _():
        col = lax.broadcasted_iota(jnp.int32, (Q_BLOCK, 2 * Q_BLOCK), 1)
        for br in range(len(BRANCHES)):
            for h in range(2):
                base = jnp.broadcast_to(r0_ref[br, pl.ds(2 * hp + h, 1), :],
                                        (Q_BLOCK, 2 * Q_BLOCK))
                t = pltpu.roll(base, 0, 1, stride=1, stride_axis=0)
                tab_ref[br, 0, h] = t
                tab_ref[br, 1, h] = jnp.where(col >= Q_BLOCK, t, neg)
        zeros = jnp.zeros((Q_BLOCK, LANES), BF16)
        for slot in range(2):
            kd_ref[slot, 0:Q_BLOCK, :] = zeros
            v0_ref[slot, 0:Q_BLOCK, :] = zeros
            v1_ref[slot, 0:Q_BLOCK, :] = zeros

    def strided(start, n, d):
        return pl.ds(start, n) if d == 1 else pl.ds(start, n, stride=d)

    dilation = [d for _, d in BRANCHES]
    n_groups = seq_len // Q_BLOCK // ATTN_GROUP

    def prep(br):
        d, slot = dilation[br], br % 2
        sub_len = seq_len // d

        def chunk(c, carry):
            de0 = c * PREP_ROWS
            dst = pl.ds(pl.multiple_of(de0, PREP_ROWS), PREP_ROWS)
            dstp = pl.ds(pl.multiple_of(de0 + Q_BLOCK, Q_BLOCK), PREP_ROWS)
            if d <= 4:
                src = strided(_div_pow2(de0, sub_len) + _mod_pow2(de0, sub_len) * d, PREP_ROWS, d)
                qq, kk, vv = q_ref[0, src, :], k_ref[0, src, :], v_ref[0, src, :]
            else:
                res, m0 = _div_pow2(de0, sub_len), _mod_pow2(de0, sub_len)
                src = pl.ds(_mod_pow2(res, 4) * quarter + m0 * (d // 4) + _div_pow2(res, 4),
                            PREP_ROWS, stride=d // 4)
                qq, kk, vv = acc_ref[2, src, :], m_ref[2, src, :], l_ref[2, src, :]
            if d == 4:
                acc_ref[2, dst, :] = qq
                m_ref[2, dst, :] = kk
                l_ref[2, dst, :] = vv
            q0_ref[slot, dst, :] = (qq * (lo_f * q_scale)).astype(BF16)
            q1_ref[slot, dst, :] = (qq * (hi_f * q_scale)).astype(BF16)
            kd_ref[slot, dstp, :] = kk.astype(BF16)
            v0_ref[slot, dstp, :] = (vv * lo_f + hi_f).astype(BF16)
            v1_ref[slot, dstp, :] = (vv * hi_f + lo_f).astype(BF16)
            return carry

        lax.fori_loop(0, seq_len // PREP_ROWS, chunk, 0, unroll=4)

    def score(br, i):
        slot, nb = br % 2, seq_len // dilation[br] // Q_BLOCK
        for u in range(ATTN_GROUP):
            g = i * ATTN_GROUP + u
            rq = pl.ds(pl.multiple_of(g * Q_BLOCK, Q_BLOCK), Q_BLOCK)
            rk = pl.ds(pl.multiple_of(g * Q_BLOCK, Q_BLOCK), 2 * Q_BLOCK)
            first = jnp.where(_mod_pow2(g, nb) == 0, 1, 0)
            kb = kd_ref[slot, rk, :]
            for h, qh_ref in enumerate((q0_ref, q1_ref)):
                s = lax.dot_general(qh_ref[slot, rq, :], kb, contract_last,
                                    preferred_element_type=F32)
                s = s + tab_ref[br, first, h]
                s_ref[u, h] = s
                mx_ref[u, h] = jnp.broadcast_to(jnp.max(s, axis=-1, keepdims=True),
                                                (Q_BLOCK, LANES))

    def finish(br, i):
        d, slot = dilation[br], br % 2
        nb = seq_len // d // Q_BLOCK
        for u in range(ATTN_GROUP):
            g = i * ATTN_GROUP + u
            rk = pl.ds(pl.multiple_of(g * Q_BLOCK, Q_BLOCK), 2 * Q_BLOCK)
            outs = []
            for h, vh_ref in enumerate((v0_ref, v1_ref)):
                m = mx_ref[u, h]
                p = jnp.concatenate([jnp.exp2(s_ref[u, h, :, :LANES] - m),
                                     jnp.exp2(s_ref[u, h, :, LANES:] - m)], axis=1)
                outs.append((jnp.dot(p.astype(BF16), vh_ref[slot, rk, :],
                                     preferred_element_type=F32), m))
            (o0, m0), (o1, m1) = outs
            if d <= 4:
                idx = pl.ds(pl.multiple_of(g * Q_BLOCK, Q_BLOCK), Q_BLOCK)
            else:
                res, blk = _div_pow2(g, nb), _mod_pow2(g, nb)
                idx = pl.ds(_mod_pow2(res, 4) * quarter + blk * (Q_BLOCK * d // 4)
                            + _div_pow2(res, 4), Q_BLOCK, stride=d // 4)
            acc_ref[br, idx, :] = jnp.where(low, o0, o1)
            l_ref[br, idx, :] = pltpu.roll(jnp.where(low, o1, o0), HEAD_DIM, 1)
            m_ref[br, idx, :] = jnp.where(low, m0, m1)

    prep(0)
    score(0, 0)
    for br in range(len(BRANCHES)):
        for i in range(1, n_groups):
            finish(br, i - 1)
            score(br, i)
        if br + 1 < len(BRANCHES):
            prep(br + 1)
            finish(br, n_groups - 1)
            score(br + 1, 0)
        else:
            finish(br, n_groups - 1)

    def merge(c, carry):
        de0 = c * PREP_ROWS
        dil = pl.ds(pl.multiple_of(de0, PREP_ROWS), PREP_ROWS)
        nat = pl.ds(_div_pow2(de0, quarter) + _mod_pow2(de0, quarter) * 4, PREP_ROWS, stride=4)
        ms = (m_ref[0, nat, :], m_ref[1, dil, :], m_ref[2, dil, :])
        m_all = jnp.maximum(jnp.maximum(ms[0], ms[1]), ms[2])
        ws = [jnp.exp2(m - m_all) for m in ms]
        num = (acc_ref[0, nat, :] * ws[0] + acc_ref[1, dil, :] * ws[1] + acc_ref[2, dil, :] * ws[2])
        den = l_ref[0, nat, :] * ws[0] + l_ref[1, dil, :] * ws[1] + l_ref[2, dil, :] * ws[2]
        acc_ref[0, nat, :] = num / den
        return carry

    for c in range(seq_len // PREP_ROWS):
        merge(c, 0)

    rows_per = 512

    def fin(i, c):
        sl = pl.ds(pl.multiple_of(i * rows_per, rows_per), rows_per)
        o_ref[0, sl, :] = acc_ref[0, sl, :].astype(o_ref.dtype)
        return c

    lax.fori_loop(0, seq_len // rows_per, fin, 0)


def _attn(q, k, v, r0):
    b, s, _ = q.shape
    n_br = len(BRANCHES)
    blk = pl.BlockSpec((1, s, LANES), lambda hp, i: (i, 0, hp))
    state = pltpu.VMEM((n_br, s, LANES), F32)
    qd = pltpu.VMEM((2, s, LANES), BF16)
    kd = pltpu.VMEM((2, s + Q_BLOCK, LANES), BF16)
    return pl.pallas_call(
        functools.partial(_attn_kernel, seq_len=s),
        grid=(D_ATT // LANES, b),
        in_specs=[blk, blk, blk, _const_spec(r0.shape)],
        out_specs=blk,
        out_shape=jax.ShapeDtypeStruct((b, s, D_ATT), BF16),
        scratch_shapes=[pltpu.VMEM((n_br, 2, 2, Q_BLOCK, 2 * Q_BLOCK), F32),
                        state, state, state, qd, qd, kd, kd, kd,
                        pltpu.VMEM((ATTN_GROUP, 2, Q_BLOCK, 2 * Q_BLOCK), F32),
                        pltpu.VMEM((ATTN_GROUP, 2, Q_BLOCK, LANES), F32)],
        compiler_params=pltpu.CompilerParams(dimension_semantics=("arbitrary", "arbitrary"),
                                             vmem_limit_bytes=VMEM_LIMIT_ATTN),
        name="attn",
    )(q, k, v, r0)


def _sample_side_steps(b, q_ref, ktn_ref, vtn_ref, logc_ref, kin_ref, vin_ref, kout_ref, vout_ref,
                       att_ref, *, wb, t_new):
    rows = t_new * N_HEADS
    contract_last = (((1,), (1,)), ((), ()))
    lane = lax.broadcasted_iota(jnp.int32, (1, LANES), 1)
    keep = lane < LANES - t_new
    n_tiles = wb // LANES
    st = {}

    def scores():
        shift = lax.bitwise_and(LANES - t_new * b, LANES - 1)
        st["ktn"] = pltpu.roll(ktn_ref[0], shift, 1)
        st["vtn"] = pltpu.roll(vtn_ref[0], shift, 1)
        q_tile = q_ref[0] * (HEAD_DIM ** -0.5)
        per_tile = 8 // t_new
        q = q_tile[0:t_new]
        for i in range(1, per_tile):
            q = jnp.where(lax.rem(b, per_tile) == i, q_tile[i * t_new:(i + 1) * t_new], q)
        qrep = jnp.broadcast_to(q[:, None, :], (t_new, N_HEADS, D_ATT)).reshape(rows, D_ATT)
        row_h = lax.broadcasted_iota(jnp.int32, (rows, D_ATT), 0) % N_HEADS
        col_h = lax.broadcasted_iota(jnp.int32, (rows, D_ATT), 1) // HEAD_DIM
        st["own"] = row_h == col_h
        qbd = jnp.where(st["own"], qrep, 0.0).astype(BF16)
        st["s"] = (jnp.dot(qbd, kin_ref[...].astype(BF16), preferred_element_type=F32)
                   + logc_ref[:, :wb])
        st["sn"] = (jnp.dot(qbd, st["ktn"].astype(BF16), preferred_element_type=F32)
                    + logc_ref[:, wb:])

    def softmax():
        s, sn = st["s"], st["sn"]
        m = jnp.maximum(jnp.max(s, axis=-1, keepdims=True), jnp.max(sn, axis=-1, keepdims=True))
        p = jnp.exp(s - m)
        pn = jnp.exp(sn - m)
        st["l"] = jnp.sum(p, axis=-1, keepdims=True) + jnp.sum(pn, axis=-1, keepdims=True)
        st["p"], st["pn"] = p.astype(BF16), pn.astype(BF16)

    def values():
        o = (lax.dot_general(st["p"], vin_ref[...].astype(BF16), contract_last,
                             preferred_element_type=F32)
             + lax.dot_general(st["pn"], st["vtn"].astype(BF16), contract_last,
                               preferred_element_type=F32))
        o = jnp.where(st["own"], o, 0.0) / st["l"]
        att_ref[0] = jnp.sum(o.reshape(t_new, N_HEADS, D_ATT), axis=1)

    def shift_tiles(src_ref, new_key, dst_ref, j0, j1):
        def run():
            cur = st.get(("cur", new_key))
            if cur is None:
                cur = pltpu.roll(src_ref[:, 0:LANES], LANES - t_new, 1)
            for j in range(j0, j1):
                if j + 1 < n_tiles:
                    nxt = pltpu.roll(src_ref[:, (j + 1) * LANES:(j + 2) * LANES], LANES - t_new, 1)
                else:
                    nxt = pltpu.roll(st[new_key], LANES - t_new, 1)
                dst_ref[:, j * LANES:(j + 1) * LANES] = jnp.where(keep, cur, nxt)
                cur = nxt
            st[("cur", new_key)] = cur
        return run

    per = SIDE_TILES_PER_STEP
    k_steps = [scores] + [shift_tiles(kin_ref, "ktn", kout_ref, j0, min(j0 + per, n_tiles))
                          for j0 in range(0, n_tiles, per)]
    v_steps = [softmax, values] + [shift_tiles(vin_ref, "vtn", vout_ref, j0, min(j0 + per, n_tiles))
                                   for j0 in range(0, n_tiles, per)]
    return k_steps, v_steps


def _ffn_body(x_ref, att_ref, sg_ref, wo_ref, g2_ref, wg_ref, wu_ref, wd_ref, gf_ref,
              y_ref, a_ref, side_steps=()):
    side = list(side_steps)
    assert len(side) < N_FF_CHUNKS
    tm = x_ref.shape[0]
    halves = [slice(0, tm // 2), slice(tm // 2, tm)] if tm >= 2 * FFN_MIN_HALF else [slice(0, tm)]

    def gate_up(rows, c, h):
        cols = slice(c * FF_CHUNK, (c + 1) * FF_CHUNK)
        g = jnp.dot(h, wg_ref[:, cols], preferred_element_type=F32)
        u = jnp.dot(h, wu_ref[:, cols], preferred_element_type=F32)
        a_ref[rows, cols] = (g * jax.nn.sigmoid(g) * u).astype(BF16)

    x1s, hs = [], []
    for rows in halves:
        x1 = (x_ref[rows, :]
              + jnp.dot(att_ref[rows, :].astype(BF16), wo_ref[:D_ATT, :], preferred_element_type=F32)
              + jnp.dot(sg_ref[rows, :], wo_ref[D_ATT:, :], preferred_element_type=F32))
        ms = jnp.mean(x1 * x1, axis=-1, keepdims=True)
        x1s.append(x1)
        hs.append((x1 * lax.rsqrt(ms + EPS) * g2_ref[...]).astype(BF16))
    for rows, h in zip(halves, hs):
        gate_up(rows, 0, h)
    h = hs[0] if len(hs) == 1 else jnp.concatenate(hs, axis=0)
    for c in range(1, N_FF_CHUNKS):
        if side:
            side.pop(0)()
        gate_up(slice(0, tm), c, h)
    for rows, x1 in zip(halves, x1s):
        x2 = x1 + jnp.dot(a_ref[rows, :], wd_ref[...], preferred_element_type=F32)
        ms2 = jnp.mean(x2 * x2, axis=-1, keepdims=True)
        y_ref[rows, :] = x2 * lax.rsqrt(ms2 + EPS) * gf_ref[...]


def _ffn_kernel(*refs):
    _ffn_body(*refs)


def _ffn_side_kernel(x_ref, att_ref, sg_ref, wo_ref, g2_ref, wg_ref, wu_ref, wd_ref, gf_ref,
                     q_ref, ktn_ref, vtn_ref, logc_ref, kt_hbm, vt_hbm,
                     y_ref, atts_ref, kto_hbm, vto_hbm,
                     a_ref, kin_ref, vin_ref, kout_ref, vout_ref, sems, *, wb, t_new):
    i = pl.program_id(0)
    last = pl.num_programs(0) - 1

    def fetch(j):
        return (pltpu.make_async_copy(kt_hbm.at[j], kin_ref, sems.at[0]),
                pltpu.make_async_copy(vt_hbm.at[j], vin_ref, sems.at[1]))

    def flush(j):
        return (pltpu.make_async_copy(kout_ref, kto_hbm.at[j], sems.at[2]),
                pltpu.make_async_copy(vout_ref, vto_hbm.at[j], sems.at[3]))

    @pl.when(i == 0)
    def _():
        for cp in fetch(0):
            cp.start()

    for cp in fetch(i):
        cp.wait()

    @pl.when(i > 0)
    def _():
        for cp in flush(i - 1):
            cp.wait()

    def then_swap(step, which):
        def run():
            step()
            flush(i)[which].start()
            fetch(jnp.minimum(i + 1, last))[which].start()
        return run

    k_steps, v_steps = _sample_side_steps(i, q_ref, ktn_ref, vtn_ref, logc_ref, kin_ref, vin_ref,
                                          kout_ref, vout_ref, atts_ref, wb=wb, t_new=t_new)
    k_steps[-1] = then_swap(k_steps[-1], 0)
    v_steps[-1] = then_swap(v_steps[-1], 1)
    steps = k_steps + v_steps
    _ffn_body(x_ref, att_ref, sg_ref, wo_ref, g2_ref, wg_ref, wu_ref, wd_ref, gf_ref,
              y_ref, a_ref, side_steps=steps)

    @pl.when(i == last)
    def _():
        for cp in fetch(i) + flush(i):
            cp.wait()


def _ffn_specs(tm, weights):
    row = lambda shape: pl.BlockSpec(shape, lambda i: (i, 0))
    return ([row((tm, D_MODEL)), row((tm, D_ATT)), row((tm, D_SGU))]
            + [_const_spec(w.shape) for w in weights]), row((tm, D_MODEL))


def _ffn(x, att, sg, weights, *, tm):
    t = x.shape[0]
    in_specs, out_spec = _ffn_specs(tm, weights)
    return pl.pallas_call(
        _ffn_kernel,
        grid=(t // tm,),
        in_specs=in_specs,
        out_specs=out_spec,
        out_shape=jax.ShapeDtypeStruct((t, D_MODEL), F32),
        scratch_shapes=[pltpu.VMEM((tm, D_FF), BF16)],
        compiler_params=pltpu.CompilerParams(dimension_semantics=("arbitrary",),
                                             vmem_limit_bytes=VMEM_LIMIT),
        name="ffn",
    )(x, att, sg, *weights)


def _ffn_with_sample_side(x, att, sg, weights, q_s, ktn, vtn, logc, kt, vt, *, tm, t_new):
    t = x.shape[0]
    bd, _, wb = kt.shape
    assert t // tm == bd
    in_specs, out_spec = _ffn_specs(tm, weights)
    any_spec = pl.BlockSpec(memory_space=pl.ANY)
    window = pltpu.VMEM((D_ATT, wb), F32)
    return pl.pallas_call(
        functools.partial(_ffn_side_kernel, wb=wb, t_new=t_new),
        grid=(bd,),
        in_specs=in_specs + [pl.BlockSpec((1, 8, D_ATT), lambda i: (i // (8 // t_new), 0, 0)),
                             _const_spec(ktn.shape), _const_spec(vtn.shape),
                             _const_spec(logc.shape), any_spec, any_spec],
        out_specs=[out_spec, pl.BlockSpec((1, t_new, D_ATT), lambda i: (i, 0, 0)),
                   any_spec, any_spec],
        out_shape=[jax.ShapeDtypeStruct((t, D_MODEL), F32),
                   jax.ShapeDtypeStruct((bd, t_new, D_ATT), F32),
                   jax.ShapeDtypeStruct(kt.shape, F32), jax.ShapeDtypeStruct(vt.shape, F32)],
        scratch_shapes=[pltpu.VMEM((tm, D_FF), BF16), window, window, window, window,
                        pltpu.SemaphoreType.DMA((4,))],
        compiler_params=pltpu.CompilerParams(dimension_semantics=("arbitrary",),
                                             vmem_limit_bytes=VMEM_LIMIT_FUSED),
        name="ffn_side",
    )(x, att, sg, *weights, q_s, ktn, vtn, logc, kt, vt)


def _rel_bucket(dist):
    max_exact = N_BUCKETS // 2
    df = jnp.maximum(dist, 1).astype(F32)
    large = max_exact + (jnp.log(df / max_exact) / math.log(MAX_DISTANCE / max_exact)
                         * (N_BUCKETS - max_exact)).astype(jnp.int32)
    large = jnp.minimum(large, N_BUCKETS - 1)
    return jnp.where(dist < max_exact, dist, large)


def _branch_bias_reversed(rel_bias, w, d):
    nj = w // d + 1
    dist = (nj - 1 - jnp.arange(nj, dtype=jnp.int32)) * d
    return rel_bias[_rel_bucket(dist)].T.astype(F32)


def _prompt_bias_rows(rev):
    rows = [jnp.pad(r, ((0, 0), (0, 2 * Q_BLOCK - r.shape[1])), constant_values=NEG_INF)
            for r in rev]
    return jnp.stack(rows) * LOG2E


def _sample_bias_table(rev, wb, t_new):
    width = wb + LANES
    per_branch = []
    for (w, d), r in zip(BRANCHES, rev):
        nj = r.shape[1]
        if d > 1:
            fill = jnp.full((N_HEADS, nj, d - 1), NEG_INF, F32)
            r = jnp.concatenate([r[:, :, None], fill], axis=2).reshape(N_HEADS, nj * d)
        rows = []
        for t in range(t_new):
            base = wb + t - (nj - 1) * d
            rows.append(jnp.pad(r, ((0, 0), (base, width - base - nj * d)),
                                constant_values=NEG_INF))
        per_branch.append(jnp.stack(rows))
    x = jnp.stack(per_branch)
    mx = jnp.max(x, axis=0)
    logc = mx + jnp.log(jnp.sum(jnp.exp(x - mx), axis=0))
    return logc.reshape(t_new * N_HEADS, width)


def kernel(x_prompt, x_sample, cache_k_win, cache_v_win, norm1_g, w_in, sgu_ln_g, sgu_ln_b,
           sgu_w, sgu_b, w_out, norm2_g, w_gate, w_up, w_down, rel_bias, final_g):
    depth = w_in.shape[0]
    assert depth == 1
    b, s, _ = x_prompt.shape
    bd, t_new, _ = x_sample.shape
    wb = cache_k_win.shape[2]
    assert bd * t_new == CHUNK and s % (Q_BLOCK * 16) == 0 and wb == WINDOW
    assert (s // Q_BLOCK) % ATTN_GROUP == 0 and s % PREP_ROWS == 0

    l = 0
    w_in_b = w_in[l].astype(BF16)
    g1 = norm1_g[l][None]
    g2 = norm2_g[l][None]
    gf = final_g[None]
    ln_g = sgu_ln_g[l][None]
    ln_b = sgu_ln_b[l][None]

    causal = jnp.tril(jnp.ones((CHUNK, CHUNK), F32))
    wm = sgu_w[l] * causal
    mix_p = wm.astype(BF16)
    mixb_p = jnp.repeat(sgu_b[l].T, HEAD_DIM, axis=1)
    rep = jnp.tile(jnp.eye(t_new, dtype=F32), (bd, 1))
    same_batch = jnp.kron(jnp.eye(bd, dtype=F32), jnp.ones((t_new, t_new), F32))
    mix_s = (jnp.einsum('it,gts,js->gij', rep, wm[:, :t_new, :t_new], rep)
             * same_batch).astype(BF16)
    mixb_s = jnp.tile(mixb_p[:t_new], (bd, 1))

    rev = [_branch_bias_reversed(rel_bias, w, d) for w, d in BRANCHES]

    xp = x_prompt.reshape(b * s, D_MODEL)
    q, k, v, sg, kt_p, vt_p, wo, wg, wu, wd = _proj(
        xp, g1, w_in_b, ln_g, ln_b, mix_p, mixb_p, tm=PROJ_TM, emit_vn=False, emit_t=True, seq_len=s,
        cast=(w_out[l], w_gate[l], w_up[l], w_down[l]))
    att = _attn(q.reshape(b, s, D_ATT), k.reshape(b, s, D_ATT), v.reshape(b, s, D_ATT),
                _prompt_bias_rows(rev))
    nw = min(WINDOW, s)
    new_k_p = kt_p.reshape(1, b, N_HEADS, HEAD_DIM, nw).transpose(0, 1, 4, 2, 3)
    new_v_p = vt_p.reshape(1, b, N_HEADS, HEAD_DIM, nw).transpose(0, 1, 4, 2, 3)

    xs = x_sample.reshape(bd * t_new, D_MODEL)
    qs, _, _, sgs, vn_s, kt_n, vt_n = _proj(xs, g1, w_in_b, ln_g, ln_b, mix_s, mixb_s,
                                            tm=CHUNK, emit_vn=True, emit_t=True, seq_len=CHUNK)
    kt_c = cache_k_win[l].transpose(0, 2, 3, 1).reshape(bd, D_ATT, wb)
    vt_c = cache_v_win[l].transpose(0, 2, 3, 1).reshape(bd, D_ATT, wb)

    logc = _sample_bias_table(rev, wb, t_new)
    weights = (wo, g2, wg, wu, wd, gf)
    y_prompt, att_s, kt_o, vt_o = _ffn_with_sample_side(
        xp, att.reshape(b * s, D_ATT), sg, weights, qs.reshape(bd * t_new // 8, 8, D_ATT),
        kt_n, vt_n, logc, kt_c, vt_c, tm=(b * s) // bd, t_new=t_new)
    y_prompt = y_prompt.reshape(b, s, D_MODEL)
    y_sample = _ffn(xs, att_s.reshape(bd * t_new, D_ATT), sgs, weights,
                    tm=CHUNK).reshape(bd, t_new, D_MODEL)
    new_k_s = kt_o.reshape(1, bd, N_HEADS, HEAD_DIM, wb).transpose(0, 1, 4, 2, 3)
    new_v_s = vt_o.reshape(1, bd, N_HEADS, HEAD_DIM, wb).transpose(0, 1, 4, 2, 3)
    sgu_v = vn_s.reshape(1, bd, t_new, D_SGU)

    return (y_prompt, y_sample, new_k_p, new_v_p, new_k_s, new_v_s, sgu_v)
```

```python
import functools
import math

import jax
import jax.numpy as jnp
from jax import lax
from jax.experimental import pallas as pl
from jax.experimental.pallas import tpu as pltpu

D_MODEL = 1024
N_HEADS = 8
HEAD_DIM = 64
D_ATT = N_HEADS * HEAD_DIM
N_GROUPS = 8
D_SGU = 512
CHUNK = 128
BRANCHES = ((128, 1), (512, 4), (2048, 16))
WINDOW = 2048
Q_BLOCK = 128
N_BUCKETS = 32
MAX_DISTANCE = WINDOW
D_FF = 2816
EPS = 1e-6
NEG_INF = -1e30
LOG2E = math.log2(math.e)

LANES = 128
FF_CHUNK = 256
N_FF_CHUNKS = D_FF // FF_CHUNK
MIB = 1024 * 1024
V7X_VMEM_BYTES = 64 * MIB
VMEM_LIMIT = 48 * MIB
VMEM_LIMIT_FUSED = 60 * MIB
VMEM_LIMIT_ATTN = 56 * MIB
VMEM_LIMIT_PROJ = 56 * MIB
assert max(VMEM_LIMIT, VMEM_LIMIT_FUSED, VMEM_LIMIT_ATTN, VMEM_LIMIT_PROJ) < V7X_VMEM_BYTES
PROJ_TM = 1024
SIDE_TILES_PER_STEP = 8
ATTN_GROUP = 8
PREP_ROWS = 256
BF16_ROWS = 16
FFN_MIN_HALF = 128

F32 = jnp.float32
BF16 = jnp.bfloat16


def _const_spec(shape):
    nd = len(shape)
    return pl.BlockSpec(shape, lambda *_: (0,) * nd, pipeline_mode=pl.Buffered(1))


def _proj_kernel(x_ref, g1_ref, w_ref, lng_ref, lnb_ref, mix_ref, mixb_ref, *refs,
                 tm, emit_vn, emit_t, n_cast):
    cast_in, outs = refs[:n_cast], refs[n_cast:]
    cast_out = outs[len(outs) - n_cast:]
    q_ref, k_ref, v_ref, sg_ref = outs[:4]
    rest = outs[4:len(outs) - n_cast]
    lane = lax.broadcasted_iota(jnp.int32, (1, LANES), 1)
    low = lane < HEAD_DIM

    x = x_ref[...]
    xg = (x * g1_ref[...]).astype(BF16)
    r = lax.rsqrt(jnp.mean(x * x, axis=-1, keepdims=True) + EPS)

    def proj(c0):
        return jnp.dot(xg, w_ref[:, c0:c0 + D_ATT], preferred_element_type=F32) * r

    vg = proj(3 * D_ATT + D_SGU)
    u = proj(3 * D_ATT)
    mu = jnp.mean(vg, axis=-1, keepdims=True)
    dv = vg - mu
    var = jnp.mean(dv * dv, axis=-1, keepdims=True)
    vn = dv * lax.rsqrt(var + EPS) * lng_ref[...] + lnb_ref[...]
    if emit_vn:
        rest[0][...] = vn
    k = proj(D_ATT)
    k_ref[...] = k
    if emit_t:
        kt_ref, vt_ref = rest[-2:]
        kt_ref[0] = k.T

    def gating(c0):
        for s in range(D_SGU // LANES):
            cols = slice(s * LANES, (s + 1) * LANES)
            slab = vn[c0:c0 + CHUNK, cols]
            lo = jnp.where(low, slab, 0.0).astype(BF16)
            hi = jnp.where(low, 0.0, slab).astype(BF16)
            gate = (jnp.dot(mix_ref[2 * s], lo, preferred_element_type=F32)
                    + jnp.dot(mix_ref[2 * s + 1], hi, preferred_element_type=F32)
                    + mixb_ref[:, cols])
            sg_ref[c0:c0 + CHUNK, cols] = (u[c0:c0 + CHUNK, cols] * gate).astype(BF16)

    chunks = list(range(0, tm, CHUNK))
    for c0 in chunks[:len(chunks) // 2]:
        gating(c0)
    v = proj(2 * D_ATT)
    v_ref[...] = v
    if emit_t:
        vt_ref[0] = v.T
    for src, dst in zip(cast_in, cast_out):
        dst[...] = src[...].astype(BF16)
    for c0 in chunks[len(chunks) // 2:]:
        gating(c0)
    q_ref[...] = proj(0)


def _proj(x, g1, w_in, ln_g, ln_b, mix, mixb, *, tm, emit_vn, emit_t, seq_len=None, cast=()):
    t = x.shape[0]
    n_tiles = t // tm
    row = lambda shape: pl.BlockSpec(shape, lambda i: (i, 0))
    out_shape = [jax.ShapeDtypeStruct((t, D_ATT), F32)] * 3 + [jax.ShapeDtypeStruct((t, D_SGU), BF16)]
    out_specs = [row((tm, D_ATT))] * 3 + [row((tm, D_SGU))]
    if emit_vn:
        out_shape.append(jax.ShapeDtypeStruct((t, D_SGU), F32))
        out_specs.append(row((tm, D_SGU)))
    if emit_t:
        tiles_per_seq = seq_len // tm
        win = min(WINDOW, seq_len)
        first = tiles_per_seq - win // tm

        def t_map(i):
            return (i // tiles_per_seq, 0, jnp.maximum(i % tiles_per_seq - first, 0))

        for _ in range(2):
            out_shape.append(jax.ShapeDtypeStruct((t // seq_len, D_ATT, win), F32))
            out_specs.append(pl.BlockSpec((1, D_ATT, tm), t_map))
    cast_specs = []
    for w in cast:
        steps = max(n for n in range(1, n_tiles + 1)
                    if w.shape[0] % n == 0 and (w.shape[0] // n) % BF16_ROWS == 0)
        spec = pl.BlockSpec((w.shape[0] // steps, w.shape[1]),
                            lambda i, steps=steps: (jnp.minimum(i, steps - 1), 0))
        cast_specs.append(spec)
        out_shape.append(jax.ShapeDtypeStruct(w.shape, BF16))
        out_specs.append(spec)
    kern = functools.partial(_proj_kernel, tm=tm, emit_vn=emit_vn, emit_t=emit_t, n_cast=len(cast))
    return pl.pallas_call(
        kern,
        grid=(n_tiles,),
        in_specs=[row((tm, D_MODEL)), _const_spec(g1.shape), _const_spec(w_in.shape),
                  _const_spec(ln_g.shape), _const_spec(ln_b.shape), _const_spec(mix.shape),
                  _const_spec(mixb.shape)] + cast_specs,
        out_specs=out_specs,
        out_shape=out_shape,
        compiler_params=pltpu.CompilerParams(dimension_semantics=("arbitrary",),
                                             vmem_limit_bytes=VMEM_LIMIT_PROJ),
        name="proj_cast" if cast else "proj",
    )(x, g1, w_in, ln_g, ln_b, mix, mixb, *cast)


def _div_pow2(x, n):
    assert n & (n - 1) == 0
    return lax.shift_right_logical(x, n.bit_length() - 1)


def _mod_pow2(x, n):
    assert n & (n - 1) == 0
    return lax.bitwise_and(x, n - 1)


def _attn_kernel(q_ref, k_ref, v_ref, r0_ref, o_ref, tab_ref, acc_ref, m_ref, l_ref,
                 q0_ref, q1_ref, kd_ref, v0_ref, v1_ref, s_ref, mx_ref, *, seq_len):
    hp = pl.program_id(0)
    lane = lax.broadcasted_iota(jnp.int32, (1, LANES), 1)
    low = lane < HEAD_DIM
    contract_last = (((1,), (1,)), ((), ()))
    neg = NEG_INF * LOG2E
    quarter = seq_len // 4
    lo_f = jnp.where(low, 1.0, 0.0).astype(F32)
    hi_f = 1.0 - lo_f
    q_scale = HEAD_DIM ** -0.5 * LOG2E

    @pl.when(pl.program_id(1) == 0)
    def _():
        col = lax.broadcasted_iota(jnp.int32, (Q_BLOCK, 2 * Q_BLOCK), 1)
        for br in range(len(BRANCHES)):
            for h in range(2):
                base = jnp.broadcast_to(r0_ref[br, pl.ds(2 * hp + h, 1), :],
                                        (Q_BLOCK, 2 * Q_BLOCK))
                t = pltpu.roll(base, 0, 1, stride=1, stride_axis=0)
                tab_ref[br, 0, h] = t
                tab_ref[br, 1, h] = jnp.where(col >= Q_BLOCK, t, neg)
        zeros = jnp.zeros((Q_BLOCK, LANES), BF16)
        for slot in range(2):
            kd_ref[slot, 0:Q_BLOCK, :] = zeros
            v0_ref[slot, 0:Q_BLOCK, :] = zeros
            v1_ref[slot, 0:Q_BLOCK, :] = zeros

    def strided(start, n, d):
        return pl.ds(start, n) if d == 1 else pl.ds(start, n, stride=d)

    dilation = [d for _, d in BRANCHES]
    n_groups = seq_len // Q_BLOCK // ATTN_GROUP

    def prep(br):
        d, slot = dilation[br], br % 2
        sub_len = seq_len // d

        def chunk(c, carry):
            de0 = c * PREP_ROWS
            dst = pl.ds(pl.multiple_of(de0, PREP_ROWS), PREP_ROWS)
            dstp = pl.ds(pl.multiple_of(de0 + Q_BLOCK, Q_BLOCK), PREP_ROWS)
            if d <= 4:
                src = strided(_div_pow2(de0, sub_len) + _mod_pow2(de0, sub_len) * d, PREP_ROWS, d)
                qq, kk, vv = q_ref[0, src, :], k_ref[0, src, :], v_ref[0, src, :]
            else:
                res, m0 = _div_pow2(de0, sub_len), _mod_pow2(de0, sub_len)
                src = pl.ds(_mod_pow2(res, 4) * quarter + m0 * (d // 4) + _div_pow2(res, 4),
                            PREP_ROWS, stride=d // 4)
                qq, kk, vv = acc_ref[2, src, :], m_ref[2, src, :], l_ref[2, src, :]
            if d == 4:
                acc_ref[2, dst, :] = qq
                m_ref[2, dst, :] = kk
                l_ref[2, dst, :] = vv
            q0_ref[slot, dst, :] = (qq * (lo_f * q_scale)).astype(BF16)
            q1_ref[slot, dst, :] = (qq * (hi_f * q_scale)).astype(BF16)
            kd_ref[slot, dstp, :] = kk.astype(BF16)
            v0_ref[slot, dstp, :] = (vv * lo_f + hi_f).astype(BF16)
            v1_ref[slot, dstp, :] = (vv * hi_f + lo_f).astype(BF16)
            return carry

        lax.fori_loop(0, seq_len // PREP_ROWS, chunk, 0, unroll=4)

    def score(br, i):
        slot, nb = br % 2, seq_len // dilation[br] // Q_BLOCK
        for u in range(ATTN_GROUP):
            g = i * ATTN_GROUP + u
            rq = pl.ds(pl.multiple_of(g * Q_BLOCK, Q_BLOCK), Q_BLOCK)
            rk = pl.ds(pl.multiple_of(g * Q_BLOCK, Q_BLOCK), 2 * Q_BLOCK)
            first = jnp.where(_mod_pow2(g, nb) == 0, 1, 0)
            kb = kd_ref[slot, rk, :]
            for h, qh_ref in enumerate((q0_ref, q1_ref)):
                s = lax.dot_general(qh_ref[slot, rq, :], kb, contract_last,
                                    preferred_element_type=F32)
                s = s + tab_ref[br, first, h]
                s_ref[u, h] = s
                mx_ref[u, h] = jnp.broadcast_to(jnp.max(s, axis=-1, keepdims=True),
                                                (Q_BLOCK, LANES))

    def finish(br, i):
        d, slot = dilation[br], br % 2
        nb = seq_len // d // Q_BLOCK
        for u in range(ATTN_GROUP):
            g = i * ATTN_GROUP + u
            rk = pl.ds(pl.multiple_of(g * Q_BLOCK, Q_BLOCK), 2 * Q_BLOCK)
            outs = []
            for h, vh_ref in enumerate((v0_ref, v1_ref)):
                m = mx_ref[u, h]
                p = jnp.concatenate([jnp.exp2(s_ref[u, h, :, :LANES] - m),
                                     jnp.exp2(s_ref[u, h, :, LANES:] - m)], axis=1)
                outs.append((jnp.dot(p.astype(BF16), vh_ref[slot, rk, :],
                                     preferred_element_type=F32), m))
            (o0, m0), (o1, m1) = outs
            if d <= 4:
                idx = pl.ds(pl.multiple_of(g * Q_BLOCK, Q_BLOCK), Q_BLOCK)
            else:
                res, blk = _div_pow2(g, nb), _mod_pow2(g, nb)
                idx = pl.ds(_mod_pow2(res, 4) * quarter + blk * (Q_BLOCK * d // 4)
                            + _div_pow2(res, 4), Q_BLOCK, stride=d // 4)
            acc_ref[br, idx, :] = jnp.where(low, o0, o1)
            l_ref[br, idx, :] = pltpu.roll(jnp.where(low, o1, o0), HEAD_DIM, 1)
            m_ref[br, idx, :] = jnp.where(low, m0, m1)

    prep(0)
    score(0, 0)
    for br in range(len(BRANCHES)):
        for i in range(1, n_groups):
            finish(br, i - 1)
            score(br, i)
        if br + 1 < len(BRANCHES):
            prep(br + 1)
            finish(br, n_groups - 1)
            score(br + 1, 0)
        else:
            finish(br, n_groups - 1)

    def merge(c, carry):
        de0 = c * PREP_ROWS
        dil = pl.ds(pl.multiple_of(de0, PREP_ROWS), PREP_ROWS)
        nat = pl.ds(_div_pow2(de0, quarter) + _mod_pow2(de0, quarter) * 4, PREP_ROWS, stride=4)
        ms = (m_ref[0, nat, :], m_ref[1, dil, :], m_ref[2, dil, :])
        m_all = jnp.maximum(jnp.maximum(ms[0], ms[1]), ms[2])
        ws = [jnp.exp2(m - m_all) for m in ms]
        num = (acc_ref[0, nat, :] * ws[0] + acc_ref[1, dil, :] * ws[1] + acc_ref[2, dil, :] * ws[2])
        den = l_ref[0, nat, :] * ws[0] + l_ref[1, dil, :] * ws[1] + l_ref[2, dil, :] * ws[2]
        acc_ref[0, nat, :] = num / den
        return carry

    def cast_out(r0, n):
        o_ref[0, r0:r0 + n, :] = acc_ref[0, r0:r0 + n, :].astype(o_ref.dtype)

    per_class = quarter // PREP_ROWS
    for a in range(per_class):
        for r in range(4):
            merge(r * per_class + a, 0)
        cast_out(a * 4 * PREP_ROWS, 4 * PREP_ROWS)


def _attn(q, k, v, r0):
    b, s, _ = q.shape
    n_br = len(BRANCHES)
    blk = pl.BlockSpec((1, s, LANES), lambda hp, i: (i, 0, hp))
    state = pltpu.VMEM((n_br, s, LANES), F32)
    qd = pltpu.VMEM((2, s, LANES), BF16)
    kd = pltpu.VMEM((2, s + Q_BLOCK, LANES), BF16)
    return pl.pallas_call(
        functools.partial(_attn_kernel, seq_len=s),
        grid=(D_ATT // LANES, b),
        in_specs=[blk, blk, blk, _const_spec(r0.shape)],
        out_specs=blk,
        out_shape=jax.ShapeDtypeStruct((b, s, D_ATT), BF16),
        scratch_shapes=[pltpu.VMEM((n_br, 2, 2, Q_BLOCK, 2 * Q_BLOCK), F32),
                        state, state, state, qd, qd, kd, kd, kd,
                        pltpu.VMEM((ATTN_GROUP, 2, Q_BLOCK, 2 * Q_BLOCK), F32),
                        pltpu.VMEM((ATTN_GROUP, 2, Q_BLOCK, LANES), F32)],
        compiler_params=pltpu.CompilerParams(dimension_semantics=("arbitrary", "arbitrary"),
                                             vmem_limit_bytes=VMEM_LIMIT_ATTN),
        name="attn",
    )(q, k, v, r0)


def _sample_side_steps(b, q_ref, ktn_ref, vtn_ref, logc_ref, kin_ref, vin_ref, kout_ref, vout_ref,
                       att_ref, *, wb, t_new):
    rows = t_new * N_HEADS
    contract_last = (((1,), (1,)), ((), ()))
    lane = lax.broadcasted_iota(jnp.int32, (1, LANES), 1)
    keep = lane < LANES - t_new
    n_tiles = wb // LANES
    st = {}

    def scores():
        shift = lax.bitwise_and(LANES - t_new * b, LANES - 1)
        st["ktn"] = pltpu.roll(ktn_ref[0], shift, 1)
        st["vtn"] = pltpu.roll(vtn_ref[0], shift, 1)
        q_tile = q_ref[0] * (HEAD_DIM ** -0.5)
        per_tile = 8 // t_new
        q = q_tile[0:t_new]
        for i in range(1, per_tile):
            q = jnp.where(lax.rem(b, per_tile) == i, q_tile[i * t_new:(i + 1) * t_new], q)
        qrep = jnp.broadcast_to(q[:, None, :], (t_new, N_HEADS, D_ATT)).reshape(rows, D_ATT)
        row_h = lax.broadcasted_iota(jnp.int32, (rows, D_ATT), 0) % N_HEADS
        col_h = lax.broadcasted_iota(jnp.int32, (rows, D_ATT), 1) // HEAD_DIM
        st["own"] = row_h == col_h
        qbd = jnp.where(st["own"], qrep, 0.0).astype(BF16)
        st["s"] = (jnp.dot(qbd, kin_ref[...].astype(BF16), preferred_element_type=F32)
                   + logc_ref[:, :wb])
        st["sn"] = (jnp.dot(qbd, st["ktn"].astype(BF16), preferred_element_type=F32)
                    + logc_ref[:, wb:])

    def softmax():
        s, sn = st["s"], st["sn"]
        m = jnp.maximum(jnp.max(s, axis=-1, keepdims=True), jnp.max(sn, axis=-1, keepdims=True))
        p = jnp.exp(s - m)
        pn = jnp.exp(sn - m)
        st["l"] = jnp.sum(p, axis=-1, keepdims=True) + jnp.sum(pn, axis=-1, keepdims=True)
        st["p"], st["pn"] = p.astype(BF16), pn.astype(BF16)

    def values():
        o = (lax.dot_general(st["p"], vin_ref[...].astype(BF16), contract_last,
                             preferred_element_type=F32)
             + lax.dot_general(st["pn"], st["vtn"].astype(BF16), contract_last,
                               preferred_element_type=F32))
        o = jnp.where(st["own"], o, 0.0) / st["l"]
        att_ref[0] = jnp.sum(o.reshape(t_new, N_HEADS, D_ATT), axis=1)

    def shift_tiles(src_ref, new_key, dst_ref, j0, j1):
        def run():
            cur = st.get(("cur", new_key))
            if cur is None:
                cur = pltpu.roll(src_ref[:, 0:LANES], LANES - t_new, 1)
            for j in range(j0, j1):
                if j + 1 < n_tiles:
                    nxt = pltpu.roll(src_ref[:, (j + 1) * LANES:(j + 2) * LANES], LANES - t_new, 1)
                else:
                    nxt = pltpu.roll(st[new_key], LANES - t_new, 1)
                dst_ref[:, j * LANES:(j + 1) * LANES] = jnp.where(keep, cur, nxt)
                cur = nxt
            st[("cur", new_key)] = cur
        return run

    per = SIDE_TILES_PER_STEP
    k_steps = [scores] + [shift_tiles(kin_ref, "ktn", kout_ref, j0, min(j0 + per, n_tiles))
                          for j0 in range(0, n_tiles, per)]
    v_steps = [softmax, values] + [shift_tiles(vin_ref, "vtn", vout_ref, j0, min(j0 + per, n_tiles))
                                   for j0 in range(0, n_tiles, per)]
    return k_steps, v_steps


def _ffn_body(x_ref, att_ref, sg_ref, wo_ref, g2_ref, wg_ref, wu_ref, wd_ref, gf_ref,
              y_ref, a_ref, side_steps=()):
    side = list(side_steps)
    assert len(side) < N_FF_CHUNKS
    tm = x_ref.shape[0]
    halves = [slice(0, tm // 2), slice(tm // 2, tm)] if tm >= 2 * FFN_MIN_HALF else [slice(0, tm)]

    def gate_up(rows, c, h):
        cols = slice(c * FF_CHUNK, (c + 1) * FF_CHUNK)
        g = jnp.dot(h, wg_ref[:, cols], preferred_element_type=F32)
        u = jnp.dot(h, wu_ref[:, cols], preferred_element_type=F32)
        a_ref[rows, cols] = (g * jax.nn.sigmoid(g) * u).astype(BF16)

    x1s, hs = [], []
    for rows in halves:
        x1 = (x_ref[rows, :]
              + jnp.dot(att_ref[rows, :].astype(BF16), wo_ref[:D_ATT, :], preferred_element_type=F32)
              + jnp.dot(sg_ref[rows, :], wo_ref[D_ATT:, :], preferred_element_type=F32))
        ms = jnp.mean(x1 * x1, axis=-1, keepdims=True)
        x1s.append(x1)
        hs.append((x1 * lax.rsqrt(ms + EPS) * g2_ref[...]).astype(BF16))
    for rows, h in zip(halves, hs):
        gate_up(rows, 0, h)
    h = hs[0] if len(hs) == 1 else jnp.concatenate(hs, axis=0)
    for c in range(1, N_FF_CHUNKS):
        if side:
            side.pop(0)()
        gate_up(slice(0, tm), c, h)
    for rows, x1 in zip(halves, x1s):
        x2 = x1 + jnp.dot(a_ref[rows, :], wd_ref[...], preferred_element_type=F32)
        ms2 = jnp.mean(x2 * x2, axis=-1, keepdims=True)
        y_ref[rows, :] = x2 * lax.rsqrt(ms2 + EPS) * gf_ref[...]


def _ffn_kernel(*refs):
    _ffn_body(*refs)


def _ffn_side_kernel(x_ref, att_ref, sg_ref, wo_ref, g2_ref, wg_ref, wu_ref, wd_ref, gf_ref,
                     q_ref, ktn_ref, vtn_ref, logc_ref, kt_hbm, vt_hbm,
                     y_ref, atts_ref, kto_hbm, vto_hbm,
                     a_ref, kin_ref, vin_ref, kout_ref, vout_ref, sems, *, wb, t_new):
    i = pl.program_id(0)
    last = pl.num_programs(0) - 1

    def fetch(j):
        return (pltpu.make_async_copy(kt_hbm.at[j], kin_ref, sems.at[0]),
                pltpu.make_async_copy(vt_hbm.at[j], vin_ref, sems.at[1]))

    def flush(j):
        return (pltpu.make_async_copy(kout_ref, kto_hbm.at[j], sems.at[2]),
                pltpu.make_async_copy(vout_ref, vto_hbm.at[j], sems.at[3]))

    @pl.when(i == 0)
    def _():
        for cp in fetch(0):
            cp.start()

    for cp in fetch(i):
        cp.wait()

    @pl.when(i > 0)
    def _():
        for cp in flush(i - 1):
            cp.wait()

    def then_swap(step, which):
        def run():
            step()
            flush(i)[which].start()
            fetch(jnp.minimum(i + 1, last))[which].start()
        return run

    k_steps, v_steps = _sample_side_steps(i, q_ref, ktn_ref, vtn_ref, logc_ref, kin_ref, vin_ref,
                                          kout_ref, vout_ref, atts_ref, wb=wb, t_new=t_new)
    k_steps[-1] = then_swap(k_steps[-1], 0)
    v_steps[-1] = then_swap(v_steps[-1], 1)
    steps = k_steps + v_steps
    _ffn_body(x_ref, att_ref, sg_ref, wo_ref, g2_ref, wg_ref, wu_ref, wd_ref, gf_ref,
              y_ref, a_ref, side_steps=steps)

    @pl.when(i == last)
    def _():
        for cp in fetch(i) + flush(i):
            cp.wait()


def _ffn_specs(tm, weights):
    row = lambda shape: pl.BlockSpec(shape, lambda i: (i, 0))
    return ([row((tm, D_MODEL)), row((tm, D_ATT)), row((tm, D_SGU))]
            + [_const_spec(w.shape) for w in weights]), row((tm, D_MODEL))


def _ffn(x, att, sg, weights, *, tm):
    t = x.shape[0]
    in_specs, out_spec = _ffn_specs(tm, weights)
    return pl.pallas_call(
        _ffn_kernel,
        grid=(t // tm,),
        in_specs=in_specs,
        out_specs=out_spec,
        out_shape=jax.ShapeDtypeStruct((t, D_MODEL), F32),
        scratch_shapes=[pltpu.VMEM((tm, D_FF), BF16)],
        compiler_params=pltpu.CompilerParams(dimension_semantics=("arbitrary",),
                                             vmem_limit_bytes=VMEM_LIMIT),
        name="ffn",
    )(x, att, sg, *weights)


def _ffn_with_sample_side(x, att, sg, weights, q_s, ktn, vtn, logc, kt, vt, *, tm, t_new):
    t = x.shape[0]
    bd, _, wb = kt.shape
    assert t // tm == bd
    in_specs, out_spec = _ffn_specs(tm, weights)
    any_spec = pl.BlockSpec(memory_space=pl.ANY)
    window = pltpu.VMEM((D_ATT, wb), F32)
    return pl.pallas_call(
        functools.partial(_ffn_side_kernel, wb=wb, t_new=t_new),
        grid=(bd,),
        in_specs=in_specs + [pl.BlockSpec((1, 8, D_ATT), lambda i: (i // (8 // t_new), 0, 0)),
                             _const_spec(ktn.shape), _const_spec(vtn.shape),
                             _const_spec(logc.shape), any_spec, any_spec],
        out_specs=[out_spec, pl.BlockSpec((1, t_new, D_ATT), lambda i: (i, 0, 0)),
                   any_spec, any_spec],
        out_shape=[jax.ShapeDtypeStruct((t, D_MODEL), F32),
                   jax.ShapeDtypeStruct((bd, t_new, D_ATT), F32),
                   jax.ShapeDtypeStruct(kt.shape, F32), jax.ShapeDtypeStruct(vt.shape, F32)],
        scratch_shapes=[pltpu.VMEM((tm, D_FF), BF16), window, window, window, window,
                        pltpu.SemaphoreType.DMA((4,))],
        compiler_params=pltpu.CompilerParams(dimension_semantics=("arbitrary",),
                                             vmem_limit_bytes=VMEM_LIMIT_FUSED),
        name="ffn_side",
    )(x, att, sg, *weights, q_s, ktn, vtn, logc, kt, vt)


def _rel_bucket(dist):
    max_exact = N_BUCKETS // 2
    df = jnp.maximum(dist, 1).astype(F32)
    large = max_exact + (jnp.log(df / max_exact) / math.log(MAX_DISTANCE / max_exact)
                         * (N_BUCKETS - max_exact)).astype(jnp.int32)
    large = jnp.minimum(large, N_BUCKETS - 1)
    return jnp.where(dist < max_exact, dist, large)


def _branch_bias_reversed(rel_bias, w, d):
    nj = w // d + 1
    dist = (nj - 1 - jnp.arange(nj, dtype=jnp.int32)) * d
    return rel_bias[_rel_bucket(dist)].T.astype(F32)


def _prompt_bias_rows(rev):
    rows = [jnp.pad(r, ((0, 0), (0, 2 * Q_BLOCK - r.shape[1])), constant_values=NEG_INF)
            for r in rev]
    return jnp.stack(rows) * LOG2E


def _sample_bias_table(rev, wb, t_new):
    width = wb + LANES
    per_branch = []
    for (w, d), r in zip(BRANCHES, rev):
        nj = r.shape[1]
        if d > 1:
            fill = jnp.full((N_HEADS, nj, d - 1), NEG_INF, F32)
            r = jnp.concatenate([r[:, :, None], fill], axis=2).reshape(N_HEADS, nj * d)
        rows = []
        for t in range(t_new):
            base = wb + t - (nj - 1) * d
            rows.append(jnp.pad(r, ((0, 0), (base, width - base - nj * d)),
                                constant_values=NEG_INF))
        per_branch.append(jnp.stack(rows))
    x = jnp.stack(per_branch)
    mx = jnp.max(x, axis=0)
    logc = mx + jnp.log(jnp.sum(jnp.exp(x - mx), axis=0))
    return logc.reshape(t_new * N_HEADS, width)


def kernel(x_prompt, x_sample, cache_k_win, cache_v_win, norm1_g, w_in, sgu_ln_g, sgu_ln_b,
           sgu_w, sgu_b, w_out, norm2_g, w_gate, w_up, w_down, rel_bias, final_g):
    depth = w_in.shape[0]
    assert depth == 1
    b, s, _ = x_prompt.shape
    bd, t_new, _ = x_sample.shape
    wb = cache_k_win.shape[2]
    assert bd * t_new == CHUNK and s % (Q_BLOCK * 16) == 0 and wb == WINDOW
    assert (s // Q_BLOCK) % ATTN_GROUP == 0 and s % PREP_ROWS == 0

    l = 0
    w_in_b = w_in[l].astype(BF16)
    g1 = norm1_g[l][None]
    g2 = norm2_g[l][None]
    gf = final_g[None]
    ln_g = sgu_ln_g[l][None]
    ln_b = sgu_ln_b[l][None]

    causal = jnp.tril(jnp.ones((CHUNK, CHUNK), F32))
    wm = sgu_w[l] * causal
    mix_p = wm.astype(BF16)
    mixb_p = jnp.repeat(sgu_b[l].T, HEAD_DIM, axis=1)
    rep = jnp.tile(jnp.eye(t_new, dtype=F32), (bd, 1))
    same_batch = jnp.kron(jnp.eye(bd, dtype=F32), jnp.ones((t_new, t_new), F32))
    mix_s = (jnp.einsum('it,gts,js->gij', rep, wm[:, :t_new, :t_new], rep)
             * same_batch).astype(BF16)
    mixb_s = jnp.tile(mixb_p[:t_new], (bd, 1))

    rev = [_branch_bias_reversed(rel_bias, w, d) for w, d in BRANCHES]

    xp = x_prompt.reshape(b * s, D_MODEL)
    q, k, v, sg, kt_p, vt_p, wo, wg, wu, wd = _proj(
        xp, g1, w_in_b, ln_g, ln_b, mix_p, mixb_p, tm=PROJ_TM, emit_vn=False, emit_t=True, seq_len=s,
        cast=(w_out[l], w_gate[l], w_up[l], w_down[l]))
    att = _attn(q.reshape(b, s, D_ATT), k.reshape(b, s, D_ATT), v.reshape(b, s, D_ATT),
                _prompt_bias_rows(rev))
    nw = min(WINDOW, s)
    new_k_p = kt_p.reshape(1, b, N_HEADS, HEAD_DIM, nw).transpose(0, 1, 4, 2, 3)
    new_v_p = vt_p.reshape(1, b, N_HEADS, HEAD_DIM, nw).transpose(0, 1, 4, 2, 3)

    xs = x_sample.reshape(bd * t_new, D_MODEL)
    qs, _, _, sgs, vn_s, kt_n, vt_n = _proj(xs, g1, w_in_b, ln_g, ln_b, mix_s, mixb_s,
                                            tm=CHUNK, emit_vn=True, emit_t=True, seq_len=CHUNK)
    kt_c = cache_k_win[l].transpose(0, 2, 3, 1).reshape(bd, D_ATT, wb)
    vt_c = cache_v_win[l].transpose(0, 2, 3, 1).reshape(bd, D_ATT, wb)

    logc = _sample_bias_table(rev, wb, t_new)
    weights = (wo, g2, wg, wu, wd, gf)
    y_prompt, att_s, kt_o, vt_o = _ffn_with_sample_side(
        xp, att.reshape(b * s, D_ATT), sg, weights, qs.reshape(bd * t_new // 8, 8, D_ATT),
        kt_n, vt_n, logc, kt_c, vt_c, tm=(b * s) // bd, t_new=t_new)
    y_prompt = y_prompt.reshape(b, s, D_MODEL)
    y_sample = _ffn(xs, att_s.reshape(bd * t_new, D_ATT), sgs, weights,
                    tm=CHUNK).reshape(bd, t_new, D_MODEL)
    new_k_s = kt_o.reshape(1, bd, N_HEADS, HEAD_DIM, wb).transpose(0, 1, 4, 2, 3)
    new_v_s = vt_o.reshape(1, bd, N_HEADS, HEAD_DIM, wb).transpose(0, 1, 4, 2, 3)
    sgu_v = vn_s.reshape(1, bd, t_new, D_SGU)

    return (y_prompt, y_sample, new_k_p, new_v_p, new_k_s, new_v_s, sgu_v)
```

```python
import functools
import math

import jax
import jax.numpy as jnp
from jax import lax
from jax.experimental import pallas as pl
from jax.experimental.pallas import tpu as pltpu

D_MODEL = 1024
N_HEADS = 8
HEAD_DIM = 64
D_ATT = N_HEADS * HEAD_DIM
N_GROUPS = 8
D_SGU = 512
CHUNK = 128
BRANCHES = ((128, 1), (512, 4), (2048, 16))
WINDOW = 2048
Q_BLOCK = 128
N_BUCKETS = 32
MAX_DISTANCE = WINDOW
D_FF = 2816
EPS = 1e-6
NEG_INF = -1e30
LOG2E = math.log2(math.e)

LANES = 128
FF_CHUNK = 256
N_FF_CHUNKS = D_FF // FF_CHUNK
MIB = 1024 * 1024
V7X_VMEM_BYTES = 64 * MIB
VMEM_LIMIT = 48 * MIB
VMEM_LIMIT_FUSED = 60 * MIB
VMEM_LIMIT_ATTN = 56 * MIB
VMEM_LIMIT_PROJ = 56 * MIB
assert max(VMEM_LIMIT, VMEM_LIMIT_FUSED, VMEM_LIMIT_ATTN, VMEM_LIMIT_PROJ) < V7X_VMEM_BYTES
PROJ_TM = 1024
SIDE_TILES_PER_STEP = 8
ATTN_GROUP = 8
PREP_ROWS = 256
BF16_ROWS = 16
FFN_MIN_HALF = 128

F32 = jnp.float32
BF16 = jnp.bfloat16


def _const_spec(shape):
    nd = len(shape)
    return pl.BlockSpec(shape, lambda *_: (0,) * nd, pipeline_mode=pl.Buffered(1))


def _proj_kernel(x_ref, g1_ref, w_ref, lng_ref, lnb_ref, mix_ref, mixb_ref, *refs,
                 tm, emit_vn, emit_t, n_cast):
    cast_in, outs = refs[:n_cast], refs[n_cast:]
    cast_out = outs[len(outs) - n_cast:]
    q_ref, k_ref, v_ref, sg_ref = outs[:4]
    rest = outs[4:len(outs) - n_cast]
    lane = lax.broadcasted_iota(jnp.int32, (1, LANES), 1)
    low = lane < HEAD_DIM

    x = x_ref[...]
    xg = (x * g1_ref[...]).astype(BF16)
    r = lax.rsqrt(jnp.mean(x * x, axis=-1, keepdims=True) + EPS)

    def proj(c0):
        return jnp.dot(xg, w_ref[:, c0:c0 + D_ATT], preferred_element_type=F32) * r

    vg = proj(3 * D_ATT + D_SGU)
    u = proj(3 * D_ATT)
    mu = jnp.mean(vg, axis=-1, keepdims=True)
    dv = vg - mu
    var = jnp.mean(dv * dv, axis=-1, keepdims=True)
    vn = dv * lax.rsqrt(var + EPS) * lng_ref[...] + lnb_ref[...]
    if emit_vn:
        rest[0][...] = vn
    k = proj(D_ATT)
    k_ref[...] = k
    if emit_t:
        kt_ref, vt_ref = rest[-2:]
        kt_ref[0] = k.T

    def gating(c0):
        for s in range(D_SGU // LANES):
            cols = slice(s * LANES, (s + 1) * LANES)
            slab = vn[c0:c0 + CHUNK, cols]
            lo = jnp.where(low, slab, 0.0).astype(BF16)
            hi = jnp.where(low, 0.0, slab).astype(BF16)
            gate = (jnp.dot(mix_ref[2 * s], lo, preferred_element_type=F32)
                    + jnp.dot(mix_ref[2 * s + 1], hi, preferred_element_type=F32)
                    + mixb_ref[:, cols])
            sg_ref[c0:c0 + CHUNK, cols] = (u[c0:c0 + CHUNK, cols] * gate).astype(BF16)

    chunks = list(range(0, tm, CHUNK))
    for c0 in chunks[:len(chunks) // 2]:
        gating(c0)
    v = proj(2 * D_ATT)
    v_ref[...] = v
    if emit_t:
        vt_ref[0] = v.T
    for src, dst in zip(cast_in, cast_out):
        dst[...] = src[...].astype(BF16)
    for c0 in chunks[len(chunks) // 2:]:
        gating(c0)
    q_ref[...] = proj(0)


def _proj(x, g1, w_in, ln_g, ln_b, mix, mixb, *, tm, emit_vn, emit_t, seq_len=None, cast=()):
    t = x.shape[0]
    n_tiles = t // tm
    row = lambda shape: pl.BlockSpec(shape, lambda i: (i, 0))
    out_shape = [jax.ShapeDtypeStruct((t, D_ATT), F32)] * 3 + [jax.ShapeDtypeStruct((t, D_SGU), BF16)]
    out_specs = [row((tm, D_ATT))] * 3 + [row((tm, D_SGU))]
    if emit_vn:
        out_shape.append(jax.ShapeDtypeStruct((t, D_SGU), F32))
        out_specs.append(row((tm, D_SGU)))
    if emit_t:
        tiles_per_seq = seq_len // tm
        win = min(WINDOW, seq_len)
        first = tiles_per_seq - win // tm

        def t_map(i):
            return (i // tiles_per_seq, 0, jnp.maximum(i % tiles_per_seq - first, 0))

        for _ in range(2):
            out_shape.append(jax.ShapeDtypeStruct((t // seq_len, D_ATT, win), F32))
            out_specs.append(pl.BlockSpec((1, D_ATT, tm), t_map))
    cast_specs = []
    for w in cast:
        steps = max(n for n in range(1, n_tiles + 1)
                    if w.shape[0] % n == 0 and (w.shape[0] // n) % BF16_ROWS == 0)
        spec = pl.BlockSpec((w.shape[0] // steps, w.shape[1]),
                            lambda i, steps=steps: (jnp.minimum(i, steps - 1), 0))
        cast_specs.append(spec)
        out_shape.append(jax.ShapeDtypeStruct(w.shape, BF16))
        out_specs.append(spec)
    kern = functools.partial(_proj_kernel, tm=tm, emit_vn=emit_vn, emit_t=emit_t, n_cast=len(cast))
    return pl.pallas_call(
        kern,
        grid=(n_tiles,),
        in_specs=[row((tm, D_MODEL)), _const_spec(g1.shape), _const_spec(w_in.shape),
                  _const_spec(ln_g.shape), _const_spec(ln_b.shape), _const_spec(mix.shape),
                  _const_spec(mixb.shape)] + cast_specs,
        out_specs=out_specs,
        out_shape=out_shape,
        compiler_params=pltpu.CompilerParams(dimension_semantics=("arbitrary",),
                                             vmem_limit_bytes=VMEM_LIMIT_PROJ),
        name="proj_cast" if cast else "proj",
    )(x, g1, w_in, ln_g, ln_b, mix, mixb, *cast)


def _div_pow2(x, n):
    assert n & (n - 1) == 0
    return lax.shift_right_logical(x, n.bit_length() - 1)


def _mod_pow2(x, n):
    assert n & (n - 1) == 0
    return lax.bitwise_and(x, n - 1)


def _attn_kernel(q_ref, k_ref, v_ref, r0_ref, o_ref, tab_ref, acc_ref, m_ref, l_ref,
                 q0_ref, q1_ref, kd_ref, v0_ref, v1_ref, s_ref, mx_ref, *, seq_len):
    hp = pl.program_id(0)
    lane = lax.broadcasted_iota(jnp.int32, (1, LANES), 1)
    low = lane < HEAD_DIM
    contract_last = (((1,), (1,)), ((), ()))
    neg = NEG_INF * LOG2E
    quarter = seq_len // 4
    lo_f = jnp.where(low, 1.0, 0.0).astype(F32)
    hi_f = 1.0 - lo_f
    q_scale = HEAD_DIM ** -0.5 * LOG2E

    @pl.when(pl.program_id(1) == 0)
    def _():
        col = lax.broadcasted_iota(jnp.int32, (Q_BLOCK, 2 * Q_BLOCK), 1)
        for br in range(len(BRANCHES)):
            for h in range(2):
                base = jnp.broadcast_to(r0_ref[br, pl.ds(2 * hp + h, 1), :],
                                        (Q_BLOCK, 2 * Q_BLOCK))
                t = pltpu.roll(base, 0, 1, stride=1, stride_axis=0)
                tab_ref[br, 0, h] = t
                tab_ref[br, 1, h] = jnp.where(col >= Q_BLOCK, t, neg)
        zeros = jnp.zeros((Q_BLOCK, LANES), BF16)
        for slot in range(2):
            kd_ref[slot, 0:Q_BLOCK, :] = zeros
            v0_ref[slot, 0:Q_BLOCK, :] = zeros
            v1_ref[slot, 0:Q_BLOCK, :] = zeros

    def strided(start, n, d):
        return pl.ds(start, n) if d == 1 else pl.ds(start, n, stride=d)

    dilation = [d for _, d in BRANCHES]
    n_groups = seq_len // Q_BLOCK // ATTN_GROUP

    def prep(br):
        d, slot = dilation[br], br % 2
        sub_len = seq_len // d

        def chunk(c, carry):
            de0 = c * PREP_ROWS
            dst = pl.ds(pl.multiple_of(de0, PREP_ROWS), PREP_ROWS)
            dstp = pl.ds(pl.multiple_of(de0 + Q_BLOCK, Q_BLOCK), PREP_ROWS)
            if d <= 4:
                src = strided(_div_pow2(de0, sub_len) + _mod_pow2(de0, sub_len) * d, PREP_ROWS, d)
                qq, kk, vv = q_ref[0, src, :], k_ref[0, src, :], v_ref[0, src, :]
            else:
                res, m0 = _div_pow2(de0, sub_len), _mod_pow2(de0, sub_len)
                src = pl.ds(_mod_pow2(res, 4) * quarter + m0 * (d // 4) + _div_pow2(res, 4),
                            PREP_ROWS, stride=d // 4)
                qq, kk, vv = acc_ref[2, src, :], m_ref[2, src, :], l_ref[2, src, :]
            if d == 4:
                acc_ref[2, dst, :] = qq
                m_ref[2, dst, :] = kk
                l_ref[2, dst, :] = vv
            q0_ref[slot, dst, :] = (qq * (lo_f * q_scale)).astype(BF16)
            q1_ref[slot, dst, :] = (qq * (hi_f * q_scale)).astype(BF16)
            kd_ref[slot, dstp, :] = kk.astype(BF16)
            v0_ref[slot, dstp, :] = (vv * lo_f + hi_f).astype(BF16)
            v1_ref[slot, dstp, :] = (vv * hi_f + lo_f).astype(BF16)
            return carry

        lax.fori_loop(0, seq_len // PREP_ROWS, chunk, 0, unroll=4)

    def score(br, i):
        slot, nb = br % 2, seq_len // dilation[br] // Q_BLOCK
        for u in range(ATTN_GROUP):
            g = i * ATTN_GROUP + u
            rq = pl.ds(pl.multiple_of(g * Q_BLOCK, Q_BLOCK), Q_BLOCK)
            rk = pl.ds(pl.multiple_of(g * Q_BLOCK, Q_BLOCK), 2 * Q_BLOCK)
            first = jnp.where(_mod_pow2(g, nb) == 0, 1, 0)
            kb = kd_ref[slot, rk, :]
            for h, qh_ref in enumerate((q0_ref, q1_ref)):
                s = lax.dot_general(qh_ref[slot, rq, :], kb, contract_last,
                                    preferred_element_type=F32)
                s = s + tab_ref[br, first, h]
                s_ref[u, h] = s
                mx_ref[u, h] = jnp.broadcast_to(jnp.max(s, axis=-1, keepdims=True),
                                                (Q_BLOCK, LANES))

    def finish(br, i):
        d, slot = dilation[br], br % 2
        nb = seq_len // d // Q_BLOCK
        for u in range(ATTN_GROUP):
            g = i * ATTN_GROUP + u
            rk = pl.ds(pl.multiple_of(g * Q_BLOCK, Q_BLOCK), 2 * Q_BLOCK)
            outs = []
            for h, vh_ref in enumerate((v0_ref, v1_ref)):
                m = mx_ref[u, h]
                p = jnp.concatenate([jnp.exp2(s_ref[u, h, :, :LANES] - m),
                                     jnp.exp2(s_ref[u, h, :, LANES:] - m)], axis=1)
                outs.append((jnp.dot(p.astype(BF16), vh_ref[slot, rk, :],
                                     preferred_element_type=F32), m))
            (o0, m0), (o1, m1) = outs
            if d <= 4:
                idx = pl.ds(pl.multiple_of(g * Q_BLOCK, Q_BLOCK), Q_BLOCK)
            else:
                res, blk = _div_pow2(g, nb), _mod_pow2(g, nb)
                idx = pl.ds(_mod_pow2(res, 4) * quarter + blk * (Q_BLOCK * d // 4)
                            + _div_pow2(res, 4), Q_BLOCK, stride=d // 4)
            acc_ref[br, idx, :] = jnp.where(low, o0, o1)
            l_ref[br, idx, :] = pltpu.roll(jnp.where(low, o1, o0), HEAD_DIM, 1)
            m_ref[br, idx, :] = jnp.where(low, m0, m1)

    prep(0)
    score(0, 0)
    for br in range(len(BRANCHES)):
        for i in range(1, n_groups):
            finish(br, i - 1)
            score(br, i)
        if br + 1 < len(BRANCHES):
            prep(br + 1)
            finish(br, n_groups - 1)
            score(br + 1, 0)
        else:
            finish(br, n_groups - 1)

    def merge(c, carry):
        de0 = c * PREP_ROWS
        dil = pl.ds(pl.multiple_of(de0, PREP_ROWS), PREP_ROWS)
        nat = pl.ds(_div_pow2(de0, quarter) + _mod_pow2(de0, quarter) * 4, PREP_ROWS, stride=4)
        ms = (m_ref[0, nat, :], m_ref[1, dil, :], m_ref[2, dil, :])
        m_all = jnp.maximum(jnp.maximum(ms[0], ms[1]), ms[2])
        ws = [jnp.exp2(m - m_all) for m in ms]
        num = (acc_ref[0, nat, :] * ws[0] + acc_ref[1, dil, :] * ws[1] + acc_ref[2, dil, :] * ws[2])
        den = l_ref[0, nat, :] * ws[0] + l_ref[1, dil, :] * ws[1] + l_ref[2, dil, :] * ws[2]
        acc_ref[0, nat, :] = num / den
        return carry

    for c in range(seq_len // PREP_ROWS):
        merge(c, 0)

    rows_per = 512

    def fin(i, c):
        sl = pl.ds(pl.multiple_of(i * rows_per, rows_per), rows_per)
        o_ref[0, sl, :] = acc_ref[0, sl, :].astype(o_ref.dtype)
        return c

    lax.fori_loop(0, seq_len // rows_per, fin, 0)


def _attn(q, k, v, r0):
    b, s, _ = q.shape
    n_br = len(BRANCHES)
    blk = pl.BlockSpec((1, s, LANES), lambda hp, i: (i, 0, hp))
    state = pltpu.VMEM((n_br, s, LANES), F32)
    qd = pltpu.VMEM((2, s, LANES), BF16)
    kd = pltpu.VMEM((2, s + Q_BLOCK, LANES), BF16)
    return pl.pallas_call(
        functools.partial(_attn_kernel, seq_len=s),
        grid=(D_ATT // LANES, b),
        in_specs=[blk, blk, blk, _const_spec(r0.shape)],
        out_specs=blk,
        out_shape=jax.ShapeDtypeStruct((b, s, D_ATT), BF16),
        scratch_shapes=[pltpu.VMEM((n_br, 2, 2, Q_BLOCK, 2 * Q_BLOCK), F32),
                        state, state, state, qd, qd, kd, kd, kd,
                        pltpu.VMEM((ATTN_GROUP, 2, Q_BLOCK, 2 * Q_BLOCK), F32),
                        pltpu.VMEM((ATTN_GROUP, 2, Q_BLOCK, LANES), F32)],
        compiler_params=pltpu.CompilerParams(dimension_semantics=("arbitrary", "arbitrary"),
                                             vmem_limit_bytes=VMEM_LIMIT_ATTN),
        name="attn",
    )(q, k, v, r0)


def _sample_side_steps(b, q_ref, ktn_ref, vtn_ref, logc_ref, kin_ref, vin_ref, kout_ref, vout_ref,
                       att_ref, *, wb, t_new):
    rows = t_new * N_HEADS
    contract_last = (((1,), (1,)), ((), ()))
    lane = lax.broadcasted_iota(jnp.int32, (1, LANES), 1)
    keep = lane < LANES - t_new
    n_tiles = wb // LANES
    st = {}

    def scores():
        shift = lax.bitwise_and(LANES - t_new * b, LANES - 1)
        st["ktn"] = pltpu.roll(ktn_ref[0], shift, 1)
        st["vtn"] = pltpu.roll(vtn_ref[0], shift, 1)
        q_tile = q_ref[0] * (HEAD_DIM ** -0.5)
        per_tile = 8 // t_new
        q = q_tile[0:t_new]
        for i in range(1, per_tile):
            q = jnp.where(lax.rem(b, per_tile) == i, q_tile[i * t_new:(i + 1) * t_new], q)
        qrep = jnp.broadcast_to(q[:, None, :], (t_new, N_HEADS, D_ATT)).reshape(rows, D_ATT)
        row_h = lax.broadcasted_iota(jnp.int32, (rows, D_ATT), 0) % N_HEADS
        col_h = lax.broadcasted_iota(jnp.int32, (rows, D_ATT), 1) // HEAD_DIM
        st["own"] = row_h == col_h
        qbd = jnp.where(st["own"], qrep, 0.0).astype(BF16)
        st["s"] = (jnp.dot(qbd, kin_ref[...].astype(BF16), preferred_element_type=F32)
                   + logc_ref[:, :wb])
        st["sn"] = (jnp.dot(qbd, st["ktn"].astype(BF16), preferred_element_type=F32)
                    + logc_ref[:, wb:])

    def softmax():
        s, sn = st["s"], st["sn"]
        m = jnp.maximum(jnp.max(s, axis=-1, keepdims=True), jnp.max(sn, axis=-1, keepdims=True))
        p = jnp.exp(s - m)
        pn = jnp.exp(sn - m)
        st["l"] = jnp.sum(p, axis=-1, keepdims=True) + jnp.sum(pn, axis=-1, keepdims=True)
        st["p"], st["pn"] = p.astype(BF16), pn.astype(BF16)

    def values():
        o = (lax.dot_general(st["p"], vin_ref[...].astype(BF16), contract_last,
                             preferred_element_type=F32)
             + lax.dot_general(st["pn"], st["vtn"].astype(BF16), contract_last,
                               preferred_element_type=F32))
        o = jnp.where(st["own"], o, 0.0) / st["l"]
        att_ref[0] = jnp.sum(o.reshape(t_new, N_HEADS, D_ATT), axis=1)

    def shift_tiles(src_ref, new_key, dst_ref, j0, j1):
        def run():
            cur = st.get(("cur", new_key))
            if cur is None:
                cur = pltpu.roll(src_ref[:, 0:LANES], LANES - t_new, 1)
            for j in range(j0, j1):
                if j + 1 < n_tiles:
                    nxt = pltpu.roll(src_ref[:, (j + 1) * LANES:(j + 2) * LANES], LANES - t_new, 1)
                else:
                    nxt = pltpu.roll(st[new_key], LANES - t_new, 1)
                dst_ref[:, j * LANES:(j + 1) * LANES] = jnp.where(keep, cur, nxt)
                cur = nxt
            st[("cur", new_key)] = cur
        return run

    per = SIDE_TILES_PER_STEP
    k_steps = [scores] + [shift_tiles(kin_ref, "ktn", kout_ref, j0, min(j0 + per, n_tiles))
                          for j0 in range(0, n_tiles, per)]
    v_steps = [softmax, values] + [shift_tiles(vin_ref, "vtn", vout_ref, j0, min(j0 + per, n_tiles))
                                   for j0 in range(0, n_tiles, per)]
    return k_steps, v_steps


def _ffn_body(x_ref, att_ref, sg_ref, wo_ref, g2_ref, wg_ref, wu_ref, wd_ref, gf_ref,
              y_ref, a_ref, side_steps=()):
    side = list(side_steps)
    assert len(side) < N_FF_CHUNKS
    tm = x_ref.shape[0]
    halves = [slice(0, tm // 2), slice(tm // 2, tm)] if tm >= 2 * FFN_MIN_HALF else [slice(0, tm)]

    def gate_up(rows, c):
        cols = slice(c * FF_CHUNK, (c + 1) * FF_CHUNK)
        h = a_ref[rows, D_FF:]
        g = jnp.dot(h, wg_ref[:, cols], preferred_element_type=F32)
        u = jnp.dot(h, wu_ref[:, cols], preferred_element_type=F32)
        a_ref[rows, cols] = (g * jax.nn.sigmoid(g) * u).astype(BF16)

    x1s = []
    for rows in halves:
        x1 = (x_ref[rows, :]
              + jnp.dot(att_ref[rows, :].astype(BF16), wo_ref[:D_ATT, :], preferred_element_type=F32)
              + jnp.dot(sg_ref[rows, :], wo_ref[D_ATT:, :], preferred_element_type=F32))
        ms = jnp.mean(x1 * x1, axis=-1, keepdims=True)
        x1s.append(x1)
        a_ref[rows, D_FF:] = (x1 * lax.rsqrt(ms + EPS) * g2_ref[...]).astype(BF16)
    for rows in halves:
        gate_up(rows, 0)
    for c in range(1, N_FF_CHUNKS):
        if side:
            side.pop(0)()
        gate_up(slice(0, tm), c)
    for rows, x1 in zip(halves, x1s):
        x2 = x1 + jnp.dot(a_ref[rows, :D_FF], wd_ref[...], preferred_element_type=F32)
        ms2 = jnp.mean(x2 * x2, axis=-1, keepdims=True)
        y_ref[rows, :] = x2 * lax.rsqrt(ms2 + EPS) * gf_ref[...]


def _ffn_kernel(*refs):
    _ffn_body(*refs)


def _ffn_side_kernel(x_ref, att_ref, sg_ref, wo_ref, g2_ref, wg_ref, wu_ref, wd_ref, gf_ref,
                     q_ref, ktn_ref, vtn_ref, logc_ref, kt_hbm, vt_hbm,
                     y_ref, atts_ref, kto_hbm, vto_hbm,
                     a_ref, kin_ref, vin_ref, kout_ref, vout_ref, sems, *, wb, t_new):
    i = pl.program_id(0)
    last = pl.num_programs(0) - 1

    def fetch(j):
        return (pltpu.make_async_copy(kt_hbm.at[j], kin_ref, sems.at[0]),
                pltpu.make_async_copy(vt_hbm.at[j], vin_ref, sems.at[1]))

    def flush(j):
        return (pltpu.make_async_copy(kout_ref, kto_hbm.at[j], sems.at[2]),
                pltpu.make_async_copy(vout_ref, vto_hbm.at[j], sems.at[3]))

    @pl.when(i == 0)
    def _():
        for cp in fetch(0):
            cp.start()

    for cp in fetch(i):
        cp.wait()

    @pl.when(i > 0)
    def _():
        for cp in flush(i - 1):
            cp.wait()

    def then_swap(step, which):
        def run():
            step()
            flush(i)[which].start()
            fetch(jnp.minimum(i + 1, last))[which].start()
        return run

    k_steps, v_steps = _sample_side_steps(i, q_ref, ktn_ref, vtn_ref, logc_ref, kin_ref, vin_ref,
                                          kout_ref, vout_ref, atts_ref, wb=wb, t_new=t_new)
    k_steps[-1] = then_swap(k_steps[-1], 0)
    v_steps[-1] = then_swap(v_steps[-1], 1)
    steps = k_steps + v_steps
    _ffn_body(x_ref, att_ref, sg_ref, wo_ref, g2_ref, wg_ref, wu_ref, wd_ref, gf_ref,
              y_ref, a_ref, side_steps=steps)

    @pl.when(i == last)
    def _():
        for cp in fetch(i) + flush(i):
            cp.wait()


def _ffn_specs(tm, weights):
    row = lambda shape: pl.BlockSpec(shape, lambda i: (i, 0))
    return ([row((tm, D_MODEL)), row((tm, D_ATT)), row((tm, D_SGU))]
            + [_const_spec(w.shape) for w in weights]), row((tm, D_MODEL))


def _ffn(x, att, sg, weights, *, tm):
    t = x.shape[0]
    in_specs, out_spec = _ffn_specs(tm, weights)
    return pl.pallas_call(
        _ffn_kernel,
        grid=(t // tm,),
        in_specs=in_specs,
        out_specs=out_spec,
        out_shape=jax.ShapeDtypeStruct((t, D_MODEL), F32),
        scratch_shapes=[pltpu.VMEM((tm, D_FF + D_MODEL), BF16)],
        compiler_params=pltpu.CompilerParams(dimension_semantics=("arbitrary",),
                                             vmem_limit_bytes=VMEM_LIMIT),
        name="ffn",
    )(x, att, sg, *weights)


def _ffn_with_sample_side(x, att, sg, weights, q_s, ktn, vtn, logc, kt, vt, *, tm, t_new):
    t = x.shape[0]
    bd, _, wb = kt.shape
    assert t // tm == bd
    in_specs, out_spec = _ffn_specs(tm, weights)
    any_spec = pl.BlockSpec(memory_space=pl.ANY)
    window = pltpu.VMEM((D_ATT, wb), F32)
    return pl.pallas_call(
        functools.partial(_ffn_side_kernel, wb=wb, t_new=t_new),
        grid=(bd,),
        in_specs=in_specs + [pl.BlockSpec((1, 8, D_ATT), lambda i: (i // (8 // t_new), 0, 0)),
                             _const_spec(ktn.shape), _const_spec(vtn.shape),
                             _const_spec(logc.shape), any_spec, any_spec],
        out_specs=[out_spec, pl.BlockSpec((1, t_new, D_ATT), lambda i: (i, 0, 0)),
                   any_spec, any_spec],
        out_shape=[jax.ShapeDtypeStruct((t, D_MODEL), F32),
                   jax.ShapeDtypeStruct((bd, t_new, D_ATT), F32),
                   jax.ShapeDtypeStruct(kt.shape, F32), jax.ShapeDtypeStruct(vt.shape, F32)],
        scratch_shapes=[pltpu.VMEM((tm, D_FF + D_MODEL), BF16), window, window, window, window,
                        pltpu.SemaphoreType.DMA((4,))],
        compiler_params=pltpu.CompilerParams(dimension_semantics=("arbitrary",),
                                             vmem_limit_bytes=VMEM_LIMIT_FUSED),
        name="ffn_side",
    )(x, att, sg, *weights, q_s, ktn, vtn, logc, kt, vt)


def _rel_bucket(dist):
    max_exact = N_BUCKETS // 2
    df = jnp.maximum(dist, 1).astype(F32)
    large = max_exact + (jnp.log(df / max_exact) / math.log(MAX_DISTANCE / max_exact)
                         * (N_BUCKETS - max_exact)).astype(jnp.int32)
    large = jnp.minimum(large, N_BUCKETS - 1)
    return jnp.where(dist < max_exact, dist, large)


def _branch_bias_reversed(rel_bias, w, d):
    nj = w // d + 1
    dist = (nj - 1 - jnp.arange(nj, dtype=jnp.int32)) * d
    return rel_bias[_rel_bucket(dist)].T.astype(F32)


def _prompt_bias_rows(rev):
    rows = [jnp.pad(r, ((0, 0), (0, 2 * Q_BLOCK - r.shape[1])), constant_values=NEG_INF)
            for r in rev]
    return jnp.stack(rows) * LOG2E


def _sample_bias_table(rev, wb, t_new):
    width = wb + LANES
    per_branch = []
    for (w, d), r in zip(BRANCHES, rev):
        nj = r.shape[1]
        if d > 1:
            fill = jnp.full((N_HEADS, nj, d - 1), NEG_INF, F32)
            r = jnp.concatenate([r[:, :, None], fill], axis=2).reshape(N_HEADS, nj * d)
        rows = []
        for t in range(t_new):
            base = wb + t - (nj - 1) * d
            rows.append(jnp.pad(r, ((0, 0), (base, width - base - nj * d)),
                                constant_values=NEG_INF))
        per_branch.append(jnp.stack(rows))
    x = jnp.stack(per_branch)
    mx = jnp.max(x, axis=0)
    logc = mx + jnp.log(jnp.sum(jnp.exp(x - mx), axis=0))
    return logc.reshape(t_new * N_HEADS, width)


def kernel(x_prompt, x_sample, cache_k_win, cache_v_win, norm1_g, w_in, sgu_ln_g, sgu_ln_b,
           sgu_w, sgu_b, w_out, norm2_g, w_gate, w_up, w_down, rel_bias, final_g):
    depth = w_in.shape[0]
    assert depth == 1
    b, s, _ = x_prompt.shape
    bd, t_new, _ = x_sample.shape
    wb = cache_k_win.shape[2]
    assert bd * t_new == CHUNK and s % (Q_BLOCK * 16) == 0 and wb == WINDOW
    assert (s // Q_BLOCK) % ATTN_GROUP == 0 and s % PREP_ROWS == 0

    l = 0
    w_in_b = w_in[l].astype(BF16)
    g1 = norm1_g[l][None]
    g2 = norm2_g[l][None]
    gf = final_g[None]
    ln_g = sgu_ln_g[l][None]
    ln_b = sgu_ln_b[l][None]

    causal = jnp.tril(jnp.ones((CHUNK, CHUNK), F32))
    wm = sgu_w[l] * causal
    mix_p = wm.astype(BF16)
    mixb_p = jnp.repeat(sgu_b[l].T, HEAD_DIM, axis=1)
    rep = jnp.tile(jnp.eye(t_new, dtype=F32), (bd, 1))
    same_batch = jnp.kron(jnp.eye(bd, dtype=F32), jnp.ones((t_new, t_new), F32))
    mix_s = (jnp.einsum('it,gts,js->gij', rep, wm[:, :t_new, :t_new], rep)
             * same_batch).astype(BF16)
    mixb_s = jnp.tile(mixb_p[:t_new], (bd, 1))

    rev = [_branch_bias_reversed(rel_bias, w, d) for w, d in BRANCHES]

    xp = x_prompt.reshape(b * s, D_MODEL)
    q, k, v, sg, kt_p, vt_p, wo, wg, wu, wd = _proj(
        xp, g1, w_in_b, ln_g, ln_b, mix_p, mixb_p, tm=PROJ_TM, emit_vn=False, emit_t=True, seq_len=s,
        cast=(w_out[l], w_gate[l], w_up[l], w_down[l]))
    att = _attn(q.reshape(b, s, D_ATT), k.reshape(b, s, D_ATT), v.reshape(b, s, D_ATT),
                _prompt_bias_rows(rev))
    nw = min(WINDOW, s)
    new_k_p = kt_p.reshape(1, b, N_HEADS, HEAD_DIM, nw).transpose(0, 1, 4, 2, 3)
    new_v_p = vt_p.reshape(1, b, N_HEADS, HEAD_DIM, nw).transpose(0, 1, 4, 2, 3)

    xs = x_sample.reshape(bd * t_new, D_MODEL)
    qs, _, _, sgs, vn_s, kt_n, vt_n = _proj(xs, g1, w_in_b, ln_g, ln_b, mix_s, mixb_s,
                                            tm=CHUNK, emit_vn=True, emit_t=True, seq_len=CHUNK)
    kt_c = cache_k_win[l].transpose(0, 2, 3, 1).reshape(bd, D_ATT, wb)
    vt_c = cache_v_win[l].transpose(0, 2, 3, 1).reshape(bd, D_ATT, wb)

    logc = _sample_bias_table(rev, wb, t_new)
    weights = (wo, g2, wg, wu, wd, gf)
    y_prompt, att_s, kt_o, vt_o = _ffn_with_sample_side(
        xp, att.reshape(b * s, D_ATT), sg, weights, qs.reshape(bd * t_new // 8, 8, D_ATT),
        kt_n, vt_n, logc, kt_c, vt_c, tm=(b * s) // bd, t_new=t_new)
    y_prompt = y_prompt.reshape(b, s, D_MODEL)
    y_sample = _ffn(xs, att_s.reshape(bd * t_new, D_ATT), sgs, weights,
                    tm=CHUNK).reshape(bd, t_new, D_MODEL)
    new_k_s = kt_o.reshape(1, bd, N_HEADS, HEAD_DIM, wb).transpose(0, 1, 4, 2, 3)
    new_v_s = vt_o.reshape(1, bd, N_HEADS, HEAD_DIM, wb).transpose(0, 1, 4, 2, 3)
    sgu_v = vn_s.reshape(1, bd, t_new, D_SGU)

    return (y_prompt, y_sample, new_k_p, new_v_p, new_k_s, new_v_s, sgu_v)
```
